```python
import math
import jax
import jax.numpy as jnp
from jax import lax
import numpy as np

D_MODEL = 2048
BATCH = 4
SEQ = 2048
DEPTH = 2
DEC_BATCH = 128
DEC_SEQ = 4
PAST_LEN = 2048
PAGE_SIZE = 128

HEAD_DIM = 128
NSA_HEADS = 8
NSA_KV_HEADS = 2
NSA_HPG = NSA_HEADS // NSA_KV_HEADS
L_CMP = 32
L_SEL = 64
N_SEL = 16
WINDOW = 512
FORCE_BONUS = 1e4
DIFF_HEADS = 8
DIFF_DK = HEAD_DIM // 2
DIFF_DV = HEAD_DIM
ATT_IN = NSA_HEADS * HEAD_DIM + 6 * NSA_KV_HEADS * HEAD_DIM + 3 * NSA_HEADS + 2 * DIFF_HEADS * 2 * DIFF_DK + DIFF_HEADS * DIFF_DV
ATT_MIX = NSA_HEADS * HEAD_DIM + DIFF_HEADS * DIFF_DV
POOL_WINDOWS = (2, 4, 8, 16)
POOL_WIDTH = D_MODEL // 2
POOL_GW = POOL_WIDTH // len(POOL_WINDOWS)
POOL_HIST = max(POOL_WINDOWS) - 1
CONV_WIDTH = D_MODEL // 2
CONV_K = 3
PC_IN = POOL_WIDTH + 3 * CONV_WIDTH
PC_MIX = POOL_WIDTH + CONV_WIDTH
PEER_HEADS = 8
PEER_NKEYS = 128
PEER_DK = 256
PEER_TOPK = 16
N_EXPERTS = PEER_NKEYS ** 2
N_ATT_LAYERS = (DEPTH + 1) // 2
N_PC_LAYERS = DEPTH // 2
Q_BLOCK = 128
SEL_ROWS = 256
PEER_BLOCK = 128
EPS = 1e-6
NEG = -1e30

kernel_name = 'hybrid_nsa_diff_pool_conv_peer_step'


def rms_norm(x, g):
    xf = x.astype(jnp.float32)
    y = xf * lax.rsqrt(jnp.mean(xf * xf, axis=-1, keepdims=True) + EPS)
    return (y * g.astype(jnp.float32)).astype(x.dtype)


def alibi_slopes(n):
    return jnp.asarray([2.0 ** (-8.0 * (h + 1) / n) for h in range(n)], dtype=jnp.float32)


def masked_softmax(s, mask):
    p = jax.nn.softmax(jnp.where(mask, s, NEG), axis=-1)
    return jnp.where(mask, p, 0.0)


def att_project(h, w_in, q_gain, k_gain, dq_gain, dk_gain):
    B, T, _ = h.shape
    z = h @ w_in
    o1 = NSA_HEADS * HEAD_DIM
    o2 = o1 + 6 * NSA_KV_HEADS * HEAD_DIM
    o3 = o2 + 3 * NSA_HEADS
    o4 = o3 + DIFF_HEADS * 2 * DIFF_DK
    o5 = o4 + DIFF_HEADS * 2 * DIFF_DK
    q = rms_norm(z[..., :o1].reshape(B, T, NSA_KV_HEADS, NSA_HPG, HEAD_DIM), q_gain) * (HEAD_DIM ** -0.5)
    kv = z[..., o1:o2].reshape(B, T, 3, 2, NSA_KV_HEADS, HEAD_DIM)
    k = rms_norm(kv[:, :, :, 0], k_gain[:, None, :])
    kv = jnp.stack([k, kv[:, :, :, 1]], axis=3)
    rows = kv[:, :, :2].reshape(B, T, 4, NSA_KV_HEADS, HEAD_DIM)
    win = kv[:, :, 2]
    gates = jax.nn.sigmoid(z[..., o2:o3].reshape(B, T, NSA_KV_HEADS, NSA_HPG, 3))
    dq = rms_norm(z[..., o3:o4].reshape(B, T, DIFF_HEADS, 2, DIFF_DK), dq_gain) * (DIFF_DK ** -0.5)
    dk = rms_norm(z[..., o4:o5].reshape(B, T, DIFF_HEADS, 2, DIFF_DK), dk_gain)
    dv = z[..., o5:].reshape(B, T, DIFF_HEADS, DIFF_DV)
    drows = jnp.stack([dk.reshape(B, T, DIFF_HEADS, 2 * DIFF_DK), dv], axis=2)
    return q, gates, rows, win, dq, drows


def compress_blocks(k, w):
    B, L, G, Dh = k.shape
    nc = L // L_CMP
    kb = k[:, :nc * L_CMP].reshape(B, nc, L_CMP, G, Dh)
    return jnp.einsum('bnjgd,jd->bngd', kb, w)


def select_blocks(k):
    B, L, G, Dh = k.shape
    ns = -(-L // L_SEL)
    k = jnp.pad(k, ((0, 0), (0, ns * L_SEL - L), (0, 0), (0, 0)))
    return k.reshape(B, ns, L_SEL, G, Dh).transpose(0, 3, 1, 2, 4)


def nsa_chunk(q, gates, t_q, kc, vc, pc, ks, vs, kw, vw, pw, slopes):
    B, Q, G, Hg, Dh = q.shape
    ns = ks.shape[2]
    sl = slopes.reshape(1, G, Hg, 1, 1)
    dist_c = (t_q[:, None] - pc[None, :]).astype(jnp.float32)
    s_c = jnp.einsum('bqghd,bcgd->bghqc', q, kc).astype(jnp.float32) - sl * dist_c
    p_c = masked_softmax(s_c, pc[None, :] <= t_q[:, None])
    o_c = jnp.einsum('bghqc,bcgd->bqghd', p_c.astype(vc.dtype), vc)
    ratio = L_SEL // L_CMP
    imp = p_c.sum(axis=2)
    imp = jnp.pad(imp, ((0, 0), (0, 0), (0, 0), (0, ns * ratio - imp.shape[-1])))
    imp = imp.reshape(B, G, Q, ns, ratio).sum(-1)
    j = jnp.arange(ns)[None, :]
    cur = (t_q // L_SEL)[:, None]
    forced = (j == 0) | (j == cur) | (j == cur - 1)
    valid = j * L_SEL <= t_q[:, None]
    score = jnp.where(valid, imp + jnp.where(forced, FORCE_BONUS, 0.0), NEG)
    _, idx = lax.top_k(score, min(N_SEL, ns))
    bi = jnp.arange(B)[:, None, None, None]
    gi = jnp.arange(G)[None, :, None, None]
    k_s = ks[bi, gi, idx].reshape(B, G, Q, -1, Dh)
    v_s = vs[bi, gi, idx].reshape(B, G, Q, -1, Dh)
    pos_s = (idx[..., None] * L_SEL + jnp.arange(L_SEL)).reshape(B, G, Q, -1)
    dist_s = t_q[None, None, :, None] - pos_s
    s_s = jnp.einsum('bqghd,bgqnd->bghqn', q, k_s).astype(jnp.float32) - sl * dist_s.astype(jnp.float32)[:, :, None]
    p_s = masked_softmax(s_s, (dist_s >= 0)[:, :, None])
    o_s = jnp.einsum('bghqn,bgqnd->bqghd', p_s.astype(v_s.dtype), v_s)
    dist_w = t_q[:, None] - pw[None, :]
    s_w = jnp.einsum('bqghd,bwgd->bghqw', q, kw).astype(jnp.float32) - sl * dist_w.astype(jnp.float32)
    p_w = masked_softmax(s_w, (dist_w >= 0) & (dist_w < WINDOW) & (pw[None, :] >= 0))
    o_w = jnp.einsum('bghqw,bwgd->bqghd', p_w.astype(vw.dtype), vw)
    o = gates[..., 0:1] * o_c + gates[..., 1:2] * o_s + gates[..., 2:3] * o_w
    return o.reshape(B, Q, G * Hg * Dh)


def nsa_attend(q, gates, t0, rows_all, win_ext, cmp_w, slopes):
    B, Tq = q.shape[:2]
    kc = compress_blocks(rows_all[:, :, 0], cmp_w[0])
    vc = compress_blocks(rows_all[:, :, 1], cmp_w[1])
    pc = jnp.arange(kc.shape[1]) * L_CMP + (L_CMP - 1)
    ks = select_blocks(rows_all[:, :, 2])
    vs = select_blocks(rows_all[:, :, 3])
    qb = math.gcd(Tq, max(1, SEL_ROWS // B))

    def body(i):
        s = i * qb
        q_c = lax.dynamic_slice_in_dim(q, s, qb, axis=1)
        g_c = lax.dynamic_slice_in_dim(gates, s, qb, axis=1)
        w_c = lax.dynamic_slice_in_dim(win_ext, s, qb + WINDOW, axis=1)
        t_c = t0 + s + jnp.arange(qb)
        pw = t0 + s - WINDOW + jnp.arange(qb + WINDOW)
        return nsa_chunk(q_c, g_c, t_c, kc, vc, pc, ks, vs, w_c[:, :, 0], w_c[:, :, 1], pw, slopes)

    o = lax.map(body, jnp.arange(Tq // qb))
    return jnp.moveaxis(o, 0, 1).reshape(B, Tq, NSA_HEADS * HEAD_DIM)


def diff_attend(q, k_all, v_all, t0, lam, slopes):
    B, Tq = q.shape[:2]
    L = k_all.shape[1]
    qb = math.gcd(Tq, Q_BLOCK)
    k_pos = jnp.arange(L)

    def body(i):
        s = i * qb
        q_c = lax.dynamic_slice_in_dim(q, s, qb, axis=1)
        t_c = t0 + s + jnp.arange(qb)
        dist = t_c[:, None] - k_pos[None, :]
        sc = jnp.einsum('bqhmd,blhmd->bhmql', q_c, k_all).astype(jnp.float32)
        sc = sc - slopes[None, :, None, None, None] * dist.astype(jnp.float32)
        p = masked_softmax(sc, dist >= 0)
        a = p[:, :, 0] - lam * p[:, :, 1]
        return jnp.einsum('bhql,blhd->bqhd', a.astype(v_all.dtype), v_all)

    o = lax.map(body, jnp.arange(Tq // qb))
    return jnp.moveaxis(o, 0, 1).reshape(B, Tq, DIFF_HEADS, DIFF_DV)


def attention_mixer(h, t0, nsa_past, win_buf, diff_past, w_in, w_out, q_gain, k_gain, cmp_w,
                    dq_gain, dk_gain, lam_vec, out_gain, lam_init):
    B, T, _ = h.shape
    q, gates, rows, win, dq, drows = att_project(h, w_in, q_gain, k_gain, dq_gain, dk_gain)
    if nsa_past is None:
        rows_all, d_all = rows, drows
        hist = jnp.zeros((B, WINDOW) + win.shape[2:], win.dtype)
        keep = min(WINDOW, T)
    else:
        rows_all = jnp.concatenate([nsa_past, rows], axis=1)
        d_all = jnp.concatenate([diff_past, drows], axis=1)
        keep = win_buf.shape[1]
        hist = jnp.pad(win_buf, ((0, 0), (WINDOW - keep, 0), (0, 0), (0, 0), (0, 0)))
    win_ext = jnp.concatenate([hist, win], axis=1)
    o_nsa = nsa_attend(q, gates, t0, rows_all, win_ext, cmp_w, alibi_slopes(NSA_HEADS))
    lv = lam_vec.astype(jnp.float32)
    lam = jnp.exp(jnp.sum(lv[0] * lv[1])) - jnp.exp(jnp.sum(lv[2] * lv[3])) + lam_init
    k_all = d_all[:, :, 0].reshape(B, -1, DIFF_HEADS, 2, DIFF_DK)
    o_diff = diff_attend(dq, k_all, d_all[:, :, 1], t0, lam, alibi_slopes(DIFF_HEADS))
    o_diff = (rms_norm(o_diff, out_gain) * (1.0 - lam_init)).reshape(B, T, DIFF_HEADS * DIFF_DV)
    out = jnp.concatenate([o_nsa, o_diff], axis=-1) @ w_out
    return out, rows, win_ext[:, -keep:], drows


def pool_mix(p_ext, t0, pool_w, pool_scale):
    B, n, _ = p_ext.shape
    T = n - POOL_HIST
    pf = p_ext.astype(jnp.float32)
    cs = jnp.pad(jnp.cumsum(pf, axis=1), ((0, 0), (1, 0), (0, 0)))
    t = t0 + jnp.arange(T)
    groups = []
    for g, w in enumerate(POOL_WINDOWS):
        c = slice(g * POOL_GW, (g + 1) * POOL_GW)
        win_sum = cs[:, POOL_HIST + 1:POOL_HIST + 1 + T, c] - cs[:, POOL_HIST + 1 - w:POOL_HIST + 1 - w + T, c]
        cnt = jnp.minimum(w, t + 1).astype(jnp.float32)[None, :, None]
        groups.append(win_sum / cnt - pf[:, POOL_HIST:, c])
    m = jnp.stack(groups, axis=2).astype(p_ext.dtype)
    y = jnp.einsum('btgc,gcd->btgd', m, pool_w).reshape(B, T, POOL_WIDTH)
    return y * pool_scale


def short_conv(u_ext, conv_w):
    T = u_ext.shape[1] - (CONV_K - 1)
    return sum(u_ext[:, j:j + T] * conv_w[j] for j in range(CONV_K))


def pool_conv_mixer(h, t0, pool_hist, conv_hist, w_in, w_out, pool_w, pool_scale, conv_w):
    B, T, _ = h.shape
    z = h @ w_in
    p = z[..., :POOL_WIDTH]
    b_gate = z[..., POOL_WIDTH:POOL_WIDTH + CONV_WIDTH]
    c_gate = z[..., POOL_WIDTH + CONV_WIDTH:POOL_WIDTH + 2 * CONV_WIDTH]
    u = c_gate * z[..., POOL_WIDTH + 2 * CONV_WIDTH:]
    if pool_hist is None:
        pool_hist = jnp.zeros((B, POOL_HIST, POOL_WIDTH), p.dtype)
        conv_hist = jnp.zeros((B, CONV_K - 1, CONV_WIDTH), u.dtype)
    p_ext = jnp.concatenate([pool_hist, p], axis=1)
    u_ext = jnp.concatenate([conv_hist, u], axis=1)
    y_pool = pool_mix(p_ext, t0, pool_w, pool_scale)
    y_conv = b_gate * short_conv(u_ext, conv_w)
    out = jnp.concatenate([y_pool, y_conv], axis=-1) @ w_out
    return out, p_ext[:, -POOL_HIST:], u_ext[:, -(CONV_K - 1):]


def peer_ffn(h, wq, keys, u_tab, v_tab):
    B, T, D = h.shape
    n = B * T
    hf = h.reshape(n, D)
    q = (hf @ wq).reshape(n, PEER_HEADS, 2, PEER_DK // 2)
    s = jnp.einsum('nhpd,hpkd->nhpk', q, keys).astype(jnp.float32)
    sv, si = lax.top_k(s, PEER_TOPK)
    comb = (sv[:, :, 0, :, None] + sv[:, :, 1, None, :]).reshape(n, PEER_HEADS, PEER_TOPK * PEER_TOPK)
    cv, ci = lax.top_k(comb, PEER_TOPK)
    i1 = jnp.take_along_axis(si[:, :, 0], ci // PEER_TOPK, axis=-1)
    i2 = jnp.take_along_axis(si[:, :, 1], ci % PEER_TOPK, axis=-1)
    experts = i1 * PEER_NKEYS + i2
    gate = jax.nn.softmax(cv, axis=-1)
    blk = math.gcd(n, PEER_BLOCK)

    def body(args):
        x_c, e_c, g_c = args
        act = jax.nn.gelu(jnp.einsum('nd,nhkd->nhk', x_c, u_tab[e_c]).astype(jnp.float32), approximate=False)
        return jnp.einsum('nhk,nhkd->nd', (g_c * act).astype(x_c.dtype), v_tab[e_c])

    out = lax.map(body, (hf.reshape(n // blk, blk, D),
                         experts.reshape(n // blk, blk, PEER_HEADS, PEER_TOPK),
                         gate.reshape(n // blk, blk, PEER_HEADS, PEER_TOPK)))
    return out.reshape(B, T, D)


def setup_inputs(seed: int = 0) -> dict:
    key = jax.random.key(seed)
    keys = iter(jax.random.split(key, 40))

    def nrm(shape, scale=1.0):
        return jax.random.normal(next(keys), shape, jnp.float32) * scale

    def gain(shape):
        return 1.0 + 0.02 * jax.random.normal(next(keys), shape, jnp.float32)

    na, npc = N_ATT_LAYERS, N_PC_LAYERS
    n_pages = PAST_LEN // PAGE_SIZE
    n_used = DEC_BATCH * n_pages
    n_phys = n_used + n_used // 4
    wb = min(WINDOW, PAST_LEN)
    page_table = jax.random.permutation(next(keys), n_phys)[:n_used].reshape(DEC_BATCH, n_pages).astype(jnp.int32)
    return {
        'x_prompt': nrm((BATCH, SEQ, D_MODEL)),
        'x_sample': nrm((DEC_BATCH, DEC_SEQ, D_MODEL)),
        'cache_nsa': nrm((na, n_phys, PAGE_SIZE, 4, NSA_KV_HEADS, HEAD_DIM)),
        'cache_diff': nrm((na, n_phys, PAGE_SIZE, 2, DIFF_HEADS, HEAD_DIM)),
        'state_nsa_win': nrm((na, DEC_BATCH, wb, 2, NSA_KV_HEADS, HEAD_DIM)),
        'state_pool': nrm((npc, DEC_BATCH, POOL_HIST, POOL_WIDTH)),
        'state_conv': nrm((npc, DEC_BATCH, CONV_K - 1, CONV_WIDTH)),
        'page_table': page_table,
        'att_norm_g': gain((na, D_MODEL)),
        'att_w_in': nrm((na, D_MODEL, ATT_IN), D_MODEL ** -0.5),
        'att_w_out': nrm((na, ATT_MIX, D_MODEL), ATT_MIX ** -0.5),
        'nsa_q_gain': gain((na, HEAD_DIM)),
        'nsa_k_gain': gain((na, 3, HEAD_DIM)),
        'nsa_cmp_w': (1.0 + 0.1 * nrm((na, 2, L_CMP, HEAD_DIM))) / L_CMP,
        'diff_q_gain': gain((na, 2, DIFF_DK)),
        'diff_k_gain': gain((na, 2, DIFF_DK)),
        'diff_lambda': nrm((na, 4, DIFF_DK), 0.1),
        'diff_out_gain': gain((na, DIFF_DV)),
        'pc_norm_g': gain((npc, D_MODEL)),
        'pc_w_in': nrm((npc, D_MODEL, PC_IN), D_MODEL ** -0.5),
        'pc_w_out': nrm((npc, PC_MIX, D_MODEL), PC_MIX ** -0.5),
        'pool_w': nrm((npc, len(POOL_WINDOWS), POOL_GW, POOL_GW), POOL_GW ** -0.5),
        'pool_scale': gain((npc, POOL_WIDTH)),
        'conv_w': nrm((npc, CONV_K, CONV_WIDTH), CONV_K ** -0.5),
        'ffn_norm_g': gain((DEPTH, D_MODEL)),
        'peer_wq': nrm((DEPTH, D_MODEL, PEER_HEADS * PEER_DK), D_MODEL ** -0.5),
        'peer_keys': nrm((DEPTH, PEER_HEADS, 2, PEER_NKEYS, PEER_DK // 2), (PEER_DK // 2) ** -0.5),
        'peer_u': nrm((DEPTH, N_EXPERTS, D_MODEL), D_MODEL ** -0.5),
        'peer_v': nrm((DEPTH, N_EXPERTS, D_MODEL), PEER_HEADS ** -0.5),
    }


def reference(x_prompt, x_sample, cache_nsa, cache_diff, state_nsa_win, state_pool, state_conv, page_table,
              att_norm_g, att_w_in, att_w_out, nsa_q_gain, nsa_k_gain, nsa_cmp_w, diff_q_gain, diff_k_gain,
              diff_lambda, diff_out_gain, pc_norm_g, pc_w_in, pc_w_out, pool_w, pool_scale, conv_w,
              ffn_norm_g, peer_wq, peer_keys, peer_u, peer_v):
    xp, xs = x_prompt, x_sample
    db = x_sample.shape[0]
    past_len = page_table.shape[1] * cache_nsa.shape[2]
    nsa_p, nsa_s, win_p, win_s, dif_p, dif_s = [], [], [], [], [], []
    pool_p, pool_s, conv_p, conv_s = [], [], [], []
    for li in range(DEPTH):
        i = li // 2
        if li % 2 == 0:
            lam_init = 0.8 - 0.6 * math.exp(-0.3 * li)
            wts = (att_w_in[i], att_w_out[i], nsa_q_gain[i], nsa_k_gain[i], nsa_cmp_w[i],
                   diff_q_gain[i], diff_k_gain[i], diff_lambda[i], diff_out_gain[i], lam_init)
            mix_p, rows_p, wnew_p, drows_p = attention_mixer(rms_norm(xp, att_norm_g[i]), 0, None, None, None, *wts)
            nsa_past = cache_nsa[i][page_table].reshape((db, past_len) + cache_nsa.shape[3:])
            diff_past = cache_diff[i][page_table].reshape((db, past_len) + cache_diff.shape[3:])
            mix_s, rows_s, wnew_s, drows_s = attention_mixer(rms_norm(xs, att_norm_g[i]), past_len, nsa_past,
                                                             state_nsa_win[i], diff_past, *wts)
            nsa_p.append(rows_p)
            nsa_s.append(rows_s)
            win_p.append(wnew_p)
            win_s.append(wnew_s)
            dif_p.append(drows_p)
            dif_s.append(drows_s)
        else:
            wts = (pc_w_in[i], pc_w_out[i], pool_w[i], pool_scale[i], conv_w[i])
            mix_p, ph_p, ch_p = pool_conv_mixer(rms_norm(xp, pc_norm_g[i]), 0, None, None, *wts)
            mix_s, ph_s, ch_s = pool_conv_mixer(rms_norm(xs, pc_norm_g[i]), past_len, state_pool[i], state_conv[i], *wts)
            pool_p.append(ph_p)
            pool_s.append(ph_s)
            conv_p.append(ch_p)
            conv_s.append(ch_s)
        xp = xp + mix_p
        xs = xs + mix_s
        xp = xp + peer_ffn(rms_norm(xp, ffn_norm_g[li]), peer_wq[li], peer_keys[li], peer_u[li], peer_v[li])
        xs = xs + peer_ffn(rms_norm(xs, ffn_norm_g[li]), peer_wq[li], peer_keys[li], peer_u[li], peer_v[li])
    return (xp, xs, jnp.stack(nsa_p), jnp.stack(nsa_s), jnp.stack(win_p), jnp.stack(win_s),
            jnp.stack(dif_p), jnp.stack(dif_s), jnp.stack(pool_p), jnp.stack(pool_s),
            jnp.stack(conv_p), jnp.stack(conv_s))
```

```python
import functools
import math

import jax
import jax.numpy as jnp
from jax import lax
from jax.experimental import pallas as pl
from jax.experimental.pallas import tpu as pltpu

F32 = jnp.float32
BF = jnp.bfloat16
I32 = jnp.int32

HEAD_DIM = 128
NSA_HEADS = 8
NSA_KV_HEADS = 2
NSA_HPG = NSA_HEADS // NSA_KV_HEADS
L_CMP = 32
L_SEL = 64
N_SEL = 16
WINDOW = 512
FORCE_BONUS = 1e4
DIFF_HEADS = 8
DIFF_DK = HEAD_DIM // 2
POOL_WINDOWS = (2, 4, 8, 16)
POOL_HIST = max(POOL_WINDOWS) - 1
CONV_K = 3
PEER_HEADS = 8
PEER_NKEYS = 128
PEER_TOPK = 16
EPS = 1e-6
NEG = -1e30
LANES = 128
HALO = 16
VMEM_LIMIT = 56 * 1024 * 1024

_NT = (((1,), (1,)), ((), ()))


def _params(*sem):
    return pltpu.CompilerParams(dimension_semantics=sem, vmem_limit_bytes=VMEM_LIMIT)


def _dot_nt(a, b):
    return lax.dot_general(a, b, _NT, preferred_element_type=F32)


def _dot(a, b):
    return jnp.dot(a, b, preferred_element_type=F32)


def _pow2_neg(k):
    return lax.bitcast_convert_type((127 - k) << 23, F32)


def _rms(x, g):
    return x * lax.rsqrt(jnp.mean(x * x, axis=-1, keepdims=True) + EPS) * g


def _rms_halves(x, g):
    lo = lax.broadcasted_iota(I32, x.shape, 1) < DIFF_DK
    x2 = x * x
    s_lo = jnp.sum(jnp.where(lo, x2, 0.0), axis=-1, keepdims=True)
    s_hi = jnp.sum(jnp.where(lo, 0.0, x2), axis=-1, keepdims=True)
    ms = jnp.where(lo, s_lo, s_hi) * (1.0 / DIFF_DK)
    return x * lax.rsqrt(ms + EPS) * g


def _msoftmax(s, mask):
    sm = jnp.where(mask, s, NEG)
    m = jnp.max(sm, axis=-1, keepdims=True)
    e = jnp.where(mask, jnp.exp(sm - m), 0.0)
    d = jnp.sum(e, axis=-1, keepdims=True)
    return e / jnp.where(d > 0.0, d, 1.0)


def _msoftmax2(s1, m1, s2, m2):
    a1 = jnp.where(m1, s1, NEG)
    a2 = jnp.where(m2, s2, NEG)
    m = jnp.maximum(jnp.max(a1, axis=-1, keepdims=True), jnp.max(a2, axis=-1, keepdims=True))
    e1 = jnp.where(m1, jnp.exp(a1 - m), 0.0)
    e2 = jnp.where(m2, jnp.exp(a2 - m), 0.0)
    d = jnp.sum(e1, axis=-1, keepdims=True) + jnp.sum(e2, axis=-1, keepdims=True)
    inv = 1.0 / jnp.where(d > 0.0, d, 1.0)
    return e1 * inv, e2 * inv


def _mm_body(*refs, norm, has_res):
    if norm:
        x_ref, g_ref, w_ref = refs[:3]
        rest = refs[3:]
    else:
        x_ref, w_ref = refs[:2]
        rest = refs[2:]
    if has_res:
        r_ref, o_ref, xb_ref = rest
    else:
        o_ref, xb_ref = rest

    @pl.when(pl.program_id(1) == 0)
    def _():
        x = x_ref[...]
        if norm:
            x = _rms(x.astype(F32), g_ref[...])
        xb_ref[...] = x.astype(BF)

    y = _dot(xb_ref[...], w_ref[...].astype(BF))
    if has_res:
        y = y + r_ref[...]
    o_ref[...] = y.astype(o_ref.dtype)


def _mm(x, w, *, gain=None, res=None, tm, tn, name):
    m, k = x.shape
    n = w.shape[1]
    assert m % tm == 0 and n % tn == 0
    norm = gain is not None
    in_specs = [pl.BlockSpec((tm, k), lambda i, j: (i, 0))]
    args = [x]
    if norm:
        in_specs.append(pl.BlockSpec((1, k), lambda i, j: (0, 0)))
        args.append(gain.reshape(1, k))
    in_specs.append(pl.BlockSpec((k, tn), lambda i, j: (0, j)))
    args.append(w)
    if res is not None:
        in_specs.append(pl.BlockSpec((tm, tn), lambda i, j: (i, j)))
        args.append(res)
    return pl.pallas_call(
        functools.partial(_mm_body, norm=norm, has_res=res is not None),
        grid=(m // tm, n // tn),
        in_specs=in_specs,
        out_specs=pl.BlockSpec((tm, tn), lambda i, j: (i, j)),
        out_shape=jax.ShapeDtypeStruct((m, n), F32),
        scratch_shapes=[pltpu.VMEM((tm, k), BF)],
        compiler_params=_params("parallel", "arbitrary"),
        name=name,
    )(*args)


_Q0 = 0
_KV0 = NSA_HEADS * HEAD_DIM
_DQ0 = _KV0 + 6 * NSA_KV_HEADS * HEAD_DIM
_DK0 = _DQ0 + DIFF_HEADS * HEAD_DIM
_DV0 = _DK0 + DIFF_HEADS * HEAD_DIM
_GT0 = _DV0 + DIFF_HEADS * HEAD_DIM
_ZW = _GT0 + NSA_KV_HEADS * LANES


def _att_post_body(z_ref, qg_ref, kg_ref, dqg_ref, dkg_ref,
                   qn_ref, rows_ref, win_ref, gate_ref, dqn_ref, drows_ref):
    hd = HEAD_DIM
    qg = qg_ref[...]
    for h in range(NSA_HEADS):
        x = z_ref[:, _Q0 + h * hd:_Q0 + (h + 1) * hd]
        qn_ref[:, h * hd:(h + 1) * hd] = (_rms(x, qg) * (HEAD_DIM ** -0.5)).astype(qn_ref.dtype)
    gw = NSA_KV_HEADS * hd
    for br in range(3):
        for g in range(NSA_KV_HEADS):
            ko = _KV0 + br * 2 * gw + g * hd
            k = _rms(z_ref[:, ko:ko + hd], kg_ref[br:br + 1, :])
            v = z_ref[:, ko + gw:ko + gw + hd]
            if br < 2:
                rows_ref[:, br * 2 * gw + g * hd:br * 2 * gw + (g + 1) * hd] = k
                rows_ref[:, br * 2 * gw + gw + g * hd:br * 2 * gw + gw + (g + 1) * hd] = v
            else:
                win_ref[:, g * hd:(g + 1) * hd] = k
                win_ref[:, gw + g * hd:gw + (g + 1) * hd] = v
    gate_ref[...] = jax.nn.sigmoid(z_ref[:, _GT0:_ZW])
    dqg = dqg_ref[...]
    dkg = dkg_ref[...]
    for h in range(DIFF_HEADS):
        x = z_ref[:, _DQ0 + h * hd:_DQ0 + (h + 1) * hd]
        dqn_ref[:, h * hd:(h + 1) * hd] = (_rms_halves(x, dqg) * (DIFF_DK ** -0.5)).astype(dqn_ref.dtype)
        x = z_ref[:, _DK0 + h * hd:_DK0 + (h + 1) * hd]
        drows_ref[:, h * hd:(h + 1) * hd] = _rms_halves(x, dkg)
    dvw = DIFF_HEADS * hd
    drows_ref[:, dvw:2 * dvw] = z_ref[:, _DV0:_DV0 + dvw]


def _att_post(z, q_gain, k_gain, dq_gain, dk_gain, *, tm):
    n = z.shape[0]
    assert n % tm == 0 and z.shape[1] == _ZW
    hd = HEAD_DIM
    widths = (NSA_HEADS * hd, 4 * NSA_KV_HEADS * hd, 2 * NSA_KV_HEADS * hd, NSA_KV_HEADS * LANES,
              DIFF_HEADS * hd, 2 * DIFF_HEADS * hd)
    dtypes = (BF, F32, F32, F32, BF, F32)
    small = lambda r: pl.BlockSpec((r, hd), lambda i: (0, 0))
    return pl.pallas_call(
        _att_post_body,
        grid=(n // tm,),
        in_specs=[pl.BlockSpec((tm, _ZW), lambda i: (i, 0)), small(1), small(3), small(1), small(1)],
        out_specs=[pl.BlockSpec((tm, w), lambda i: (i, 0)) for w in widths],
        out_shape=[jax.ShapeDtypeStruct((n, w), d) for w, d in zip(widths, dtypes)],
        compiler_params=_params("parallel"),
        name="att_post",
    )(z, q_gain.reshape(1, hd), k_gain, dq_gain.reshape(1, hd), dk_gain.reshape(1, hd))


def _compress(x, w):
    length, c = x.shape
    npair = length // (2 * L_CMP)
    x3 = x.reshape(npair, 2 * L_CMP, c)
    even = jnp.sum(x3[:, :L_CMP, :] * w, axis=1)
    odd = jnp.sum(x3[:, L_CMP:, :] * w, axis=1)
    pad = jnp.zeros((LANES - 2 * npair, c), F32)
    return jnp.concatenate([even, odd, pad], axis=0)


def _nsa_cmp_body(x_ref, w_ref, o_ref):
    o_ref[...] = _compress(x_ref[...], w_ref[...])


def _nsa_cmp(rows, cmp_w, *, batch, seq):
    gw = NSA_KV_HEADS * HEAD_DIM
    assert seq % (2 * L_CMP) == 0 and seq // L_CMP <= LANES
    w = jnp.concatenate([jnp.tile(cmp_w[0], (1, NSA_KV_HEADS)), jnp.tile(cmp_w[1], (1, NSA_KV_HEADS))], axis=1)
    return pl.pallas_call(
        _nsa_cmp_body,
        grid=(batch,),
        in_specs=[pl.BlockSpec((seq, 2 * gw), lambda b: (b, 0)),
                  pl.BlockSpec((L_CMP, 2 * gw), lambda b: (0, 0))],
        out_specs=pl.BlockSpec((None, LANES, 2 * gw), lambda b: (b, 0, 0)),
        out_shape=jax.ShapeDtypeStruct((batch, LANES, 2 * gw), F32),
        compiler_params=_params("parallel"),
        name="nsa_cmp",
    )(rows, w)


def _cmp_positions(ncmp):
    c = lax.broadcasted_iota(I32, (1, LANES), 1)
    half = ncmp // 2
    blk = jnp.where(c < half, 2 * c, 2 * (c - half) + 1)
    return c, blk * L_CMP + (L_CMP - 1), c < ncmp


def _select_blocks(p_heads, t1, ncmp, nsel_blocks):
    c = lax.broadcasted_iota(I32, (1, LANES), 1)
    half = ncmp // 2
    imp = p_heads[0]
    for p in p_heads[1:]:
        imp = imp + p
    imp = jnp.where(c < half, imp + pltpu.roll(imp, LANES - half, 1), 0.0)
    cur = t1 // L_SEL
    forced = (c == 0) | (c == cur) | (c == cur - 1)
    valid = (c * L_SEL <= t1) & (c < nsel_blocks)
    score = jnp.where(valid, imp + jnp.where(forced, FORCE_BONUS, 0.0), NEG)
    rank = jnp.zeros(score.shape, I32)
    for jp in range(nsel_blocks):
        sj = score[:, jp:jp + 1]
        beats = (sj > score) | ((sj == score) & (c > jp))
        rank = rank + beats.astype(I32)
    return (rank < min(N_SEL, nsel_blocks)) & (c < nsel_blocks)


def _expand_blocks(sel01, length):
    j = lax.broadcasted_iota(I32, (LANES, length), 0)
    l = lax.broadcasted_iota(I32, (LANES, length), 1)
    e = jnp.where(l // L_SEL == j, 1.0, 0.0).astype(BF)
    return _dot(sel01.astype(BF), e)


def _nsa_prompt_body(q_ref, gate_ref, kc_ref, vc_ref, ks_ref, vs_ref, kw_ref, vw_ref, o_ref, *, tq, seq):
    g = pl.program_id(1)
    t0 = pl.program_id(2) * tq
    hg = NSA_HPG
    hd = HEAD_DIM
    rows = hg * tq
    q = q_ref[...]
    qs = jnp.concatenate([q[:, h * hd:(h + 1) * hd] for h in range(hg)], axis=0)
    r_io = lax.broadcasted_iota(I32, (rows, 1), 0)
    hrow = r_io // tq
    tpos = t0 + (r_io - hrow * tq)
    slope = _pow2_neg(g * hg + hrow + 1)
    t1 = t0 + lax.broadcasted_iota(I32, (tq, 1), 0)
    ncmp = seq // L_CMP
    nblk = -(-seq // L_SEL)

    _, pc, cvalid = _cmp_positions(ncmp)
    s = _dot_nt(qs, kc_ref[...].astype(BF)) - slope * (tpos - pc).astype(F32)
    p = _msoftmax(s, cvalid & (pc <= tpos))
    o_c = _dot(p.astype(BF), vc_ref[...].astype(BF))
    sel = _select_blocks([p[h * tq:(h + 1) * tq] for h in range(hg)], t1, ncmp, nblk)
    selm = _expand_blocks(jnp.where(sel, 1.0, 0.0), seq)
    selm = jnp.concatenate([selm] * hg, axis=0)

    pos = lax.broadcasted_iota(I32, (1, seq), 1)
    dist = tpos - pos
    s = _dot_nt(qs, ks_ref[...].astype(BF)) - slope * dist.astype(F32)
    p = _msoftmax(s, (dist >= 0) & (selm > 0.5))
    o_s = _dot(p.astype(BF), vs_ref[...].astype(BF))

    wl = WINDOW + tq
    start = pl.multiple_of(jnp.maximum(t0 - WINDOW, 0), tq)
    posw = start + lax.broadcasted_iota(I32, (1, wl), 1)
    dist = tpos - posw
    s = _dot_nt(qs, kw_ref[pl.ds(start, wl), :].astype(BF)) - slope * dist.astype(F32)
    p = _msoftmax(s, (dist >= 0) & (dist < WINDOW))
    o_w = _dot(p.astype(BF), vw_ref[pl.ds(start, wl), :].astype(BF))

    gt = gate_ref[...]
    for h in range(hg):
        rs = slice(h * tq, (h + 1) * tq)
        o = gt[:, 3 * h:3 * h + 1] * o_c[rs] + gt[:, 3 * h + 1:3 * h + 2] * o_s[rs] + gt[:, 3 * h + 2:3 * h + 3] * o_w[rs]
        o_ref[:, h * hd:(h + 1) * hd] = o.astype(o_ref.dtype)


def _nsa_prompt(qn, gates, cmp, rows, win, *, batch, seq, tq):
    hd = HEAD_DIM
    ng = NSA_KV_HEADS
    nq = seq // tq
    assert seq % tq == 0 and tq % LANES == 0 and WINDOW % tq == 0 and seq >= WINDOW + tq
    gq = NSA_HPG * hd
    seq_blk = lambda col: pl.BlockSpec((seq, hd), lambda b, g, i, col=col: (b, col + g))
    return pl.pallas_call(
        functools.partial(_nsa_prompt_body, tq=tq, seq=seq),
        grid=(batch, ng, nq),
        in_specs=[
            pl.BlockSpec((tq, gq), lambda b, g, i: (b * nq + i, g)),
            pl.BlockSpec((tq, LANES), lambda b, g, i: (b * nq + i, g)),
            pl.BlockSpec((None, LANES, hd), lambda b, g, i: (b, 0, g)),
            pl.BlockSpec((None, LANES, hd), lambda b, g, i: (b, 0, ng + g)),
            seq_blk(2 * ng), seq_blk(3 * ng),
            seq_blk(0), seq_blk(ng),
        ],
        out_specs=pl.BlockSpec((tq, gq), lambda b, g, i: (b * nq + i, g)),
        out_shape=jax.ShapeDtypeStruct((batch * seq, ng * gq), BF),
        compiler_params=_params("parallel", "parallel", "arbitrary"),
        name="nsa_prompt",
    )(qn, gates, cmp, cmp, rows, rows, win, win)


def _lambda(lv, lam_init):
    a = jnp.sum(lv[0:1, :] * lv[1:2, :], axis=-1, keepdims=True)
    b = jnp.sum(lv[2:3, :] * lv[3:4, :], axis=-1, keepdims=True)
    return jnp.exp(a) - jnp.exp(b) + lam_init


def _diff_prompt_body(lv_ref, og_ref, q_ref, k_ref, v_ref, o_ref, *, tq, seq, lam_init):
    h = pl.program_id(1)
    t0 = pl.program_id(2) * tq
    q = q_ref[...].astype(F32)
    lo = lax.broadcasted_iota(I32, q.shape, 1) < DIFF_DK
    qs = jnp.concatenate([jnp.where(lo, q, 0.0), jnp.where(lo, 0.0, q)], axis=0).astype(BF)
    r_io = lax.broadcasted_iota(I32, (2 * tq, 1), 0)
    tpos = t0 + jnp.where(r_io >= tq, r_io - tq, r_io)
    slope = _pow2_neg(jnp.full((1, 1), h + 1, I32))
    dist = tpos - lax.broadcasted_iota(I32, (1, seq), 1)
    s = _dot_nt(qs, k_ref[...].astype(BF)) - slope * dist.astype(F32)
    p = _msoftmax(s, dist >= 0)
    lam = _lambda(lv_ref[...], lam_init)
    a = p[:tq] - lam * p[tq:]
    o = _dot(a.astype(BF), v_ref[...].astype(BF))
    o_ref[...] = (_rms(o, og_ref[...]) * (1.0 - lam_init)).astype(o_ref.dtype)


def _diff_prompt(dqn, drows, lam_vec, out_gain, *, batch, seq, tq, lam_init):
    hd = HEAD_DIM
    nh = DIFF_HEADS
    nq = seq // tq
    assert seq % tq == 0
    return pl.pallas_call(
        functools.partial(_diff_prompt_body, tq=tq, seq=seq, lam_init=lam_init),
        grid=(batch, nh, nq),
        in_specs=[
            pl.BlockSpec((4, DIFF_DK), lambda b, h, i: (0, 0)),
            pl.BlockSpec((1, hd), lambda b, h, i: (0, 0)),
            pl.BlockSpec((tq, hd), lambda b, h, i: (b * nq + i, h)),
            pl.BlockSpec((seq, hd), lambda b, h, i: (b, h)),
            pl.BlockSpec((seq, hd), lambda b, h, i: (b, nh + h)),
        ],
        out_specs=pl.BlockSpec((tq, hd), lambda b, h, i: (b * nq + i, h)),
        out_shape=jax.ShapeDtypeStruct((batch * seq, nh * hd), BF),
        compiler_params=_params("parallel", "parallel", "arbitrary"),
        name="diff_prompt",
    )(lam_vec, out_gain.reshape(1, hd), dqn, drows, drows)


def _nsa_sample_body(q_ref, gate_ref, past_ref, wst_ref, nrow_ref, nwin_ref, cw_ref, o_ref, *, past, ts):
    hg = NSA_HPG
    hd = HEAD_DIM
    ng = NSA_KV_HEADS
    rows = q_ref.shape[1]
    tsp = rows // hg
    npad = nrow_ref.shape[0]
    gw = ng * hd
    ncmp = (past + ts) // L_CMP
    nblk = -(-(past + ts) // L_SEL)
    new_blk = past // L_SEL
    r_io = lax.broadcasted_iota(I32, (rows, 1), 0)
    hrow = r_io // tsp
    trow = r_io - hrow * tsp
    tpos = past + trow
    t1 = past + lax.broadcasted_iota(I32, (tsp, 1), 0)
    jn = lax.broadcasted_iota(I32, (1, npad), 1)
    new_ok = (jn <= trow) & (jn < ts)
    dist_new = (trow - jn).astype(F32)
    _, pc, cvalid = _cmp_positions(ncmp)
    pos = lax.broadcasted_iota(I32, (1, past), 1)
    nwst = wst_ref.shape[2]
    posw = past - nwst + lax.broadcasted_iota(I32, (1, nwst), 1)
    for g in range(ng):
        qs = q_ref[g]
        slope = _pow2_neg(g * hg + hrow + 1)
        kc = _compress(past_ref[0, g], cw_ref[0]).astype(BF)
        vc = _compress(past_ref[1, g], cw_ref[1]).astype(BF)
        s = _dot_nt(qs, kc) - slope * (tpos - pc).astype(F32)
        p = _msoftmax(s, cvalid & (pc <= tpos))
        o_c = _dot(p.astype(BF), vc)
        sel = _select_blocks([p[h * tsp:(h + 1) * tsp] for h in range(hg)], t1, ncmp, nblk)
        sel01 = jnp.concatenate([jnp.where(sel, 1.0, 0.0)] * hg, axis=0)
        selm = _expand_blocks(sel01, past)
        sel_new = sel01[:, new_blk:new_blk + 1] > 0.5
        kn = nrow_ref[:, 2 * gw + g * hd:2 * gw + (g + 1) * hd].astype(BF)
        vn = nrow_ref[:, 3 * gw + g * hd:3 * gw + (g + 1) * hd].astype(BF)
        s1 = _dot_nt(qs, past_ref[2, g].astype(BF)) - slope * (tpos - pos).astype(F32)
        s2 = _dot_nt(qs, kn) - slope * dist_new
        p1, p2 = _msoftmax2(s1, selm > 0.5, s2, new_ok & sel_new)
        o_s = _dot(p1.astype(BF), past_ref[3, g].astype(BF)) + _dot(p2.astype(BF), vn)
        kn = nwin_ref[:, g * hd:(g + 1) * hd].astype(BF)
        vn = nwin_ref[:, gw + g * hd:gw + (g + 1) * hd].astype(BF)
        dist = tpos - posw
        s1 = _dot_nt(qs, wst_ref[0, g].astype(BF)) - slope * dist.astype(F32)
        s2 = _dot_nt(qs, kn) - slope * dist_new
        p1, p2 = _msoftmax2(s1, (dist >= 0) & (dist < WINDOW), s2, new_ok)
        o_w = _dot(p1.astype(BF), wst_ref[1, g].astype(BF)) + _dot(p2.astype(BF), vn)
        gt = gate_ref[g]
        o_ref[g] = gt[:, 0:1] * o_c + gt[:, 1:2] * o_s + gt[:, 2:3] * o_w


def _nsa_sample(q, gates, past_kv, win_state, new_rows, new_win, cmp_w, *, past, ts):
    nb, ng, rows, hd = q.shape
    nwst = win_state.shape[3]
    npad = new_rows.shape[1]
    assert past % (2 * L_CMP) == 0 and past % L_SEL == 0 and nwst == WINDOW and (past + ts) // L_CMP == past // L_CMP
    return pl.pallas_call(
        functools.partial(_nsa_sample_body, past=past, ts=ts),
        grid=(nb,),
        in_specs=[
            pl.BlockSpec((None, ng, rows, hd), lambda b: (b, 0, 0, 0)),
            pl.BlockSpec((None, ng, rows, LANES), lambda b: (b, 0, 0, 0)),
            pl.BlockSpec((None, 4, ng, past, hd), lambda b: (b, 0, 0, 0, 0)),
            pl.BlockSpec((None, 2, ng, nwst, hd), lambda b: (b, 0, 0, 0, 0)),
            pl.BlockSpec((None, npad, new_rows.shape[2]), lambda b: (b, 0, 0)),
            pl.BlockSpec((None, npad, new_win.shape[2]), lambda b: (b, 0, 0)),
            pl.BlockSpec((2, L_CMP, hd), lambda b: (0, 0, 0)),
        ],
        out_specs=pl.BlockSpec((None, ng, rows, hd), lambda b: (b, 0, 0, 0)),
        out_shape=jax.ShapeDtypeStruct((nb, ng, rows, hd), F32),
        compiler_params=_params("parallel"),
        name="nsa_sample",
    )(q, gates, past_kv, win_state, new_rows, new_win, cmp_w)


def _diff_sample_body(lv_ref, og_ref, q_ref, past_ref, ndrow_ref, o_ref, *, past, ts, lam_init):
    hd = HEAD_DIM
    nh = DIFF_HEADS
    npad = ndrow_ref.shape[0]
    rows = q_ref.shape[1]
    tsp = rows // 2
    r_io = lax.broadcasted_iota(I32, (rows, 1), 0)
    trow = jnp.where(r_io >= tsp, r_io - tsp, r_io)
    tpos = past + trow
    dist = (tpos - lax.broadcasted_iota(I32, (1, past), 1)).astype(F32)
    jn = lax.broadcasted_iota(I32, (1, npad), 1)
    new_ok = (jn <= trow) & (jn < ts)
    dist_new = (trow - jn).astype(F32)
    lam = _lambda(lv_ref[...], lam_init)
    og = og_ref[...]
    for h in range(nh):
        slope = 2.0 ** -(h + 1)
        qs = q_ref[h]
        kn = ndrow_ref[:, h * hd:(h + 1) * hd].astype(BF)
        vn = ndrow_ref[:, (nh + h) * hd:(nh + h + 1) * hd].astype(BF)
        s1 = _dot_nt(qs, past_ref[0, h].astype(BF)) - slope * dist
        s2 = _dot_nt(qs, kn) - slope * dist_new
        p1, p2 = _msoftmax2(s1, dist >= 0.0, s2, new_ok)
        a1 = p1 - lam * jnp.concatenate([p1[tsp:], p1[:tsp]], axis=0)
        a2 = p2 - lam * jnp.concatenate([p2[tsp:], p2[:tsp]], axis=0)
        o = _dot(a1.astype(BF), past_ref[1, h].astype(BF)) + _dot(a2.astype(BF), vn)
        o_ref[h] = _rms(o, og) * (1.0 - lam_init)


def _diff_sample(q, past_kv, new_drows, lam_vec, out_gain, *, past, ts, lam_init):
    nb, nh, rows, hd = q.shape
    npad = new_drows.shape[1]
    return pl.pallas_call(
        functools.partial(_diff_sample_body, past=past, ts=ts, lam_init=lam_init),
        grid=(nb,),
        in_specs=[
            pl.BlockSpec((4, DIFF_DK), lambda b: (0, 0)),
            pl.BlockSpec((1, hd), lambda b: (0, 0)),
            pl.BlockSpec((None, nh, rows, hd), lambda b: (b, 0, 0, 0)),
            pl.BlockSpec((None, 2, nh, past, hd), lambda b: (b, 0, 0, 0, 0)),
            pl.BlockSpec((None, npad, new_drows.shape[2]), lambda b: (b, 0, 0)),
        ],
        out_specs=pl.BlockSpec((None, nh, rows, hd), lambda b: (b, 0, 0, 0)),
        out_shape=jax.ShapeDtypeStruct((nb, nh, rows, hd), F32),
        compiler_params=_params("parallel"),
        name="diff_sample",
    )(lam_vec, out_gain.reshape(1, hd), q, past_kv, new_drows)


def _rows(a, lo, n):
    return lax.slice_in_dim(a, lo, lo + n, axis=a.ndim - 2)


def _pool_conv(p_ext, u_ext, b_gate, cnt_pos, pool_w_ref, pool_scale, conv_w, ts):
    gwid = p_ext.shape[-1] // len(POOL_WINDOWS)
    lead = p_ext.shape[:-2]
    outs = []
    for gi, w in enumerate(POOL_WINDOWS):
        x = p_ext[..., gi * gwid:(gi + 1) * gwid]
        acc = x
        span = 1
        while span < w:
            n = acc.shape[-2]
            acc = _rows(acc, span, n - span) + _rows(acc, 0, n - span)
            span *= 2
        win_sum = _rows(acc, HALO - (w - 1), ts)
        cnt = jnp.minimum(w, cnt_pos + 1).astype(F32)
        m = win_sum / cnt - _rows(x, HALO, ts)
        m2 = m.reshape((-1, gwid)).astype(BF)
        outs.append(_dot(m2, pool_w_ref[gi].astype(BF)).reshape(lead + (ts, gwid)))
    y_pool = jnp.concatenate(outs, axis=-1) * pool_scale
    conv = None
    for j in range(CONV_K):
        term = _rows(u_ext, HALO - (CONV_K - 1) + j, ts) * conv_w[j:j + 1, :]
        conv = term if conv is None else conv + term
    return jnp.concatenate([y_pool, b_gate * conv], axis=-1).astype(BF)


def _pc_prompt_body(zc_ref, zh_ref, pw_ref, ps_ref, cw_ref, o_ref, *, ts, cw):
    i = pl.program_id(1)
    keep = i > 0
    p_ext = jnp.concatenate([jnp.where(keep, zh_ref[:, 0:cw], 0.0), zc_ref[:, 0:cw]], axis=0)
    u_h = jnp.where(keep, zh_ref[:, 2 * cw:3 * cw] * zh_ref[:, 3 * cw:4 * cw], 0.0)
    u_ext = jnp.concatenate([u_h, zc_ref[:, 2 * cw:3 * cw] * zc_ref[:, 3 * cw:4 * cw]], axis=0)
    cnt_pos = i * ts + lax.broadcasted_iota(I32, (ts, 1), 0)
    o_ref[...] = _pool_conv(p_ext, u_ext, zc_ref[:, cw:2 * cw], cnt_pos, pw_ref, ps_ref[...], cw_ref[...], ts)


def _pc_prompt(z, pool_w, pool_scale, conv_w, *, batch, seq, ts):
    cw = pool_scale.shape[0]
    nt = seq // ts
    assert seq % ts == 0 and ts % HALO == 0 and z.shape[1] == 4 * cw
    hb = ts // HALO
    return pl.pallas_call(
        functools.partial(_pc_prompt_body, ts=ts, cw=cw),
        grid=(batch, nt),
        in_specs=[
            pl.BlockSpec((ts, 4 * cw), lambda b, i: (b * nt + i, 0)),
            pl.BlockSpec((HALO, 4 * cw), lambda b, i: (jnp.maximum((b * nt + i) * hb - 1, 0), 0)),
            pl.BlockSpec(pool_w.shape, lambda b, i: (0, 0, 0)),
            pl.BlockSpec((1, cw), lambda b, i: (0, 0)),
            pl.BlockSpec((CONV_K, cw), lambda b, i: (0, 0)),
        ],
        out_specs=pl.BlockSpec((ts, 2 * cw), lambda b, i: (b * nt + i, 0)),
        out_shape=jax.ShapeDtypeStruct((batch * seq, 2 * cw), BF),
        compiler_params=_params("parallel", "arbitrary"),
        name="pool_conv_prompt",
    )(z, z, pool_w, pool_scale.reshape(1, cw), conv_w)


def _pc_sample_body(pe_ref, ue_ref, bg_ref, pw_ref, ps_ref, cw_ref, o_ref, *, ts, past):
    cnt_pos = past + lax.broadcasted_iota(I32, (ts, 1), 0)
    o_ref[...] = _pool_conv(pe_ref[...], ue_ref[...], bg_ref[...], cnt_pos, pw_ref, ps_ref[...], cw_ref[...], ts)


def _pc_sample(p_ext, u_ext, b_gate, pool_w, pool_scale, conv_w, *, past, tb):
    nb, ext, cw = p_ext.shape
    ts = ext - HALO
    assert nb % tb == 0
    blk = lambda r: pl.BlockSpec((tb, r, cw), lambda b: (b, 0, 0))
    return pl.pallas_call(
        functools.partial(_pc_sample_body, ts=ts, past=past),
        grid=(nb // tb,),
        in_specs=[blk(ext), blk(ext), blk(ts),
                  pl.BlockSpec(pool_w.shape, lambda b: (0, 0, 0)),
                  pl.BlockSpec((1, cw), lambda b: (0, 0)),
                  pl.BlockSpec((CONV_K, cw), lambda b: (0, 0))],
        out_specs=pl.BlockSpec((tb, ts, 2 * cw), lambda b: (b, 0, 0)),
        out_shape=jax.ShapeDtypeStruct((nb, ts, 2 * cw), BF),
        compiler_params=_params("parallel"),
        name="pool_conv_sample",
    )(p_ext, u_ext, b_gate, pool_w, pool_scale.reshape(1, cw), conv_w)


def _peer_score_body(x_ref, g_ref, wq_ref, keys_ref, s_ref, xn_ref):
    @pl.when(pl.program_id(1) == 0)
    def _():
        xn_ref[...] = _rms(x_ref[...], g_ref[...]).astype(xn_ref.dtype)

    z = _dot(xn_ref[...], wq_ref[...].astype(BF)).astype(BF)
    nk = keys_ref.shape[1]
    dk = keys_ref.shape[2]
    for r in range(keys_ref.shape[0]):
        s_ref[:, r * nk:(r + 1) * nk] = _dot_nt(z[:, r * dk:(r + 1) * dk], keys_ref[r].astype(BF))


def _peer_scores(x, gain, wq, keys, *, tm):
    n, d = x.shape
    nsub, nk, dk = keys.shape
    per = 2
    assert n % tm == 0 and nsub % per == 0 and wq.shape[1] == nsub * dk
    return pl.pallas_call(
        _peer_score_body,
        grid=(n // tm, nsub // per),
        in_specs=[
            pl.BlockSpec((tm, d), lambda i, j: (i, 0)),
            pl.BlockSpec((1, d), lambda i, j: (0, 0)),
            pl.BlockSpec((d, per * dk), lambda i, j: (0, j)),
            pl.BlockSpec((per, nk, dk), lambda i, j: (j, 0, 0)),
        ],
        out_specs=[pl.BlockSpec((tm, per * nk), lambda i, j: (i, j)),
                   pl.BlockSpec((tm, d), lambda i, j: (i, 0))],
        out_shape=[jax.ShapeDtypeStruct((n, nsub * nk), F32), jax.ShapeDtypeStruct((n, d), BF)],
        compiler_params=_params("parallel", "arbitrary"),
        name="peer_scores",
    )(x, gain.reshape(1, d), wq, keys)


def _peer_w_body(a_ref, b_ref, g_ref, w_ref):
    tt = a_ref.shape[0]
    npair = a_ref.shape[2]
    io = lax.broadcasted_iota(I32, (tt, PEER_NKEYS, npair), 1)
    one_a = jnp.where(a_ref[...] == io, 1.0, 0.0).astype(BF)
    g = g_ref[...]
    g_hi = g.astype(BF).astype(F32)
    g_lo = g - g_hi
    hit = b_ref[...] == io
    dims = (((2,), (2,)), ((0,), (0,)))
    w = lax.dot_general(one_a, jnp.where(hit, g_hi, 0.0).astype(BF), dims, preferred_element_type=F32)
    w = w + lax.dot_general(one_a, jnp.where(hit, g_lo, 0.0).astype(BF), dims, preferred_element_type=F32)
    w_ref[...] = w


def _peer_w(i1, i2, gate, *, tt):
    n, npair = i1.shape
    assert n % tt == 0
    blk = pl.BlockSpec((tt, 1, npair), lambda i: (i, 0, 0))
    return pl.pallas_call(
        _peer_w_body,
        grid=(n // tt,),
        in_specs=[blk, blk, blk],
        out_specs=pl.BlockSpec((tt, PEER_NKEYS, PEER_NKEYS), lambda i: (i, 0, 0)),
        out_shape=jax.ShapeDtypeStruct((n, PEER_NKEYS, PEER_NKEYS), F32),
        compiler_params=_params("parallel"),
        name="peer_w",
    )(i1.reshape(n, 1, npair), i2.reshape(n, 1, npair), gate.reshape(n, 1, npair))


def _peer_dense_body(xn_ref, xr_ref, u_ref, v_ref, w_ref, o_ref):
    @pl.when(pl.program_id(1) == 0)
    def _():
        o_ref[...] = xr_ref[...]

    nw = w_ref.shape[1]
    s = _dot_nt(xn_ref[...], u_ref[...])
    act = 0.5 * s * (1.0 + lax.erf(s * (2.0 ** -0.5)))
    wact = jnp.concatenate(
        [act[:, r * PEER_NKEYS:(r + 1) * PEER_NKEYS] * w_ref[:, r, :] for r in range(nw)], axis=1)
    o_ref[...] += _dot(wact.astype(BF), v_ref[...])


def _peer_dense(xn, x_res, u_bf, v_bf, w, *, tn, ec):
    n, d = xn.shape
    ne = u_bf.shape[0]
    nw = ec // PEER_NKEYS
    assert n % tn == 0 and ne % ec == 0 and ec % PEER_NKEYS == 0 and nw % 8 == 0
    return pl.pallas_call(
        _peer_dense_body,
        grid=(n // tn, ne // ec),
        in_specs=[
            pl.BlockSpec((tn, d), lambda i, j: (i, 0)),
            pl.BlockSpec((tn, d), lambda i, j: (i, 0)),
            pl.BlockSpec((ec, d), lambda i, j: (j, 0)),
            pl.BlockSpec((ec, d), lambda i, j: (j, 0)),
            pl.BlockSpec((tn, nw, PEER_NKEYS), lambda i, j: (i, j, 0)),
        ],
        out_specs=pl.BlockSpec((tn, d), lambda i, j: (i, 0)),
        out_shape=jax.ShapeDtypeStruct((n, d), F32),
        compiler_params=_params("parallel", "arbitrary"),
        name="peer_dense",
    )(xn, x_res, u_bf, v_bf, w)


def _peer_ffn(x, gain, wq, keys, u_tab, v_tab, *, tm):
    n, d = x.shape
    nh = keys.shape[0]
    s, xn = _peer_scores(x, gain, wq, keys.reshape(nh * 2, PEER_NKEYS, keys.shape[-1]), tm=tm)
    s = s.reshape(n, nh, 2, PEER_NKEYS)
    sv, si = lax.top_k(s, PEER_TOPK)
    comb = (sv[:, :, 0, :, None] + sv[:, :, 1, None, :]).reshape(n, nh, PEER_TOPK * PEER_TOPK)
    cv, ci = lax.top_k(comb, PEER_TOPK)
    i1 = jnp.take_along_axis(si[:, :, 0], ci // PEER_TOPK, axis=-1)
    i2 = jnp.take_along_axis(si[:, :, 1], ci % PEER_TOPK, axis=-1)
    gate = jax.nn.softmax(cv, axis=-1)
    npair = nh * PEER_TOPK
    w = _peer_w(i1.reshape(n, npair), i2.reshape(n, npair), gate.reshape(n, npair), tt=32)
    return _peer_dense(xn, x, u_tab.astype(BF), v_tab.astype(BF), w, tn=tm // 2, ec=1024)


def _reorder_att_w_in(w_in):
    d = w_in.shape[0]
    o2 = _DQ0
    ngate = 3 * NSA_HPG
    gates = [jnp.pad(w_in[:, o2 + g * ngate:o2 + (g + 1) * ngate], ((0, 0), (0, LANES - ngate)))
             for g in range(NSA_KV_HEADS)]
    return jnp.concatenate([w_in[:, :o2], w_in[:, o2 + 3 * NSA_HEADS:]] + gates, axis=1)


def _attention_layer(x, li, n_prompt, batch, seq, nb, ts, cache_nsa_l, cache_diff_l, win_state_l, page_table,
                     norm_g, w_in, w_out, q_gain, k_gain, cmp_w, dq_gain, dk_gain, lam_vec, out_gain, *, tm):
    hd = HEAD_DIM
    ng = NSA_KV_HEADS
    hg = NSA_HPG
    nh = DIFF_HEADS
    lam_init = 0.8 - 0.6 * math.exp(-0.3 * li)
    past = page_table.shape[1] * cache_nsa_l.shape[1]

    z = _mm(x, _reorder_att_w_in(w_in), gain=norm_g, tm=tm, tn=256, name="att_in")
    qn, rows, win, gates, dqn, drows = _att_post(z, q_gain, k_gain, dq_gain, dk_gain, tm=272)

    cmp = _nsa_cmp(rows, cmp_w, batch=batch, seq=seq)
    o_nsa_p = _nsa_prompt(qn, gates, cmp, rows, win, batch=batch, seq=seq, tq=128)
    o_diff_p = _diff_prompt(dqn, drows, lam_vec, out_gain, batch=batch, seq=seq, tq=256, lam_init=lam_init)

    nsa_past = cache_nsa_l[page_table].reshape(nb, past, 4, ng, hd).transpose(0, 2, 3, 1, 4)
    diff_past = cache_diff_l[page_table].reshape(nb, past, 2, nh, hd).transpose(0, 2, 3, 1, 4)
    win_st = win_state_l.transpose(0, 2, 3, 1, 4)
    tsp = 8
    assert ts <= tsp
    pad_t = lambda a, ax: jnp.pad(a, [(0, tsp - ts) if k == ax else (0, 0) for k in range(a.ndim)])
    pad8 = lambda a: pad_t(a.reshape(nb, ts, a.shape[-1]), 1)
    rows_s, win_s, drows_s = rows[n_prompt:], win[n_prompt:], drows[n_prompt:]
    q_s = pad_t(qn[n_prompt:].reshape(nb, ts, ng, hg, hd).transpose(0, 2, 3, 1, 4), 3).reshape(nb, ng, hg * tsp, hd)
    g_s = gates[n_prompt:].reshape(nb, ts, ng, LANES)[..., :3 * hg].reshape(nb, ts, ng, hg, 3)
    g_s = pad_t(g_s.transpose(0, 2, 3, 1, 4), 3).reshape(nb, ng, hg * tsp, 3)
    g_s = jnp.pad(g_s, ((0, 0), (0, 0), (0, 0), (0, LANES - 3)))
    o_nsa_s = _nsa_sample(q_s, g_s, nsa_past, win_st, pad8(rows_s), pad8(win_s), cmp_w, past=past, ts=ts)
    o_nsa_s = o_nsa_s.reshape(nb, ng, hg, tsp, hd)[:, :, :, :ts]
    o_nsa_s = o_nsa_s.transpose(0, 3, 1, 2, 4).reshape(nb * ts, ng * hg * hd)

    dq_s = pad_t(dqn[n_prompt:].reshape(nb, ts, nh, hd).transpose(0, 2, 1, 3), 2)
    lo = jnp.arange(hd) < DIFF_DK
    dq_s = jnp.concatenate([jnp.where(lo, dq_s, 0), jnp.where(lo, 0, dq_s)], axis=2).astype(BF)
    o_diff_s = _diff_sample(dq_s, diff_past, pad8(drows_s), lam_vec, out_gain, past=past, ts=ts, lam_init=lam_init)
    o_diff_s = o_diff_s[:, :, :ts].transpose(0, 2, 1, 3).reshape(nb * ts, nh * hd)

    mix = jnp.concatenate([jnp.concatenate([o_nsa_p, o_diff_p], axis=1),
                           jnp.concatenate([o_nsa_s, o_diff_s], axis=1).astype(BF)], axis=0)
    x = _mm(mix, w_out, res=x, tm=tm, tn=512, name="att_out")

    rows_p = rows[:n_prompt].reshape(batch, seq, 4, ng, hd)
    win_p = win[:n_prompt].reshape(batch, seq, 2, ng, hd)
    keep_p = min(WINDOW, seq)
    win_new_s = win_s.reshape(nb, ts, 2, ng, hd)
    keep_s = win_state_l.shape[1]
    state = (rows_p, rows_s.reshape(nb, ts, 4, ng, hd),
             win_p[:, seq - keep_p:], jnp.concatenate([win_state_l, win_new_s], axis=1)[:, -keep_s:],
             drows[:n_prompt].reshape(batch, seq, 2, nh, hd), drows_s.reshape(nb, ts, 2, nh, hd))
    return x, state


def _pool_conv_layer(x, n_prompt, batch, seq, nb, ts, past, pool_hist, conv_hist,
                     norm_g, w_in, w_out, pool_w, pool_scale, conv_w, *, tm):
    cw = pool_scale.shape[0]
    z = _mm(x, w_in, gain=norm_g, tm=tm, tn=512, name="pc_in")
    mix_p = _pc_prompt(z, pool_w, pool_scale, conv_w, batch=batch, seq=seq, ts=256)

    zs = z[n_prompt:].reshape(nb, ts, 4 * cw)
    p_s = zs[..., :cw]
    u_s = zs[..., 2 * cw:3 * cw] * zs[..., 3 * cw:]
    tpad = 8
    front = lambda hist: jnp.pad(hist, ((0, 0), (HALO - hist.shape[1], 0), (0, 0)))
    back = lambda a: jnp.pad(a, ((0, 0), (0, tpad - ts), (0, 0)))
    p_ext = jnp.concatenate([front(pool_hist), back(p_s)], axis=1)
    u_ext = jnp.concatenate([front(conv_hist), back(u_s)], axis=1)
    mix_s = _pc_sample(p_ext, u_ext, back(zs[..., cw:2 * cw]), pool_w, pool_scale, conv_w, past=past, tb=8)
    mix = jnp.concatenate([mix_p, mix_s[:, :ts].reshape(nb * ts, 2 * cw)], axis=0)
    x = _mm(mix, w_out, res=x, tm=tm, tn=512, name="pc_out")

    zp = z[:n_prompt].reshape(batch, seq, 4 * cw)
    u_tail = zp[:, seq - (CONV_K - 1):, 2 * cw:3 * cw] * zp[:, seq - (CONV_K - 1):, 3 * cw:]
    state = (zp[:, seq - POOL_HIST:, :cw],
             jnp.concatenate([pool_hist, p_s], axis=1)[:, -POOL_HIST:],
             u_tail,
             jnp.concatenate([conv_hist, u_s], axis=1)[:, -(CONV_K - 1):])
    return x, state


def kernel(x_prompt, x_sample, cache_nsa, cache_diff, state_nsa_win, state_pool, state_conv, page_table,
           att_norm_g, att_w_in, att_w_out, nsa_q_gain, nsa_k_gain, nsa_cmp_w, diff_q_gain, diff_k_gain,
           diff_lambda, diff_out_gain, pc_norm_g, pc_w_in, pc_w_out, pool_w, pool_scale, conv_w,
           ffn_norm_g, peer_wq, peer_keys, peer_u, peer_v):
    batch, seq, d = x_prompt.shape
    nb, ts, _ = x_sample.shape
    n_prompt = batch * seq
    n = n_prompt + nb * ts
    depth = ffn_norm_g.shape[0]
    past = page_table.shape[1] * cache_nsa.shape[2]
    tm = n // 8
    assert n % 8 == 0 and tm % 16 == 0
    x = jnp.concatenate([x_prompt.reshape(n_prompt, d), x_sample.reshape(nb * ts, d)], axis=0)
    att_states, pc_states = [], []
    for li in range(depth):
        i = li // 2
        if li % 2 == 0:
            x, st = _attention_layer(
                x, li, n_prompt, batch, seq, nb, ts, cache_nsa[i], cache_diff[i], state_nsa_win[i], page_table,
                att_norm_g[i], att_w_in[i], att_w_out[i], nsa_q_gain[i], nsa_k_gain[i], nsa_cmp_w[i],
                diff_q_gain[i], diff_k_gain[i], diff_lambda[i], diff_out_gain[i], tm=tm)
            att_states.append(st)
        else:
            x, st = _pool_conv_layer(
                x, n_prompt, batch, seq, nb, ts, past, state_pool[i], state_conv[i],
                pc_norm_g[i], pc_w_in[i], pc_w_out[i], pool_w[i], pool_scale[i], conv_w[i], tm=tm)
            pc_states.append(st)
        x = _peer_ffn(x, ffn_norm_g[li], peer_wq[li], peer_keys[li], peer_u[li], peer_v[li], tm=tm)
    stack = lambda states, k: jnp.stack([s[k] for s in states])
    return (x[:n_prompt].reshape(batch, seq, d), x[n_prompt:].reshape(nb, ts, d),
            stack(att_states, 0), stack(att_states, 1), stack(att_states, 2), stack(att_states, 3),
            stack(att_states, 4), stack(att_states, 5),
            stack(pc_states, 0), stack(pc_states, 1), stack(pc_states, 2), stack(pc_states, 3))
```

```python
import functools
import math

import jax
import jax.numpy as jnp
from jax import lax
from jax.experimental import pallas as pl
from jax.experimental.pallas import tpu as pltpu

F32 = jnp.float32
BF = jnp.bfloat16
I32 = jnp.int32

HEAD_DIM = 128
NSA_HEADS = 8
NSA_KV_HEADS = 2
NSA_HPG = NSA_HEADS // NSA_KV_HEADS
L_CMP = 32
L_SEL = 64
N_SEL = 16
WINDOW = 512
FORCE_BONUS = 1e4
DIFF_HEADS = 8
DIFF_DK = HEAD_DIM // 2
POOL_WINDOWS = (2, 4, 8, 16)
POOL_HIST = max(POOL_WINDOWS) - 1
CONV_K = 3
PEER_HEADS = 8
PEER_NKEYS = 128
PEER_TOPK = 16
EPS = 1e-6
NEG = -1e30
LANES = 128
HALO = 16
VMEM_LIMIT = 56 * 1024 * 1024

_NT = (((1,), (1,)), ((), ()))


def _params(*sem):
    return pltpu.CompilerParams(dimension_semantics=sem, vmem_limit_bytes=VMEM_LIMIT)


def _dot_nt(a, b):
    return lax.dot_general(a, b, _NT, preferred_element_type=F32)


def _dot(a, b):
    return jnp.dot(a, b, preferred_element_type=F32)


def _pow2_neg(k):
    return lax.bitcast_convert_type((127 - k) << 23, F32)


def _rms(x, g):
    return x * lax.rsqrt(jnp.mean(x * x, axis=-1, keepdims=True) + EPS) * g


def _rms_halves(x, g):
    lo = lax.broadcasted_iota(I32, x.shape, 1) < DIFF_DK
    x2 = x * x
    s_lo = jnp.sum(jnp.where(lo, x2, 0.0), axis=-1, keepdims=True)
    s_hi = jnp.sum(jnp.where(lo, 0.0, x2), axis=-1, keepdims=True)
    ms = jnp.where(lo, s_lo, s_hi) * (1.0 / DIFF_DK)
    return x * lax.rsqrt(ms + EPS) * g


def _msoftmax(s, mask):
    sm = jnp.where(mask, s, NEG)
    m = jnp.max(sm, axis=-1, keepdims=True)
    e = jnp.where(mask, jnp.exp(sm - m), 0.0)
    d = jnp.sum(e, axis=-1, keepdims=True)
    return e / jnp.where(d > 0.0, d, 1.0)


def _msoftmax2(s1, m1, s2, m2):
    a1 = jnp.where(m1, s1, NEG)
    a2 = jnp.where(m2, s2, NEG)
    m = jnp.maximum(jnp.max(a1, axis=-1, keepdims=True), jnp.max(a2, axis=-1, keepdims=True))
    e1 = jnp.where(m1, jnp.exp(a1 - m), 0.0)
    e2 = jnp.where(m2, jnp.exp(a2 - m), 0.0)
    d = jnp.sum(e1, axis=-1, keepdims=True) + jnp.sum(e2, axis=-1, keepdims=True)
    inv = 1.0 / jnp.where(d > 0.0, d, 1.0)
    return e1 * inv, e2 * inv


def _mm_body(*refs, norm, has_res):
    if norm:
        x_ref, g_ref, w_ref = refs[:3]
        rest = refs[3:]
    else:
        x_ref, w_ref = refs[:2]
        rest = refs[2:]
    if has_res:
        r_ref, o_ref, xb_ref = rest
    else:
        o_ref, xb_ref = rest

    @pl.when(pl.program_id(1) == 0)
    def _():
        x = x_ref[...]
        if norm:
            x = _rms(x.astype(F32), g_ref[...])
        xb_ref[...] = x.astype(BF)

    y = _dot(xb_ref[...], w_ref[...].astype(BF))
    if has_res:
        y = y + r_ref[...]
    o_ref[...] = y.astype(o_ref.dtype)


def _mm(x, w, *, gain=None, res=None, tm, tn, name):
    m, k = x.shape
    n = w.shape[1]
    assert m % tm == 0 and n % tn == 0
    norm = gain is not None
    in_specs = [pl.BlockSpec((tm, k), lambda i, j: (i, 0))]
    args = [x]
    if norm:
        in_specs.append(pl.BlockSpec((1, k), lambda i, j: (0, 0)))
        args.append(gain.reshape(1, k))
    in_specs.append(pl.BlockSpec((k, tn), lambda i, j: (0, j)))
    args.append(w)
    if res is not None:
        in_specs.append(pl.BlockSpec((tm, tn), lambda i, j: (i, j)))
        args.append(res)
    return pl.pallas_call(
        functools.partial(_mm_body, norm=norm, has_res=res is not None),
        grid=(m // tm, n // tn),
        in_specs=in_specs,
        out_specs=pl.BlockSpec((tm, tn), lambda i, j: (i, j)),
        out_shape=jax.ShapeDtypeStruct((m, n), F32),
        scratch_shapes=[pltpu.VMEM((tm, k), BF)],
        compiler_params=_params("parallel", "arbitrary"),
        name=name,
    )(*args)


_Q0 = 0
_KV0 = NSA_HEADS * HEAD_DIM
_DQ0 = _KV0 + 6 * NSA_KV_HEADS * HEAD_DIM
_DK0 = _DQ0 + DIFF_HEADS * HEAD_DIM
_DV0 = _DK0 + DIFF_HEADS * HEAD_DIM
_GT0 = _DV0 + DIFF_HEADS * HEAD_DIM
_ZW = _GT0 + NSA_KV_HEADS * LANES


def _att_post_body(z_ref, qg_ref, kg_ref, dqg_ref, dkg_ref,
                   qn_ref, rows_ref, win_ref, gate_ref, dqn_ref, drows_ref):
    hd = HEAD_DIM
    qg = qg_ref[...]
    for h in range(NSA_HEADS):
        x = z_ref[:, _Q0 + h * hd:_Q0 + (h + 1) * hd]
        qn_ref[:, h * hd:(h + 1) * hd] = (_rms(x, qg) * (HEAD_DIM ** -0.5)).astype(qn_ref.dtype)
    gw = NSA_KV_HEADS * hd
    for br in range(3):
        for g in range(NSA_KV_HEADS):
            ko = _KV0 + br * 2 * gw + g * hd
            k = _rms(z_ref[:, ko:ko + hd], kg_ref[br:br + 1, :])
            v = z_ref[:, ko + gw:ko + gw + hd]
            if br < 2:
                rows_ref[:, br * 2 * gw + g * hd:br * 2 * gw + (g + 1) * hd] = k
                rows_ref[:, br * 2 * gw + gw + g * hd:br * 2 * gw + gw + (g + 1) * hd] = v
            else:
                win_ref[:, g * hd:(g + 1) * hd] = k
                win_ref[:, gw + g * hd:gw + (g + 1) * hd] = v
    gate_ref[...] = jax.nn.sigmoid(z_ref[:, _GT0:_ZW])
    dqg = dqg_ref[...]
    dkg = dkg_ref[...]
    for h in range(DIFF_HEADS):
        x = z_ref[:, _DQ0 + h * hd:_DQ0 + (h + 1) * hd]
        dqn_ref[:, h * hd:(h + 1) * hd] = (_rms_halves(x, dqg) * (DIFF_DK ** -0.5)).astype(dqn_ref.dtype)
        x = z_ref[:, _DK0 + h * hd:_DK0 + (h + 1) * hd]
        drows_ref[:, h * hd:(h + 1) * hd] = _rms_halves(x, dkg)
    dvw = DIFF_HEADS * hd
    drows_ref[:, dvw:2 * dvw] = z_ref[:, _DV0:_DV0 + dvw]


def _att_post(z, q_gain, k_gain, dq_gain, dk_gain, *, tm):
    n = z.shape[0]
    assert n % tm == 0 and z.shape[1] == _ZW
    hd = HEAD_DIM
    widths = (NSA_HEADS * hd, 4 * NSA_KV_HEADS * hd, 2 * NSA_KV_HEADS * hd, NSA_KV_HEADS * LANES,
              DIFF_HEADS * hd, 2 * DIFF_HEADS * hd)
    dtypes = (BF, F32, F32, F32, BF, F32)
    small = lambda r: pl.BlockSpec((r, hd), lambda i: (0, 0))
    return pl.pallas_call(
        _att_post_body,
        grid=(n // tm,),
        in_specs=[pl.BlockSpec((tm, _ZW), lambda i: (i, 0)), small(1), small(3), small(1), small(1)],
        out_specs=[pl.BlockSpec((tm, w), lambda i: (i, 0)) for w in widths],
        out_shape=[jax.ShapeDtypeStruct((n, w), d) for w, d in zip(widths, dtypes)],
        compiler_params=_params("parallel"),
        name="att_post",
    )(z, q_gain.reshape(1, hd), k_gain, dq_gain.reshape(1, hd), dk_gain.reshape(1, hd))


def _compress(x, w):
    length, c = x.shape
    npair = length // (2 * L_CMP)
    x3 = x.reshape(npair, 2 * L_CMP, c)
    even = jnp.sum(x3[:, :L_CMP, :] * w, axis=1)
    odd = jnp.sum(x3[:, L_CMP:, :] * w, axis=1)
    pad = jnp.zeros((LANES - 2 * npair, c), F32)
    return jnp.concatenate([even, odd, pad], axis=0)


def _nsa_cmp_body(x_ref, w_ref, o_ref):
    o_ref[...] = _compress(x_ref[...], w_ref[...])


def _nsa_cmp(rows, cmp_w, *, batch, seq):
    gw = NSA_KV_HEADS * HEAD_DIM
    assert seq % (2 * L_CMP) == 0 and seq // L_CMP <= LANES
    w = jnp.concatenate([jnp.tile(cmp_w[0], (1, NSA_KV_HEADS)), jnp.tile(cmp_w[1], (1, NSA_KV_HEADS))], axis=1)
    return pl.pallas_call(
        _nsa_cmp_body,
        grid=(batch,),
        in_specs=[pl.BlockSpec((seq, 2 * gw), lambda b: (b, 0)),
                  pl.BlockSpec((L_CMP, 2 * gw), lambda b: (0, 0))],
        out_specs=pl.BlockSpec((None, LANES, 2 * gw), lambda b: (b, 0, 0)),
        out_shape=jax.ShapeDtypeStruct((batch, LANES, 2 * gw), F32),
        compiler_params=_params("parallel"),
        name="nsa_cmp",
    )(rows, w)


def _cmp_positions(ncmp):
    c = lax.broadcasted_iota(I32, (1, LANES), 1)
    half = ncmp // 2
    blk = jnp.where(c < half, 2 * c, 2 * (c - half) + 1)
    return c, blk * L_CMP + (L_CMP - 1), c < ncmp


def _select_blocks(p_heads, t1, ncmp, nsel_blocks):
    c = lax.broadcasted_iota(I32, (1, LANES), 1)
    half = ncmp // 2
    imp = p_heads[0]
    for p in p_heads[1:]:
        imp = imp + p
    imp = jnp.where(c < half, imp + pltpu.roll(imp, LANES - half, 1), 0.0)
    cur = t1 // L_SEL
    forced = (c == 0) | (c == cur) | (c == cur - 1)
    valid = (c * L_SEL <= t1) & (c < nsel_blocks)
    score = jnp.where(valid, imp + jnp.where(forced, FORCE_BONUS, 0.0), NEG)
    rank = jnp.zeros(score.shape, I32)
    for jp in range(nsel_blocks):
        sj = score[:, jp:jp + 1]
        beats = (sj > score) | ((sj == score) & (c > jp))
        rank = rank + beats.astype(I32)
    return (rank < min(N_SEL, nsel_blocks)) & (c < nsel_blocks)


def _expand_blocks(sel01, length):
    j = lax.broadcasted_iota(I32, (LANES, length), 0)
    l = lax.broadcasted_iota(I32, (LANES, length), 1)
    e = jnp.where(l // L_SEL == j, 1.0, 0.0).astype(BF)
    return _dot(sel01.astype(BF), e)


def _nsa_prompt_body(q_ref, gate_ref, kc_ref, vc_ref, ks_ref, vs_ref, kw_ref, vw_ref, o_ref, *, tq, seq):
    g = pl.program_id(1)
    t0 = pl.program_id(2) * tq
    hg = NSA_HPG
    hd = HEAD_DIM
    rows = hg * tq
    q = q_ref[...]
    qs = jnp.concatenate([q[:, h * hd:(h + 1) * hd] for h in range(hg)], axis=0)
    r_io = lax.broadcasted_iota(I32, (rows, 1), 0)
    hrow = r_io // tq
    tpos = t0 + (r_io - hrow * tq)
    slope = _pow2_neg(g * hg + hrow + 1)
    t1 = t0 + lax.broadcasted_iota(I32, (tq, 1), 0)
    ncmp = seq // L_CMP
    nblk = -(-seq // L_SEL)

    _, pc, cvalid = _cmp_positions(ncmp)
    s = _dot_nt(qs, kc_ref[...].astype(BF)) - slope * (tpos - pc).astype(F32)
    p = _msoftmax(s, cvalid & (pc <= tpos))
    o_c = _dot(p.astype(BF), vc_ref[...].astype(BF))
    sel = _select_blocks([p[h * tq:(h + 1) * tq] for h in range(hg)], t1, ncmp, nblk)
    selm = _expand_blocks(jnp.where(sel, 1.0, 0.0), seq)
    selm = jnp.concatenate([selm] * hg, axis=0)

    pos = lax.broadcasted_iota(I32, (1, seq), 1)
    dist = tpos - pos
    s = _dot_nt(qs, ks_ref[...].astype(BF)) - slope * dist.astype(F32)
    p = _msoftmax(s, (dist >= 0) & (selm > 0.5))
    o_s = _dot(p.astype(BF), vs_ref[...].astype(BF))

    wl = WINDOW + tq
    start = pl.multiple_of(jnp.maximum(t0 - WINDOW, 0), tq)
    posw = start + lax.broadcasted_iota(I32, (1, wl), 1)
    dist = tpos - posw
    s = _dot_nt(qs, kw_ref[pl.ds(start, wl), :].astype(BF)) - slope * dist.astype(F32)
    p = _msoftmax(s, (dist >= 0) & (dist < WINDOW))
    o_w = _dot(p.astype(BF), vw_ref[pl.ds(start, wl), :].astype(BF))

    gt = gate_ref[...]
    for h in range(hg):
        rs = slice(h * tq, (h + 1) * tq)
        o = gt[:, 3 * h:3 * h + 1] * o_c[rs] + gt[:, 3 * h + 1:3 * h + 2] * o_s[rs] + gt[:, 3 * h + 2:3 * h + 3] * o_w[rs]
        o_ref[:, h * hd:(h + 1) * hd] = o.astype(o_ref.dtype)


def _nsa_prompt(qn, gates, cmp, rows, win, *, batch, seq, tq):
    hd = HEAD_DIM
    ng = NSA_KV_HEADS
    nq = seq // tq
    assert seq % tq == 0 and tq % LANES == 0 and WINDOW % tq == 0 and seq >= WINDOW + tq
    gq = NSA_HPG * hd
    seq_blk = lambda col: pl.BlockSpec((seq, hd), lambda b, g, i, col=col: (b, col + g))
    return pl.pallas_call(
        functools.partial(_nsa_prompt_body, tq=tq, seq=seq),
        grid=(batch, ng, nq),
        in_specs=[
            pl.BlockSpec((tq, gq), lambda b, g, i: (b * nq + i, g)),
            pl.BlockSpec((tq, LANES), lambda b, g, i: (b * nq + i, g)),
            pl.BlockSpec((None, LANES, hd), lambda b, g, i: (b, 0, g)),
            pl.BlockSpec((None, LANES, hd), lambda b, g, i: (b, 0, ng + g)),
            seq_blk(2 * ng), seq_blk(3 * ng),
            seq_blk(0), seq_blk(ng),
        ],
        out_specs=pl.BlockSpec((tq, gq), lambda b, g, i: (b * nq + i, g)),
        out_shape=jax.ShapeDtypeStruct((batch * seq, ng * gq), BF),
        compiler_params=_params("parallel", "parallel", "arbitrary"),
        name="nsa_prompt",
    )(qn, gates, cmp, cmp, rows, rows, win, win)


def _lambda(lv, lam_init):
    a = jnp.sum(lv[0:1, :] * lv[1:2, :], axis=-1, keepdims=True)
    b = jnp.sum(lv[2:3, :] * lv[3:4, :], axis=-1, keepdims=True)
    return jnp.exp(a) - jnp.exp(b) + lam_init


def _diff_prompt_body(lv_ref, og_ref, q_ref, k_ref, v_ref, o_ref, *, tq, seq, lam_init):
    h = pl.program_id(1)
    t0 = pl.program_id(2) * tq
    q = q_ref[...].astype(F32)
    lo = lax.broadcasted_iota(I32, q.shape, 1) < DIFF_DK
    qs = jnp.concatenate([jnp.where(lo, q, 0.0), jnp.where(lo, 0.0, q)], axis=0).astype(BF)
    r_io = lax.broadcasted_iota(I32, (2 * tq, 1), 0)
    tpos = t0 + jnp.where(r_io >= tq, r_io - tq, r_io)
    slope = _pow2_neg(jnp.full((1, 1), h + 1, I32))
    dist = tpos - lax.broadcasted_iota(I32, (1, seq), 1)
    s = _dot_nt(qs, k_ref[...].astype(BF)) - slope * dist.astype(F32)
    p = _msoftmax(s, dist >= 0)
    lam = _lambda(lv_ref[...], lam_init)
    a = p[:tq] - lam * p[tq:]
    o = _dot(a.astype(BF), v_ref[...].astype(BF))
    o_ref[...] = (_rms(o, og_ref[...]) * (1.0 - lam_init)).astype(o_ref.dtype)


def _diff_prompt(dqn, drows, lam_vec, out_gain, *, batch, seq, tq, lam_init):
    hd = HEAD_DIM
    nh = DIFF_HEADS
    nq = seq // tq
    assert seq % tq == 0
    return pl.pallas_call(
        functools.partial(_diff_prompt_body, tq=tq, seq=seq, lam_init=lam_init),
        grid=(batch, nh, nq),
        in_specs=[
            pl.BlockSpec((4, DIFF_DK), lambda b, h, i: (0, 0)),
            pl.BlockSpec((1, hd), lambda b, h, i: (0, 0)),
            pl.BlockSpec((tq, hd), lambda b, h, i: (b * nq + i, h)),
            pl.BlockSpec((seq, hd), lambda b, h, i: (b, h)),
            pl.BlockSpec((seq, hd), lambda b, h, i: (b, nh + h)),
        ],
        out_specs=pl.BlockSpec((tq, hd), lambda b, h, i: (b * nq + i, h)),
        out_shape=jax.ShapeDtypeStruct((batch * seq, nh * hd), BF),
        compiler_params=_params("parallel", "parallel", "arbitrary"),
        name="diff_prompt",
    )(lam_vec, out_gain.reshape(1, hd), dqn, drows, drows)


def _nsa_sample_body(q_ref, gate_ref, past_ref, wst_ref, nrow_ref, nwin_ref, cw_ref, o_ref, *, past, ts):
    hg = NSA_HPG
    hd = HEAD_DIM
    ng = NSA_KV_HEADS
    rows = q_ref.shape[1]
    tsp = rows // hg
    npad = nrow_ref.shape[0]
    gw = ng * hd
    ncmp = (past + ts) // L_CMP
    nblk = -(-(past + ts) // L_SEL)
    new_blk = past // L_SEL
    r_io = lax.broadcasted_iota(I32, (rows, 1), 0)
    hrow = r_io // tsp
    trow = r_io - hrow * tsp
    tpos = past + trow
    t1 = past + lax.broadcasted_iota(I32, (tsp, 1), 0)
    jn = lax.broadcasted_iota(I32, (1, npad), 1)
    new_ok = (jn <= trow) & (jn < ts)
    dist_new = (trow - jn).astype(F32)
    _, pc, cvalid = _cmp_positions(ncmp)
    pos = lax.broadcasted_iota(I32, (1, past), 1)
    nwst = wst_ref.shape[2]
    posw = past - nwst + lax.broadcasted_iota(I32, (1, nwst), 1)
    for g in range(ng):
        qs = q_ref[g]
        slope = _pow2_neg(g * hg + hrow + 1)
        kc = _compress(past_ref[0, g], cw_ref[0]).astype(BF)
        vc = _compress(past_ref[1, g], cw_ref[1]).astype(BF)
        s = _dot_nt(qs, kc) - slope * (tpos - pc).astype(F32)
        p = _msoftmax(s, cvalid & (pc <= tpos))
        o_c = _dot(p.astype(BF), vc)
        sel = _select_blocks([p[h * tsp:(h + 1) * tsp] for h in range(hg)], t1, ncmp, nblk)
        sel01 = jnp.concatenate([jnp.where(sel, 1.0, 0.0)] * hg, axis=0)
        selm = _expand_blocks(sel01, past)
        sel_new = sel01[:, new_blk:new_blk + 1] > 0.5
        kn = nrow_ref[:, 2 * gw + g * hd:2 * gw + (g + 1) * hd].astype(BF)
        vn = nrow_ref[:, 3 * gw + g * hd:3 * gw + (g + 1) * hd].astype(BF)
        s1 = _dot_nt(qs, past_ref[2, g].astype(BF)) - slope * (tpos - pos).astype(F32)
        s2 = _dot_nt(qs, kn) - slope * dist_new
        p1, p2 = _msoftmax2(s1, selm > 0.5, s2, new_ok & sel_new)
        o_s = _dot(p1.astype(BF), past_ref[3, g].astype(BF)) + _dot(p2.astype(BF), vn)
        kn = nwin_ref[:, g * hd:(g + 1) * hd].astype(BF)
        vn = nwin_ref[:, gw + g * hd:gw + (g + 1) * hd].astype(BF)
        dist = tpos - posw
        s1 = _dot_nt(qs, wst_ref[0, g].astype(BF)) - slope * dist.astype(F32)
        s2 = _dot_nt(qs, kn) - slope * dist_new
        p1, p2 = _msoftmax2(s1, (dist >= 0) & (dist < WINDOW), s2, new_ok)
        o_w = _dot(p1.astype(BF), wst_ref[1, g].astype(BF)) + _dot(p2.astype(BF), vn)
        gt = gate_ref[g]
        o_ref[g] = gt[:, 0:1] * o_c + gt[:, 1:2] * o_s + gt[:, 2:3] * o_w


def _nsa_sample(q, gates, past_kv, win_state, new_rows, new_win, cmp_w, *, past, ts):
    nb, ng, rows, hd = q.shape
    nwst = win_state.shape[3]
    npad = new_rows.shape[1]
    assert past % (2 * L_CMP) == 0 and past % L_SEL == 0 and nwst == WINDOW and (past + ts) // L_CMP == past // L_CMP
    return pl.pallas_call(
        functools.partial(_nsa_sample_body, past=past, ts=ts),
        grid=(nb,),
        in_specs=[
            pl.BlockSpec((None, ng, rows, hd), lambda b: (b, 0, 0, 0)),
            pl.BlockSpec((None, ng, rows, LANES), lambda b: (b, 0, 0, 0)),
            pl.BlockSpec((None, 4, ng, past, hd), lambda b: (b, 0, 0, 0, 0)),
            pl.BlockSpec((None, 2, ng, nwst, hd), lambda b: (b, 0, 0, 0, 0)),
            pl.BlockSpec((None, npad, new_rows.shape[2]), lambda b: (b, 0, 0)),
            pl.BlockSpec((None, npad, new_win.shape[2]), lambda b: (b, 0, 0)),
            pl.BlockSpec((2, L_CMP, hd), lambda b: (0, 0, 0)),
        ],
        out_specs=pl.BlockSpec((None, ng, rows, hd), lambda b: (b, 0, 0, 0)),
        out_shape=jax.ShapeDtypeStruct((nb, ng, rows, hd), F32),
        compiler_params=_params("parallel"),
        name="nsa_sample",
    )(q, gates, past_kv, win_state, new_rows, new_win, cmp_w)


def _diff_sample_body(lv_ref, og_ref, q_ref, past_ref, ndrow_ref, o_ref, *, past, ts, lam_init):
    hd = HEAD_DIM
    nh = DIFF_HEADS
    npad = ndrow_ref.shape[0]
    rows = q_ref.shape[1]
    tsp = rows // 2
    r_io = lax.broadcasted_iota(I32, (rows, 1), 0)
    trow = jnp.where(r_io >= tsp, r_io - tsp, r_io)
    tpos = past + trow
    dist = (tpos - lax.broadcasted_iota(I32, (1, past), 1)).astype(F32)
    jn = lax.broadcasted_iota(I32, (1, npad), 1)
    new_ok = (jn <= trow) & (jn < ts)
    dist_new = (trow - jn).astype(F32)
    lam = _lambda(lv_ref[...], lam_init)
    og = og_ref[...]
    for h in range(nh):
        slope = 2.0 ** -(h + 1)
        qs = q_ref[h]
        kn = ndrow_ref[:, h * hd:(h + 1) * hd].astype(BF)
        vn = ndrow_ref[:, (nh + h) * hd:(nh + h + 1) * hd].astype(BF)
        s1 = _dot_nt(qs, past_ref[0, h].astype(BF)) - slope * dist
        s2 = _dot_nt(qs, kn) - slope * dist_new
        p1, p2 = _msoftmax2(s1, dist >= 0.0, s2, new_ok)
        a1 = p1 - lam * jnp.concatenate([p1[tsp:], p1[:tsp]], axis=0)
        a2 = p2 - lam * jnp.concatenate([p2[tsp:], p2[:tsp]], axis=0)
        o = _dot(a1.astype(BF), past_ref[1, h].astype(BF)) + _dot(a2.astype(BF), vn)
        o_ref[h] = _rms(o, og) * (1.0 - lam_init)


def _diff_sample(q, past_kv, new_drows, lam_vec, out_gain, *, past, ts, lam_init):
    nb, nh, rows, hd = q.shape
    npad = new_drows.shape[1]
    return pl.pallas_call(
        functools.partial(_diff_sample_body, past=past, ts=ts, lam_init=lam_init),
        grid=(nb,),
        in_specs=[
            pl.BlockSpec((4, DIFF_DK), lambda b: (0, 0)),
            pl.BlockSpec((1, hd), lambda b: (0, 0)),
            pl.BlockSpec((None, nh, rows, hd), lambda b: (b, 0, 0, 0)),
            pl.BlockSpec((None, 2, nh, past, hd), lambda b: (b, 0, 0, 0, 0)),
            pl.BlockSpec((None, npad, new_drows.shape[2]), lambda b: (b, 0, 0)),
        ],
        out_specs=pl.BlockSpec((None, nh, rows, hd), lambda b: (b, 0, 0, 0)),
        out_shape=jax.ShapeDtypeStruct((nb, nh, rows, hd), F32),
        compiler_params=_params("parallel"),
        name="diff_sample",
    )(lam_vec, out_gain.reshape(1, hd), q, past_kv, new_drows)


def _rows(a, lo, n):
    return lax.slice_in_dim(a, lo, lo + n, axis=a.ndim - 2)


def _pool_conv(p_ext, u_ext, b_gate, cnt_pos, pool_w_ref, pool_scale, conv_w, ts):
    gwid = p_ext.shape[-1] // len(POOL_WINDOWS)
    lead = p_ext.shape[:-2]
    outs = []
    for gi, w in enumerate(POOL_WINDOWS):
        x = p_ext[..., gi * gwid:(gi + 1) * gwid]
        acc = x
        span = 1
        while span < w:
            n = acc.shape[-2]
            acc = _rows(acc, span, n - span) + _rows(acc, 0, n - span)
            span *= 2
        win_sum = _rows(acc, HALO - (w - 1), ts)
        cnt = jnp.minimum(w, cnt_pos + 1).astype(F32)
        m = win_sum / cnt - _rows(x, HALO, ts)
        m2 = m.reshape((-1, gwid)).astype(BF)
        outs.append(_dot(m2, pool_w_ref[gi].astype(BF)).reshape(lead + (ts, gwid)))
    y_pool = jnp.concatenate(outs, axis=-1) * pool_scale
    conv = None
    for j in range(CONV_K):
        term = _rows(u_ext, HALO - (CONV_K - 1) + j, ts) * conv_w[j:j + 1, :]
        conv = term if conv is None else conv + term
    return jnp.concatenate([y_pool, b_gate * conv], axis=-1).astype(BF)


def _pc_prompt_body(zc_ref, zh_ref, pw_ref, ps_ref, cw_ref, o_ref, *, ts, cw):
    i = pl.program_id(1)
    keep = i > 0
    p_ext = jnp.concatenate([jnp.where(keep, zh_ref[:, 0:cw], 0.0), zc_ref[:, 0:cw]], axis=0)
    u_h = jnp.where(keep, zh_ref[:, 2 * cw:3 * cw] * zh_ref[:, 3 * cw:4 * cw], 0.0)
    u_ext = jnp.concatenate([u_h, zc_ref[:, 2 * cw:3 * cw] * zc_ref[:, 3 * cw:4 * cw]], axis=0)
    cnt_pos = i * ts + lax.broadcasted_iota(I32, (ts, 1), 0)
    o_ref[...] = _pool_conv(p_ext, u_ext, zc_ref[:, cw:2 * cw], cnt_pos, pw_ref, ps_ref[...], cw_ref[...], ts)


def _pc_prompt(z, pool_w, pool_scale, conv_w, *, batch, seq, ts):
    cw = pool_scale.shape[0]
    nt = seq // ts
    assert seq % ts == 0 and ts % HALO == 0 and z.shape[1] == 4 * cw
    hb = ts // HALO
    return pl.pallas_call(
        functools.partial(_pc_prompt_body, ts=ts, cw=cw),
        grid=(batch, nt),
        in_specs=[
            pl.BlockSpec((ts, 4 * cw), lambda b, i: (b * nt + i, 0)),
            pl.BlockSpec((HALO, 4 * cw), lambda b, i: (jnp.maximum((b * nt + i) * hb - 1, 0), 0)),
            pl.BlockSpec(pool_w.shape, lambda b, i: (0, 0, 0)),
            pl.BlockSpec((1, cw), lambda b, i: (0, 0)),
            pl.BlockSpec((CONV_K, cw), lambda b, i: (0, 0)),
        ],
        out_specs=pl.BlockSpec((ts, 2 * cw), lambda b, i: (b * nt + i, 0)),
        out_shape=jax.ShapeDtypeStruct((batch * seq, 2 * cw), BF),
        compiler_params=_params("parallel", "arbitrary"),
        name="pool_conv_prompt",
    )(z, z, pool_w, pool_scale.reshape(1, cw), conv_w)


def _pc_sample_body(pe_ref, ue_ref, bg_ref, pw_ref, ps_ref, cw_ref, o_ref, *, ts, past):
    cnt_pos = past + lax.broadcasted_iota(I32, (ts, 1), 0)
    o_ref[...] = _pool_conv(pe_ref[...], ue_ref[...], bg_ref[...], cnt_pos, pw_ref, ps_ref[...], cw_ref[...], ts)


def _pc_sample(p_ext, u_ext, b_gate, pool_w, pool_scale, conv_w, *, past, tb):
    nb, ext, cw = p_ext.shape
    ts = ext - HALO
    assert nb % tb == 0
    blk = lambda r: pl.BlockSpec((tb, r, cw), lambda b: (b, 0, 0))
    return pl.pallas_call(
        functools.partial(_pc_sample_body, ts=ts, past=past),
        grid=(nb // tb,),
        in_specs=[blk(ext), blk(ext), blk(ts),
                  pl.BlockSpec(pool_w.shape, lambda b: (0, 0, 0)),
                  pl.BlockSpec((1, cw), lambda b: (0, 0)),
                  pl.BlockSpec((CONV_K, cw), lambda b: (0, 0))],
        out_specs=pl.BlockSpec((tb, ts, 2 * cw), lambda b: (b, 0, 0)),
        out_shape=jax.ShapeDtypeStruct((nb, ts, 2 * cw), BF),
        compiler_params=_params("parallel"),
        name="pool_conv_sample",
    )(p_ext, u_ext, b_gate, pool_w, pool_scale.reshape(1, cw), conv_w)


def _peer_score_body(x_ref, g_ref, wq_ref, keys_ref, s_ref, xn_ref):
    @pl.when(pl.program_id(1) == 0)
    def _():
        xn_ref[...] = _rms(x_ref[...], g_ref[...]).astype(xn_ref.dtype)

    z = _dot(xn_ref[...], wq_ref[...].astype(BF)).astype(BF)
    nk = keys_ref.shape[1]
    dk = keys_ref.shape[2]
    for r in range(keys_ref.shape[0]):
        s_ref[r * nk:(r + 1) * nk, :] = _dot_nt(keys_ref[r].astype(BF), z[:, r * dk:(r + 1) * dk])


def _peer_scores(x, gain, wq, keys, *, tm):
    n, d = x.shape
    nsub, nk, dk = keys.shape
    per = 2
    assert n % tm == 0 and tm % LANES == 0 and nsub % per == 0 and wq.shape[1] == nsub * dk
    return pl.pallas_call(
        _peer_score_body,
        grid=(n // tm, nsub // per),
        in_specs=[
            pl.BlockSpec((tm, d), lambda i, j: (i, 0)),
            pl.BlockSpec((1, d), lambda i, j: (0, 0)),
            pl.BlockSpec((d, per * dk), lambda i, j: (0, j)),
            pl.BlockSpec((per, nk, dk), lambda i, j: (j, 0, 0)),
        ],
        out_specs=[pl.BlockSpec((per * nk, tm), lambda i, j: (j, i)),
                   pl.BlockSpec((tm, d), lambda i, j: (i, 0))],
        out_shape=[jax.ShapeDtypeStruct((nsub * nk, n), F32), jax.ShapeDtypeStruct((n, d), BF)],
        compiler_params=_params("parallel", "arbitrary"),
        name="peer_scores",
    )(x, gain.reshape(1, d), wq, keys)


def _topk_rows(x, k):
    nrow = x.shape[0]
    row = lax.broadcasted_iota(I32, x.shape, 0)
    slot = lax.broadcasted_iota(I32, (k, x.shape[1]), 0)
    vals = jnp.zeros((k, x.shape[1]), F32)
    idxs = jnp.zeros((k, x.shape[1]), I32)
    for kk in range(k):
        m = jnp.max(x, axis=0, keepdims=True)
        idx = jnp.min(jnp.where(x == m, row, nrow), axis=0, keepdims=True)
        vals = jnp.where(slot == kk, m, vals)
        idxs = jnp.where(slot == kk, idx, idxs)
        x = jnp.where(row == idx, -jnp.inf, x)
    return vals, idxs


def _peer_topk_body(s_ref, i1_ref, i2_ref, g_ref, sv_ref, si_ref, i1t_ref, i2t_ref, gt_ref):
    nsub = sv_ref.shape[0]
    k = PEER_TOPK
    nk = PEER_NKEYS

    def sub_key(gi, c):
        v, i = _topk_rows(s_ref[pl.ds(pl.multiple_of(gi * nk, nk), nk), :], k)
        sv_ref[gi] = v
        si_ref[gi] = i
        return c

    lax.fori_loop(0, nsub, sub_key, 0)

    def head(h, c):
        v0 = sv_ref[2 * h]
        v1 = sv_ref[2 * h + 1]
        comb = jnp.concatenate([v0[a:a + 1] + v1 for a in range(k)], axis=0)
        cv, ci = _topk_rows(comb, k)
        a = ci // k
        b = ci - a * k
        s0 = si_ref[2 * h]
        s1 = si_ref[2 * h + 1]
        i1 = jnp.zeros(ci.shape, I32)
        i2 = jnp.zeros(ci.shape, I32)
        for q in range(k):
            i1 = jnp.where(a == q, s0[q:q + 1], i1)
            i2 = jnp.where(b == q, s1[q:q + 1], i2)
        e = jnp.exp(cv - cv[0:1])
        rows = pl.ds(pl.multiple_of(h * k, k), k)
        i1t_ref[rows, :] = i1
        i2t_ref[rows, :] = i2
        gt_ref[rows, :] = e / jnp.sum(e, axis=0, keepdims=True)
        return c

    lax.fori_loop(0, nsub // 2, head, 0)
    i1_ref[...] = i1t_ref[...].T
    i2_ref[...] = i2t_ref[...].T
    g_ref[...] = gt_ref[...].T


def _peer_topk(s_t, *, nh, tl):
    rows, n = s_t.shape
    npair = nh * PEER_TOPK
    assert rows == nh * 2 * PEER_NKEYS and n % tl == 0 and tl % LANES == 0
    out = pl.BlockSpec((tl, npair), lambda i: (i, 0))
    return pl.pallas_call(
        _peer_topk_body,
        grid=(n // tl,),
        in_specs=[pl.BlockSpec((rows, tl), lambda i: (0, i))],
        out_specs=[out, out, out],
        out_shape=[jax.ShapeDtypeStruct((n, npair), I32), jax.ShapeDtypeStruct((n, npair), I32),
                   jax.ShapeDtypeStruct((n, npair), F32)],
        scratch_shapes=[pltpu.VMEM((nh * 2, PEER_TOPK, tl), F32), pltpu.VMEM((nh * 2, PEER_TOPK, tl), I32),
                        pltpu.VMEM((npair, tl), I32), pltpu.VMEM((npair, tl), I32), pltpu.VMEM((npair, tl), F32)],
        compiler_params=_params("parallel"),
        name="peer_topk",
    )(s_t)


def _peer_w_body(a_ref, b_ref, g_ref, w_ref):
    tt = a_ref.shape[0]
    npair = a_ref.shape[2]
    io = lax.broadcasted_iota(I32, (tt, PEER_NKEYS, npair), 1)
    one_a = jnp.where(a_ref[...] == io, 1.0, 0.0).astype(BF)
    g = g_ref[...]
    g_hi = g.astype(BF).astype(F32)
    g_lo = g - g_hi
    hit = b_ref[...] == io
    dims = (((2,), (2,)), ((0,), (0,)))
    w = lax.dot_general(one_a, jnp.where(hit, g_hi, 0.0).astype(BF), dims, preferred_element_type=F32)
    w = w + lax.dot_general(one_a, jnp.where(hit, g_lo, 0.0).astype(BF), dims, preferred_element_type=F32)
    w_ref[...] = w


def _peer_w(i1, i2, gate, *, tt):
    n, npair = i1.shape
    assert n % tt == 0
    blk = pl.BlockSpec((tt, 1, npair), lambda i: (i, 0, 0))
    return pl.pallas_call(
        _peer_w_body,
        grid=(n // tt,),
        in_specs=[blk, blk, blk],
        out_specs=pl.BlockSpec((tt, PEER_NKEYS, PEER_NKEYS), lambda i: (i, 0, 0)),
        out_shape=jax.ShapeDtypeStruct((n, PEER_NKEYS, PEER_NKEYS), F32),
        compiler_params=_params("parallel"),
        name="peer_w",
    )(i1.reshape(n, 1, npair), i2.reshape(n, 1, npair), gate.reshape(n, 1, npair))


def _peer_dense_body(xn_ref, xr_ref, u_ref, v_ref, w_ref, o_ref):
    @pl.when(pl.program_id(1) == 0)
    def _():
        o_ref[...] = xr_ref[...]

    nw = w_ref.shape[1]
    s = _dot_nt(xn_ref[...], u_ref[...])
    act = 0.5 * s * (1.0 + lax.erf(s * (2.0 ** -0.5)))
    wact = jnp.concatenate(
        [act[:, r * PEER_NKEYS:(r + 1) * PEER_NKEYS] * w_ref[:, r, :] for r in range(nw)], axis=1)
    o_ref[...] += _dot(wact.astype(BF), v_ref[...])


def _peer_dense(xn, x_res, u_bf, v_bf, w, *, tn, ec):
    n, d = xn.shape
    ne = u_bf.shape[0]
    nw = ec // PEER_NKEYS
    assert n % tn == 0 and ne % ec == 0 and ec % PEER_NKEYS == 0 and nw % 8 == 0
    return pl.pallas_call(
        _peer_dense_body,
        grid=(n // tn, ne // ec),
        in_specs=[
            pl.BlockSpec((tn, d), lambda i, j: (i, 0)),
            pl.BlockSpec((tn, d), lambda i, j: (i, 0)),
            pl.BlockSpec((ec, d), lambda i, j: (j, 0)),
            pl.BlockSpec((ec, d), lambda i, j: (j, 0)),
            pl.BlockSpec((tn, nw, PEER_NKEYS), lambda i, j: (i, j, 0)),
        ],
        out_specs=pl.BlockSpec((tn, d), lambda i, j: (i, 0)),
        out_shape=jax.ShapeDtypeStruct((n, d), F32),
        compiler_params=_params("parallel", "arbitrary"),
        name="peer_dense",
    )(xn, x_res, u_bf, v_bf, w)


def _peer_ffn(x, gain, wq, keys, u_tab, v_tab, *, tm):
    n, d = x.shape
    nh = keys.shape[0]
    s_t, xn = _peer_scores(x, gain, wq, keys.reshape(nh * 2, PEER_NKEYS, keys.shape[-1]), tm=512)
    i1, i2, gate = _peer_topk(s_t, nh=nh, tl=256)
    w = _peer_w(i1, i2, gate, tt=32)
    return _peer_dense(xn, x, u_tab.astype(BF), v_tab.astype(BF), w, tn=tm // 2, ec=1024)


def _reorder_att_w_in(w_in):
    d = w_in.shape[0]
    o2 = _DQ0
    ngate = 3 * NSA_HPG
    gates = [jnp.pad(w_in[:, o2 + g * ngate:o2 + (g + 1) * ngate], ((0, 0), (0, LANES - ngate)))
             for g in range(NSA_KV_HEADS)]
    return jnp.concatenate([w_in[:, :o2], w_in[:, o2 + 3 * NSA_HEADS:]] + gates, axis=1)


def _attention_layer(x, li, n_prompt, batch, seq, nb, ts, cache_nsa_l, cache_diff_l, win_state_l, page_table,
                     norm_g, w_in, w_out, q_gain, k_gain, cmp_w, dq_gain, dk_gain, lam_vec, out_gain, *, tm):
    hd = HEAD_DIM
    ng = NSA_KV_HEADS
    hg = NSA_HPG
    nh = DIFF_HEADS
    lam_init = 0.8 - 0.6 * math.exp(-0.3 * li)
    past = page_table.shape[1] * cache_nsa_l.shape[1]

    z = _mm(x, _reorder_att_w_in(w_in), gain=norm_g, tm=tm, tn=256, name="att_in")
    qn, rows, win, gates, dqn, drows = _att_post(z, q_gain, k_gain, dq_gain, dk_gain, tm=272)

    cmp = _nsa_cmp(rows, cmp_w, batch=batch, seq=seq)
    o_nsa_p = _nsa_prompt(qn, gates, cmp, rows, win, batch=batch, seq=seq, tq=128)
    o_diff_p = _diff_prompt(dqn, drows, lam_vec, out_gain, batch=batch, seq=seq, tq=256, lam_init=lam_init)

    nsa_past = cache_nsa_l[page_table].reshape(nb, past, 4, ng, hd).transpose(0, 2, 3, 1, 4)
    diff_past = cache_diff_l[page_table].reshape(nb, past, 2, nh, hd).transpose(0, 2, 3, 1, 4)
    win_st = win_state_l.transpose(0, 2, 3, 1, 4)
    tsp = 8
    assert ts <= tsp
    pad_t = lambda a, ax: jnp.pad(a, [(0, tsp - ts) if k == ax else (0, 0) for k in range(a.ndim)])
    pad8 = lambda a: pad_t(a.reshape(nb, ts, a.shape[-1]), 1)
    rows_s, win_s, drows_s = rows[n_prompt:], win[n_prompt:], drows[n_prompt:]
    q_s = pad_t(qn[n_prompt:].reshape(nb, ts, ng, hg, hd).transpose(0, 2, 3, 1, 4), 3).reshape(nb, ng, hg * tsp, hd)
    g_s = gates[n_prompt:].reshape(nb, ts, ng, LANES)[..., :3 * hg].reshape(nb, ts, ng, hg, 3)
    g_s = pad_t(g_s.transpose(0, 2, 3, 1, 4), 3).reshape(nb, ng, hg * tsp, 3)
    g_s = jnp.pad(g_s, ((0, 0), (0, 0), (0, 0), (0, LANES - 3)))
    o_nsa_s = _nsa_sample(q_s, g_s, nsa_past, win_st, pad8(rows_s), pad8(win_s), cmp_w, past=past, ts=ts)
    o_nsa_s = o_nsa_s.reshape(nb, ng, hg, tsp, hd)[:, :, :, :ts]
    o_nsa_s = o_nsa_s.transpose(0, 3, 1, 2, 4).reshape(nb * ts, ng * hg * hd)

    dq_s = pad_t(dqn[n_prompt:].reshape(nb, ts, nh, hd).transpose(0, 2, 1, 3), 2)
    lo = jnp.arange(hd) < DIFF_DK
    dq_s = jnp.concatenate([jnp.where(lo, dq_s, 0), jnp.where(lo, 0, dq_s)], axis=2).astype(BF)
    o_diff_s = _diff_sample(dq_s, diff_past, pad8(drows_s), lam_vec, out_gain, past=past, ts=ts, lam_init=lam_init)
    o_diff_s = o_diff_s[:, :, :ts].transpose(0, 2, 1, 3).reshape(nb * ts, nh * hd)

    mix = jnp.concatenate([jnp.concatenate([o_nsa_p, o_diff_p], axis=1),
                           jnp.concatenate([o_nsa_s, o_diff_s], axis=1).astype(BF)], axis=0)
    x = _mm(mix, w_out, res=x, tm=tm, tn=512, name="att_out")

    rows_p = rows[:n_prompt].reshape(batch, seq, 4, ng, hd)
    win_p = win[:n_prompt].reshape(batch, seq, 2, ng, hd)
    keep_p = min(WINDOW, seq)
    win_new_s = win_s.reshape(nb, ts, 2, ng, hd)
    keep_s = win_state_l.shape[1]
    state = (rows_p, rows_s.reshape(nb, ts, 4, ng, hd),
             win_p[:, seq - keep_p:], jnp.concatenate([win_state_l, win_new_s], axis=1)[:, -keep_s:],
             drows[:n_prompt].reshape(batch, seq, 2, nh, hd), drows_s.reshape(nb, ts, 2, nh, hd))
    return x, state


def _pool_conv_layer(x, n_prompt, batch, seq, nb, ts, past, pool_hist, conv_hist,
                     norm_g, w_in, w_out, pool_w, pool_scale, conv_w, *, tm):
    cw = pool_scale.shape[0]
    z = _mm(x, w_in, gain=norm_g, tm=tm, tn=512, name="pc_in")
    mix_p = _pc_prompt(z, pool_w, pool_scale, conv_w, batch=batch, seq=seq, ts=256)

    zs = z[n_prompt:].reshape(nb, ts, 4 * cw)
    p_s = zs[..., :cw]
    u_s = zs[..., 2 * cw:3 * cw] * zs[..., 3 * cw:]
    tpad = 8
    front = lambda hist: jnp.pad(hist, ((0, 0), (HALO - hist.shape[1], 0), (0, 0)))
    back = lambda a: jnp.pad(a, ((0, 0), (0, tpad - ts), (0, 0)))
    p_ext = jnp.concatenate([front(pool_hist), back(p_s)], axis=1)
    u_ext = jnp.concatenate([front(conv_hist), back(u_s)], axis=1)
    mix_s = _pc_sample(p_ext, u_ext, back(zs[..., cw:2 * cw]), pool_w, pool_scale, conv_w, past=past, tb=8)
    mix = jnp.concatenate([mix_p, mix_s[:, :ts].reshape(nb * ts, 2 * cw)], axis=0)
    x = _mm(mix, w_out, res=x, tm=tm, tn=512, name="pc_out")

    zp = z[:n_prompt].reshape(batch, seq, 4 * cw)
    u_tail = zp[:, seq - (CONV_K - 1):, 2 * cw:3 * cw] * zp[:, seq - (CONV_K - 1):, 3 * cw:]
    state = (zp[:, seq - POOL_HIST:, :cw],
             jnp.concatenate([pool_hist, p_s], axis=1)[:, -POOL_HIST:],
             u_tail,
             jnp.concatenate([conv_hist, u_s], axis=1)[:, -(CONV_K - 1):])
    return x, state


def kernel(x_prompt, x_sample, cache_nsa, cache_diff, state_nsa_win, state_pool, state_conv, page_table,
           att_norm_g, att_w_in, att_w_out, nsa_q_gain, nsa_k_gain, nsa_cmp_w, diff_q_gain, diff_k_gain,
           diff_lambda, diff_out_gain, pc_norm_g, pc_w_in, pc_w_out, pool_w, pool_scale, conv_w,
           ffn_norm_g, peer_wq, peer_keys, peer_u, peer_v):
    batch, seq, d = x_prompt.shape
    nb, ts, _ = x_sample.shape
    n_prompt = batch * seq
    n = n_prompt + nb * ts
    depth = ffn_norm_g.shape[0]
    past = page_table.shape[1] * cache_nsa.shape[2]
    tm = n // 8
    assert n % 8 == 0 and tm % 16 == 0
    x = jnp.concatenate([x_prompt.reshape(n_prompt, d), x_sample.reshape(nb * ts, d)], axis=0)
    att_states, pc_states = [], []
    for li in range(depth):
        i = li // 2
        if li % 2 == 0:
            x, st = _attention_layer(
                x, li, n_prompt, batch, seq, nb, ts, cache_nsa[i], cache_diff[i], state_nsa_win[i], page_table,
                att_norm_g[i], att_w_in[i], att_w_out[i], nsa_q_gain[i], nsa_k_gain[i], nsa_cmp_w[i],
                diff_q_gain[i], diff_k_gain[i], diff_lambda[i], diff_out_gain[i], tm=tm)
            att_states.append(st)
        else:
            x, st = _pool_conv_layer(
                x, n_prompt, batch, seq, nb, ts, past, state_pool[i], state_conv[i],
                pc_norm_g[i], pc_w_in[i], pc_w_out[i], pool_w[i], pool_scale[i], conv_w[i], tm=tm)
            pc_states.append(st)
        x = _peer_ffn(x, ffn_norm_g[li], peer_wq[li], peer_keys[li], peer_u[li], peer_v[li], tm=tm)
    stack = lambda states, k: jnp.stack([s[k] for s in states])
    return (x[:n_prompt].reshape(batch, seq, d), x[n_prompt:].reshape(nb, ts, d),
            stack(att_states, 0), stack(att_states, 1), stack(att_states, 2), stack(att_states, 3),
            stack(att_states, 4), stack(att_states, 5),
            stack(pc_states, 0), stack(pc_states, 1), stack(pc_states, 2), stack(pc_states, 3))
```

```python
import functools
import math

import jax
import jax.numpy as jnp
from jax import lax
from jax.experimental import pallas as pl
from jax.experimental.pallas import tpu as pltpu

F32 = jnp.float32
BF = jnp.bfloat16
I32 = jnp.int32

HEAD_DIM = 128
NSA_HEADS = 8
NSA_KV_HEADS = 2
NSA_HPG = NSA_HEADS // NSA_KV_HEADS
L_CMP = 32
L_SEL = 64
N_SEL = 16
WINDOW = 512
FORCE_BONUS = 1e4
DIFF_HEADS = 8
DIFF_DK = HEAD_DIM // 2
POOL_WINDOWS = (2, 4, 8, 16)
POOL_HIST = max(POOL_WINDOWS) - 1
CONV_K = 3
PEER_HEADS = 8
PEER_NKEYS = 128
PEER_TOPK = 16
EPS = 1e-6
NEG = -1e30
LANES = 128
HALO = 16
VMEM_LIMIT = 56 * 1024 * 1024

_NT = (((1,), (1,)), ((), ()))


def _params(*sem):
    return pltpu.CompilerParams(dimension_semantics=sem, vmem_limit_bytes=VMEM_LIMIT)


def _dot_nt(a, b):
    return lax.dot_general(a, b, _NT, preferred_element_type=F32)


def _dot(a, b):
    return jnp.dot(a, b, preferred_element_type=F32)


def _pow2_neg(k):
    return lax.bitcast_convert_type((127 - k) << 23, F32)


def _rms(x, g):
    return x * lax.rsqrt(jnp.mean(x * x, axis=-1, keepdims=True) + EPS) * g


def _rms_halves(x, g):
    lo = lax.broadcasted_iota(I32, x.shape, 1) < DIFF_DK
    x2 = x * x
    s_lo = jnp.sum(jnp.where(lo, x2, 0.0), axis=-1, keepdims=True)
    s_hi = jnp.sum(jnp.where(lo, 0.0, x2), axis=-1, keepdims=True)
    ms = jnp.where(lo, s_lo, s_hi) * (1.0 / DIFF_DK)
    return x * lax.rsqrt(ms + EPS) * g


def _msoftmax(s, mask):
    sm = jnp.where(mask, s, NEG)
    m = jnp.max(sm, axis=-1, keepdims=True)
    e = jnp.where(mask, jnp.exp(sm - m), 0.0)
    d = jnp.sum(e, axis=-1, keepdims=True)
    return e / jnp.where(d > 0.0, d, 1.0)


def _msoftmax2(s1, m1, s2, m2):
    a1 = jnp.where(m1, s1, NEG)
    a2 = jnp.where(m2, s2, NEG)
    m = jnp.maximum(jnp.max(a1, axis=-1, keepdims=True), jnp.max(a2, axis=-1, keepdims=True))
    e1 = jnp.where(m1, jnp.exp(a1 - m), 0.0)
    e2 = jnp.where(m2, jnp.exp(a2 - m), 0.0)
    d = jnp.sum(e1, axis=-1, keepdims=True) + jnp.sum(e2, axis=-1, keepdims=True)
    inv = 1.0 / jnp.where(d > 0.0, d, 1.0)
    return e1 * inv, e2 * inv


def _mm_body(*refs, norm, has_res):
    if norm:
        x_ref, g_ref, w_ref = refs[:3]
        rest = refs[3:]
    else:
        x_ref, w_ref = refs[:2]
        rest = refs[2:]
    if has_res:
        r_ref, o_ref, xb_ref = rest
    else:
        o_ref, xb_ref = rest

    @pl.when(pl.program_id(1) == 0)
    def _():
        x = x_ref[...]
        if norm:
            x = _rms(x.astype(F32), g_ref[...])
        xb_ref[...] = x.astype(BF)

    y = _dot(xb_ref[...], w_ref[...].astype(BF))
    if has_res:
        y = y + r_ref[...]
    o_ref[...] = y.astype(o_ref.dtype)


def _mm(x, w, *, gain=None, res=None, tm, tn, name):
    m, k = x.shape
    n = w.shape[1]
    assert m % tm == 0 and n % tn == 0
    norm = gain is not None
    in_specs = [pl.BlockSpec((tm, k), lambda i, j: (i, 0))]
    args = [x]
    if norm:
        in_specs.append(pl.BlockSpec((1, k), lambda i, j: (0, 0)))
        args.append(gain.reshape(1, k))
    in_specs.append(pl.BlockSpec((k, tn), lambda i, j: (0, j)))
    args.append(w)
    if res is not None:
        in_specs.append(pl.BlockSpec((tm, tn), lambda i, j: (i, j)))
        args.append(res)
    return pl.pallas_call(
        functools.partial(_mm_body, norm=norm, has_res=res is not None),
        grid=(m // tm, n // tn),
        in_specs=in_specs,
        out_specs=pl.BlockSpec((tm, tn), lambda i, j: (i, j)),
        out_shape=jax.ShapeDtypeStruct((m, n), F32),
        scratch_shapes=[pltpu.VMEM((tm, k), BF)],
        compiler_params=_params("parallel", "arbitrary"),
        name=name,
    )(*args)


_Q0 = 0
_KV0 = NSA_HEADS * HEAD_DIM
_DQ0 = _KV0 + 6 * NSA_KV_HEADS * HEAD_DIM
_DK0 = _DQ0 + DIFF_HEADS * HEAD_DIM
_DV0 = _DK0 + DIFF_HEADS * HEAD_DIM
_GT0 = _DV0 + DIFF_HEADS * HEAD_DIM
_ZW = _GT0 + NSA_KV_HEADS * LANES


def _att_post_body(z_ref, qg_ref, kg_ref, dqg_ref, dkg_ref,
                   qn_ref, rows_ref, win_ref, gate_ref, dqn_ref, drows_ref):
    hd = HEAD_DIM
    qg = qg_ref[...]
    for h in range(NSA_HEADS):
        x = z_ref[:, _Q0 + h * hd:_Q0 + (h + 1) * hd]
        qn_ref[:, h * hd:(h + 1) * hd] = (_rms(x, qg) * (HEAD_DIM ** -0.5)).astype(qn_ref.dtype)
    gw = NSA_KV_HEADS * hd
    for br in range(3):
        for g in range(NSA_KV_HEADS):
            ko = _KV0 + br * 2 * gw + g * hd
            k = _rms(z_ref[:, ko:ko + hd], kg_ref[br:br + 1, :])
            v = z_ref[:, ko + gw:ko + gw + hd]
            if br < 2:
                rows_ref[:, br * 2 * gw + g * hd:br * 2 * gw + (g + 1) * hd] = k
                rows_ref[:, br * 2 * gw + gw + g * hd:br * 2 * gw + gw + (g + 1) * hd] = v
            else:
                win_ref[:, g * hd:(g + 1) * hd] = k
                win_ref[:, gw + g * hd:gw + (g + 1) * hd] = v
    gate_ref[...] = jax.nn.sigmoid(z_ref[:, _GT0:_ZW])
    dqg = dqg_ref[...]
    dkg = dkg_ref[...]
    for h in range(DIFF_HEADS):
        x = z_ref[:, _DQ0 + h * hd:_DQ0 + (h + 1) * hd]
        dqn_ref[:, h * hd:(h + 1) * hd] = (_rms_halves(x, dqg) * (DIFF_DK ** -0.5)).astype(dqn_ref.dtype)
        x = z_ref[:, _DK0 + h * hd:_DK0 + (h + 1) * hd]
        drows_ref[:, h * hd:(h + 1) * hd] = _rms_halves(x, dkg)
    dvw = DIFF_HEADS * hd
    drows_ref[:, dvw:2 * dvw] = z_ref[:, _DV0:_DV0 + dvw]


def _att_post(z, q_gain, k_gain, dq_gain, dk_gain, *, tm):
    n = z.shape[0]
    assert n % tm == 0 and z.shape[1] == _ZW
    hd = HEAD_DIM
    widths = (NSA_HEADS * hd, 4 * NSA_KV_HEADS * hd, 2 * NSA_KV_HEADS * hd, NSA_KV_HEADS * LANES,
              DIFF_HEADS * hd, 2 * DIFF_HEADS * hd)
    dtypes = (BF, F32, F32, F32, BF, F32)
    small = lambda r: pl.BlockSpec((r, hd), lambda i: (0, 0))
    return pl.pallas_call(
        _att_post_body,
        grid=(n // tm,),
        in_specs=[pl.BlockSpec((tm, _ZW), lambda i: (i, 0)), small(1), small(3), small(1), small(1)],
        out_specs=[pl.BlockSpec((tm, w), lambda i: (i, 0)) for w in widths],
        out_shape=[jax.ShapeDtypeStruct((n, w), d) for w, d in zip(widths, dtypes)],
        compiler_params=_params("parallel"),
        name="att_post",
    )(z, q_gain.reshape(1, hd), k_gain, dq_gain.reshape(1, hd), dk_gain.reshape(1, hd))


def _compress(x, w):
    length, c = x.shape
    npair = length // (2 * L_CMP)
    x3 = x.reshape(npair, 2 * L_CMP, c)
    even = jnp.sum(x3[:, :L_CMP, :] * w, axis=1)
    odd = jnp.sum(x3[:, L_CMP:, :] * w, axis=1)
    pad = jnp.zeros((LANES - 2 * npair, c), F32)
    return jnp.concatenate([even, odd, pad], axis=0)


def _nsa_cmp_body(x_ref, w_ref, o_ref):
    o_ref[...] = _compress(x_ref[...], w_ref[...])


def _nsa_cmp(rows, cmp_w, *, batch, seq):
    gw = NSA_KV_HEADS * HEAD_DIM
    assert seq % (2 * L_CMP) == 0 and seq // L_CMP <= LANES
    w = jnp.concatenate([jnp.tile(cmp_w[0], (1, NSA_KV_HEADS)), jnp.tile(cmp_w[1], (1, NSA_KV_HEADS))], axis=1)
    return pl.pallas_call(
        _nsa_cmp_body,
        grid=(batch,),
        in_specs=[pl.BlockSpec((seq, 2 * gw), lambda b: (b, 0)),
                  pl.BlockSpec((L_CMP, 2 * gw), lambda b: (0, 0))],
        out_specs=pl.BlockSpec((None, LANES, 2 * gw), lambda b: (b, 0, 0)),
        out_shape=jax.ShapeDtypeStruct((batch, LANES, 2 * gw), F32),
        compiler_params=_params("parallel"),
        name="nsa_cmp",
    )(rows, w)


def _cmp_positions(ncmp):
    c = lax.broadcasted_iota(I32, (1, LANES), 1)
    half = ncmp // 2
    blk = jnp.where(c < half, 2 * c, 2 * (c - half) + 1)
    return c, blk * L_CMP + (L_CMP - 1), c < ncmp


def _select_blocks(p_heads, t1, ncmp, nsel_blocks):
    c = lax.broadcasted_iota(I32, (1, LANES), 1)
    half = ncmp // 2
    imp = p_heads[0]
    for p in p_heads[1:]:
        imp = imp + p
    imp = jnp.where(c < half, imp + pltpu.roll(imp, LANES - half, 1), 0.0)
    cur = t1 // L_SEL
    forced = (c == 0) | (c == cur) | (c == cur - 1)
    valid = (c * L_SEL <= t1) & (c < nsel_blocks)
    score = jnp.where(valid, imp + jnp.where(forced, FORCE_BONUS, 0.0), NEG)
    rank = jnp.zeros(score.shape, I32)
    for jp in range(nsel_blocks):
        sj = score[:, jp:jp + 1]
        beats = (sj > score) | ((sj == score) & (c > jp))
        rank = rank + beats.astype(I32)
    return (rank < min(N_SEL, nsel_blocks)) & (c < nsel_blocks)


def _expand_blocks(sel01, length):
    j = lax.broadcasted_iota(I32, (LANES, length), 0)
    l = lax.broadcasted_iota(I32, (LANES, length), 1)
    e = jnp.where(l // L_SEL == j, 1.0, 0.0).astype(BF)
    return _dot(sel01.astype(BF), e)


CAUSAL_STEP = 512


def _causal_variants(need, seq, fn, out_ref):
    step = min(CAUSAL_STEP, seq)
    assert seq % step == 0
    for klen in range(step, seq + 1, step):
        @pl.when((need <= klen) & (need > klen - step))
        def _(klen=klen):
            out_ref[...] = fn(klen)


def _nsa_prompt_body(q_ref, gate_ref, kc_ref, vc_ref, ks_ref, vs_ref, kw_ref, vw_ref, o_ref, os_ref, *, tq, seq):
    g = pl.program_id(1)
    t0 = pl.program_id(2) * tq
    hg = NSA_HPG
    hd = HEAD_DIM
    rows = hg * tq
    q = q_ref[...]
    qs = jnp.concatenate([q[:, h * hd:(h + 1) * hd] for h in range(hg)], axis=0)
    r_io = lax.broadcasted_iota(I32, (rows, 1), 0)
    hrow = r_io // tq
    tpos = t0 + (r_io - hrow * tq)
    slope = _pow2_neg(g * hg + hrow + 1)
    t1 = t0 + lax.broadcasted_iota(I32, (tq, 1), 0)
    ncmp = seq // L_CMP
    nblk = -(-seq // L_SEL)

    _, pc, cvalid = _cmp_positions(ncmp)
    s = _dot_nt(qs, kc_ref[...].astype(BF)) - slope * (tpos - pc).astype(F32)
    p = _msoftmax(s, cvalid & (pc <= tpos))
    o_c = _dot(p.astype(BF), vc_ref[...].astype(BF))
    sel = _select_blocks([p[h * tq:(h + 1) * tq] for h in range(hg)], t1, ncmp, nblk)
    sel01 = jnp.where(sel, 1.0, 0.0)

    def selected(klen):
        selm = jnp.concatenate([_expand_blocks(sel01, klen)] * hg, axis=0)
        dist = tpos - lax.broadcasted_iota(I32, (1, klen), 1)
        s = _dot_nt(qs, ks_ref[0:klen, :].astype(BF)) - slope * dist.astype(F32)
        p = _msoftmax(s, (dist >= 0) & (selm > 0.5))
        return _dot(p.astype(BF), vs_ref[0:klen, :].astype(BF))

    _causal_variants(t0 + tq, seq, selected, os_ref)
    o_s = os_ref[...]

    wl = WINDOW + tq
    start = pl.multiple_of(jnp.maximum(t0 - WINDOW, 0), tq)
    posw = start + lax.broadcasted_iota(I32, (1, wl), 1)
    dist = tpos - posw
    s = _dot_nt(qs, kw_ref[pl.ds(start, wl), :].astype(BF)) - slope * dist.astype(F32)
    p = _msoftmax(s, (dist >= 0) & (dist < WINDOW))
    o_w = _dot(p.astype(BF), vw_ref[pl.ds(start, wl), :].astype(BF))

    gt = gate_ref[...]
    for h in range(hg):
        rs = slice(h * tq, (h + 1) * tq)
        o = gt[:, 3 * h:3 * h + 1] * o_c[rs] + gt[:, 3 * h + 1:3 * h + 2] * o_s[rs] + gt[:, 3 * h + 2:3 * h + 3] * o_w[rs]
        o_ref[:, h * hd:(h + 1) * hd] = o.astype(o_ref.dtype)


def _nsa_prompt(qn, gates, cmp, rows, win, *, batch, seq, tq):
    hd = HEAD_DIM
    ng = NSA_KV_HEADS
    nq = seq // tq
    assert seq % tq == 0 and tq % LANES == 0 and WINDOW % tq == 0 and seq >= WINDOW + tq
    gq = NSA_HPG * hd
    seq_blk = lambda col: pl.BlockSpec((seq, hd), lambda b, g, i, col=col: (b, col + g))
    return pl.pallas_call(
        functools.partial(_nsa_prompt_body, tq=tq, seq=seq),
        grid=(batch, ng, nq),
        in_specs=[
            pl.BlockSpec((tq, gq), lambda b, g, i: (b * nq + i, g)),
            pl.BlockSpec((tq, LANES), lambda b, g, i: (b * nq + i, g)),
            pl.BlockSpec((None, LANES, hd), lambda b, g, i: (b, 0, g)),
            pl.BlockSpec((None, LANES, hd), lambda b, g, i: (b, 0, ng + g)),
            seq_blk(2 * ng), seq_blk(3 * ng),
            seq_blk(0), seq_blk(ng),
        ],
        out_specs=pl.BlockSpec((tq, gq), lambda b, g, i: (b * nq + i, g)),
        out_shape=jax.ShapeDtypeStruct((batch * seq, ng * gq), BF),
        scratch_shapes=[pltpu.VMEM((NSA_HPG * tq, hd), F32)],
        compiler_params=_params("parallel", "parallel", "arbitrary"),
        name="nsa_prompt",
    )(qn, gates, cmp, cmp, rows, rows, win, win)


def _lambda(lv, lam_init):
    a = jnp.sum(lv[0:1, :] * lv[1:2, :], axis=-1, keepdims=True)
    b = jnp.sum(lv[2:3, :] * lv[3:4, :], axis=-1, keepdims=True)
    return jnp.exp(a) - jnp.exp(b) + lam_init


def _diff_prompt_body(lv_ref, og_ref, q_ref, k_ref, v_ref, o_ref, acc_ref, *, tq, seq, lam_init):
    h = pl.program_id(1)
    t0 = pl.program_id(2) * tq
    q = q_ref[...].astype(F32)
    lo = lax.broadcasted_iota(I32, q.shape, 1) < DIFF_DK
    qs = jnp.concatenate([jnp.where(lo, q, 0.0), jnp.where(lo, 0.0, q)], axis=0).astype(BF)
    r_io = lax.broadcasted_iota(I32, (2 * tq, 1), 0)
    tpos = t0 + jnp.where(r_io >= tq, r_io - tq, r_io)
    slope = _pow2_neg(jnp.full((1, 1), h + 1, I32))
    lam = _lambda(lv_ref[...], lam_init)

    def attend(klen):
        dist = tpos - lax.broadcasted_iota(I32, (1, klen), 1)
        s = _dot_nt(qs, k_ref[0:klen, :].astype(BF)) - slope * dist.astype(F32)
        p = _msoftmax(s, dist >= 0)
        a = p[:tq] - lam * p[tq:]
        return _dot(a.astype(BF), v_ref[0:klen, :].astype(BF))

    _causal_variants(t0 + tq, seq, attend, acc_ref)
    o_ref[...] = (_rms(acc_ref[...], og_ref[...]) * (1.0 - lam_init)).astype(o_ref.dtype)


def _diff_prompt(dqn, drows, lam_vec, out_gain, *, batch, seq, tq, lam_init):
    hd = HEAD_DIM
    nh = DIFF_HEADS
    nq = seq // tq
    assert seq % tq == 0
    return pl.pallas_call(
        functools.partial(_diff_prompt_body, tq=tq, seq=seq, lam_init=lam_init),
        grid=(batch, nh, nq),
        in_specs=[
            pl.BlockSpec((4, DIFF_DK), lambda b, h, i: (0, 0)),
            pl.BlockSpec((1, hd), lambda b, h, i: (0, 0)),
            pl.BlockSpec((tq, hd), lambda b, h, i: (b * nq + i, h)),
            pl.BlockSpec((seq, hd), lambda b, h, i: (b, h)),
            pl.BlockSpec((seq, hd), lambda b, h, i: (b, nh + h)),
        ],
        out_specs=pl.BlockSpec((tq, hd), lambda b, h, i: (b * nq + i, h)),
        out_shape=jax.ShapeDtypeStruct((batch * seq, nh * hd), BF),
        scratch_shapes=[pltpu.VMEM((tq, hd), F32)],
        compiler_params=_params("parallel", "parallel", "arbitrary"),
        name="diff_prompt",
    )(lam_vec, out_gain.reshape(1, hd), dqn, drows, drows)


def _page_copies(pt_ref, cache_ref, buf_ref, sem_ref, b, slot, start):
    npages = pt_ref.shape[1]
    page, nkind, nhead = cache_ref.shape[1:4]

    def one_page(p, c):
        phys = pt_ref[b, p]
        dst_rows = pl.ds(pl.multiple_of(p * page, page), page)
        for kind in range(nkind):
            for h in range(nhead):
                cp = pltpu.make_async_copy(cache_ref.at[phys, :, kind, h, :],
                                           buf_ref.at[slot, kind, h, dst_rows, :], sem_ref.at[slot])
                if start:
                    cp.start()
                else:
                    cp.wait()
        return c

    lax.fori_loop(0, npages, one_page, 0)


def _state_copies(state_ref, buf_ref, sem_ref, b, slot, start):
    nkind, nhead = state_ref.shape[2:4]
    for kind in range(nkind):
        for h in range(nhead):
            cp = pltpu.make_async_copy(state_ref.at[b, :, kind, h, :], buf_ref.at[slot, kind, h], sem_ref.at[slot])
            if start:
                cp.start()
            else:
                cp.wait()


def _double_buffered(fetch):
    b = pl.program_id(0)
    slot = b % 2

    @pl.when(b == 0)
    def _():
        fetch(b, slot, True)

    @pl.when(b + 1 < pl.num_programs(0))
    def _():
        fetch(b + 1, 1 - slot, True)

    fetch(b, slot, False)
    return slot


def _nsa_sample_body(pt_ref, q_ref, gate_ref, nrow_ref, nwin_ref, cw_ref, cache_ref, wstate_ref, o_ref,
                     kv_buf, w_buf, kv_sem, w_sem, *, past, ts):
    def fetch(b, slot, start):
        _page_copies(pt_ref, cache_ref, kv_buf, kv_sem, b, slot, start)
        _state_copies(wstate_ref, w_buf, w_sem, b, slot, start)

    slot = _double_buffered(fetch)
    past_ref = kv_buf.at[slot]
    wst_ref = w_buf.at[slot]
    hg = NSA_HPG
    hd = HEAD_DIM
    ng = NSA_KV_HEADS
    rows = q_ref.shape[1]
    tsp = rows // hg
    npad = nrow_ref.shape[0]
    gw = ng * hd
    ncmp = (past + ts) // L_CMP
    nblk = -(-(past + ts) // L_SEL)
    new_blk = past // L_SEL
    r_io = lax.broadcasted_iota(I32, (rows, 1), 0)
    hrow = r_io // tsp
    trow = r_io - hrow * tsp
    tpos = past + trow
    t1 = past + lax.broadcasted_iota(I32, (tsp, 1), 0)
    jn = lax.broadcasted_iota(I32, (1, npad), 1)
    new_ok = (jn <= trow) & (jn < ts)
    dist_new = (trow - jn).astype(F32)
    _, pc, cvalid = _cmp_positions(ncmp)
    pos = lax.broadcasted_iota(I32, (1, past), 1)
    nwst = wst_ref.shape[2]
    posw = past - nwst + lax.broadcasted_iota(I32, (1, nwst), 1)
    for g in range(ng):
        qs = q_ref[g]
        slope = _pow2_neg(g * hg + hrow + 1)
        kc = _compress(past_ref[0, g], cw_ref[0]).astype(BF)
        vc = _compress(past_ref[1, g], cw_ref[1]).astype(BF)
        s = _dot_nt(qs, kc) - slope * (tpos - pc).astype(F32)
        p = _msoftmax(s, cvalid & (pc <= tpos))
        o_c = _dot(p.astype(BF), vc)
        sel = _select_blocks([p[h * tsp:(h + 1) * tsp] for h in range(hg)], t1, ncmp, nblk)
        sel01 = jnp.concatenate([jnp.where(sel, 1.0, 0.0)] * hg, axis=0)
        selm = _expand_blocks(sel01, past)
        sel_new = sel01[:, new_blk:new_blk + 1] > 0.5
        kn = nrow_ref[:, 2 * gw + g * hd:2 * gw + (g + 1) * hd].astype(BF)
        vn = nrow_ref[:, 3 * gw + g * hd:3 * gw + (g + 1) * hd].astype(BF)
        s1 = _dot_nt(qs, past_ref[2, g].astype(BF)) - slope * (tpos - pos).astype(F32)
        s2 = _dot_nt(qs, kn) - slope * dist_new
        p1, p2 = _msoftmax2(s1, selm > 0.5, s2, new_ok & sel_new)
        o_s = _dot(p1.astype(BF), past_ref[3, g].astype(BF)) + _dot(p2.astype(BF), vn)
        kn = nwin_ref[:, g * hd:(g + 1) * hd].astype(BF)
        vn = nwin_ref[:, gw + g * hd:gw + (g + 1) * hd].astype(BF)
        dist = tpos - posw
        s1 = _dot_nt(qs, wst_ref[0, g].astype(BF)) - slope * dist.astype(F32)
        s2 = _dot_nt(qs, kn) - slope * dist_new
        p1, p2 = _msoftmax2(s1, (dist >= 0) & (dist < WINDOW), s2, new_ok)
        o_w = _dot(p1.astype(BF), wst_ref[1, g].astype(BF)) + _dot(p2.astype(BF), vn)
        gt = gate_ref[g]
        o_ref[g] = gt[:, 0:1] * o_c + gt[:, 1:2] * o_s + gt[:, 2:3] * o_w


def _nsa_sample(q, gates, cache, page_table, win_state, new_rows, new_win, cmp_w, *, ts):
    nb, ng, rows, hd = q.shape
    past = page_table.shape[1] * cache.shape[1]
    nwst = win_state.shape[1]
    npad = new_rows.shape[1]
    assert past % (2 * L_CMP) == 0 and past % L_SEL == 0 and nwst == WINDOW and (past + ts) // L_CMP == past // L_CMP
    assert cache.shape[2:] == (4, ng, hd) and win_state.shape[2:] == (2, ng, hd)
    return pl.pallas_call(
        functools.partial(_nsa_sample_body, past=past, ts=ts),
        grid_spec=pltpu.PrefetchScalarGridSpec(
            num_scalar_prefetch=1,
            grid=(nb,),
            in_specs=[
                pl.BlockSpec((None, ng, rows, hd), lambda b, pt: (b, 0, 0, 0)),
                pl.BlockSpec((None, ng, rows, LANES), lambda b, pt: (b, 0, 0, 0)),
                pl.BlockSpec((None, npad, new_rows.shape[2]), lambda b, pt: (b, 0, 0)),
                pl.BlockSpec((None, npad, new_win.shape[2]), lambda b, pt: (b, 0, 0)),
                pl.BlockSpec((2, L_CMP, hd), lambda b, pt: (0, 0, 0)),
                pl.BlockSpec(memory_space=pl.ANY),
                pl.BlockSpec(memory_space=pl.ANY),
            ],
            out_specs=pl.BlockSpec((None, ng, rows, hd), lambda b, pt: (b, 0, 0, 0)),
            scratch_shapes=[pltpu.VMEM((2, 4, ng, past, hd), F32), pltpu.VMEM((2, 2, ng, nwst, hd), F32),
                            pltpu.SemaphoreType.DMA((2,)), pltpu.SemaphoreType.DMA((2,))],
        ),
        out_shape=jax.ShapeDtypeStruct((nb, ng, rows, hd), F32),
        compiler_params=_params("arbitrary"),
        name="nsa_sample",
    )(page_table, q, gates, new_rows, new_win, cmp_w, cache, win_state)


def _diff_sample_body(pt_ref, lv_ref, og_ref, q_ref, ndrow_ref, cache_ref, o_ref, kv_buf, kv_sem, *, past, ts, lam_init):
    slot = _double_buffered(functools.partial(_page_copies, pt_ref, cache_ref, kv_buf, kv_sem))
    past_ref = kv_buf.at[slot]
    hd = HEAD_DIM
    nh = DIFF_HEADS
    npad = ndrow_ref.shape[0]
    rows = q_ref.shape[1]
    tsp = rows // 2
    r_io = lax.broadcasted_iota(I32, (rows, 1), 0)
    trow = jnp.where(r_io >= tsp, r_io - tsp, r_io)
    tpos = past + trow
    dist = (tpos - lax.broadcasted_iota(I32, (1, past), 1)).astype(F32)
    jn = lax.broadcasted_iota(I32, (1, npad), 1)
    new_ok = (jn <= trow) & (jn < ts)
    dist_new = (trow - jn).astype(F32)
    lam = _lambda(lv_ref[...], lam_init)
    og = og_ref[...]
    for h in range(nh):
        slope = 2.0 ** -(h + 1)
        qs = q_ref[h]
        kn = ndrow_ref[:, h * hd:(h + 1) * hd].astype(BF)
        vn = ndrow_ref[:, (nh + h) * hd:(nh + h + 1) * hd].astype(BF)
        s1 = _dot_nt(qs, past_ref[0, h].astype(BF)) - slope * dist
        s2 = _dot_nt(qs, kn) - slope * dist_new
        p1, p2 = _msoftmax2(s1, dist >= 0.0, s2, new_ok)
        a1 = p1 - lam * jnp.concatenate([p1[tsp:], p1[:tsp]], axis=0)
        a2 = p2 - lam * jnp.concatenate([p2[tsp:], p2[:tsp]], axis=0)
        o = _dot(a1.astype(BF), past_ref[1, h].astype(BF)) + _dot(a2.astype(BF), vn)
        o_ref[h] = _rms(o, og) * (1.0 - lam_init)


def _diff_sample(q, cache, page_table, new_drows, lam_vec, out_gain, *, ts, lam_init):
    nb, nh, rows, hd = q.shape
    past = page_table.shape[1] * cache.shape[1]
    npad = new_drows.shape[1]
    assert cache.shape[2:] == (2, nh, hd)
    return pl.pallas_call(
        functools.partial(_diff_sample_body, past=past, ts=ts, lam_init=lam_init),
        grid_spec=pltpu.PrefetchScalarGridSpec(
            num_scalar_prefetch=1,
            grid=(nb,),
            in_specs=[
                pl.BlockSpec((4, DIFF_DK), lambda b, pt: (0, 0)),
                pl.BlockSpec((1, hd), lambda b, pt: (0, 0)),
                pl.BlockSpec((None, nh, rows, hd), lambda b, pt: (b, 0, 0, 0)),
                pl.BlockSpec((None, npad, new_drows.shape[2]), lambda b, pt: (b, 0, 0)),
                pl.BlockSpec(memory_space=pl.ANY),
            ],
            out_specs=pl.BlockSpec((None, nh, rows, hd), lambda b, pt: (b, 0, 0, 0)),
            scratch_shapes=[pltpu.VMEM((2, 2, nh, past, hd), F32), pltpu.SemaphoreType.DMA((2,))],
        ),
        out_shape=jax.ShapeDtypeStruct((nb, nh, rows, hd), F32),
        compiler_params=_params("arbitrary"),
        name="diff_sample",
    )(page_table, lam_vec, out_gain.reshape(1, hd), q, new_drows, cache)


def _rows(a, lo, n):
    return lax.slice_in_dim(a, lo, lo + n, axis=a.ndim - 2)


def _pool_conv(p_ext, u_ext, b_gate, cnt_pos, pool_w_ref, pool_scale, conv_w, ts):
    gwid = p_ext.shape[-1] // len(POOL_WINDOWS)
    lead = p_ext.shape[:-2]
    outs = []
    for gi, w in enumerate(POOL_WINDOWS):
        x = p_ext[..., gi * gwid:(gi + 1) * gwid]
        acc = x
        span = 1
        while span < w:
            n = acc.shape[-2]
            acc = _rows(acc, span, n - span) + _rows(acc, 0, n - span)
            span *= 2
        win_sum = _rows(acc, HALO - (w - 1), ts)
        cnt = jnp.minimum(w, cnt_pos + 1).astype(F32)
        m = win_sum / cnt - _rows(x, HALO, ts)
        m2 = m.reshape((-1, gwid)).astype(BF)
        outs.append(_dot(m2, pool_w_ref[gi].astype(BF)).reshape(lead + (ts, gwid)))
    y_pool = jnp.concatenate(outs, axis=-1) * pool_scale
    conv = None
    for j in range(CONV_K):
        term = _rows(u_ext, HALO - (CONV_K - 1) + j, ts) * conv_w[j:j + 1, :]
        conv = term if conv is None else conv + term
    return jnp.concatenate([y_pool, b_gate * conv], axis=-1).astype(BF)


def _pc_prompt_body(zc_ref, zh_ref, pw_ref, ps_ref, cw_ref, o_ref, *, ts, cw):
    i = pl.program_id(1)
    keep = i > 0
    p_ext = jnp.concatenate([jnp.where(keep, zh_ref[:, 0:cw], 0.0), zc_ref[:, 0:cw]], axis=0)
    u_h = jnp.where(keep, zh_ref[:, 2 * cw:3 * cw] * zh_ref[:, 3 * cw:4 * cw], 0.0)
    u_ext = jnp.concatenate([u_h, zc_ref[:, 2 * cw:3 * cw] * zc_ref[:, 3 * cw:4 * cw]], axis=0)
    cnt_pos = i * ts + lax.broadcasted_iota(I32, (ts, 1), 0)
    o_ref[...] = _pool_conv(p_ext, u_ext, zc_ref[:, cw:2 * cw], cnt_pos, pw_ref, ps_ref[...], cw_ref[...], ts)


def _pc_prompt(z, pool_w, pool_scale, conv_w, *, batch, seq, ts):
    cw = pool_scale.shape[0]
    nt = seq // ts
    assert seq % ts == 0 and ts % HALO == 0 and z.shape[1] == 4 * cw
    hb = ts // HALO
    return pl.pallas_call(
        functools.partial(_pc_prompt_body, ts=ts, cw=cw),
        grid=(batch, nt),
        in_specs=[
            pl.BlockSpec((ts, 4 * cw), lambda b, i: (b * nt + i, 0)),
            pl.BlockSpec((HALO, 4 * cw), lambda b, i: (jnp.maximum((b * nt + i) * hb - 1, 0), 0)),
            pl.BlockSpec(pool_w.shape, lambda b, i: (0, 0, 0)),
            pl.BlockSpec((1, cw), lambda b, i: (0, 0)),
            pl.BlockSpec((CONV_K, cw), lambda b, i: (0, 0)),
        ],
        out_specs=pl.BlockSpec((ts, 2 * cw), lambda b, i: (b * nt + i, 0)),
        out_shape=jax.ShapeDtypeStruct((batch * seq, 2 * cw), BF),
        compiler_params=_params("parallel", "arbitrary"),
        name="pool_conv_prompt",
    )(z, z, pool_w, pool_scale.reshape(1, cw), conv_w)


def _pc_sample_body(pe_ref, ue_ref, bg_ref, pw_ref, ps_ref, cw_ref, o_ref, *, ts, past):
    cnt_pos = past + lax.broadcasted_iota(I32, (ts, 1), 0)
    o_ref[...] = _pool_conv(pe_ref[...], ue_ref[...], bg_ref[...], cnt_pos, pw_ref, ps_ref[...], cw_ref[...], ts)


def _pc_sample(p_ext, u_ext, b_gate, pool_w, pool_scale, conv_w, *, past, tb):
    nb, ext, cw = p_ext.shape
    ts = ext - HALO
    assert nb % tb == 0
    blk = lambda r: pl.BlockSpec((tb, r, cw), lambda b: (b, 0, 0))
    return pl.pallas_call(
        functools.partial(_pc_sample_body, ts=ts, past=past),
        grid=(nb // tb,),
        in_specs=[blk(ext), blk(ext), blk(ts),
                  pl.BlockSpec(pool_w.shape, lambda b: (0, 0, 0)),
                  pl.BlockSpec((1, cw), lambda b: (0, 0)),
                  pl.BlockSpec((CONV_K, cw), lambda b: (0, 0))],
        out_specs=pl.BlockSpec((tb, ts, 2 * cw), lambda b: (b, 0, 0)),
        out_shape=jax.ShapeDtypeStruct((nb, ts, 2 * cw), BF),
        compiler_params=_params("parallel"),
        name="pool_conv_sample",
    )(p_ext, u_ext, b_gate, pool_w, pool_scale.reshape(1, cw), conv_w)


def _peer_score_body(x_ref, g_ref, wq_ref, keys_ref, s_ref, xn_ref):
    @pl.when(pl.program_id(1) == 0)
    def _():
        xn_ref[...] = _rms(x_ref[...], g_ref[...]).astype(xn_ref.dtype)

    z = _dot(xn_ref[...], wq_ref[...].astype(BF)).astype(BF)
    nk = keys_ref.shape[1]
    dk = keys_ref.shape[2]
    for r in range(keys_ref.shape[0]):
        s_ref[r * nk:(r + 1) * nk, :] = _dot_nt(keys_ref[r].astype(BF), z[:, r * dk:(r + 1) * dk])


def _peer_scores(x, gain, wq, keys, *, tm):
    n, d = x.shape
    nsub, nk, dk = keys.shape
    per = 2
    assert n % tm == 0 and tm % LANES == 0 and nsub % per == 0 and wq.shape[1] == nsub * dk
    return pl.pallas_call(
        _peer_score_body,
        grid=(n // tm, nsub // per),
        in_specs=[
            pl.BlockSpec((tm, d), lambda i, j: (i, 0)),
            pl.BlockSpec((1, d), lambda i, j: (0, 0)),
            pl.BlockSpec((d, per * dk), lambda i, j: (0, j)),
            pl.BlockSpec((per, nk, dk), lambda i, j: (j, 0, 0)),
        ],
        out_specs=[pl.BlockSpec((per * nk, tm), lambda i, j: (j, i)),
                   pl.BlockSpec((tm, d), lambda i, j: (i, 0))],
        out_shape=[jax.ShapeDtypeStruct((nsub * nk, n), F32), jax.ShapeDtypeStruct((n, d), BF)],
        compiler_params=_params("parallel", "arbitrary"),
        name="peer_scores",
    )(x, gain.reshape(1, d), wq, keys)


def _topk_rows(x, k, code):
    slot = lax.broadcasted_iota(I32, (k, x.shape[1]), 0)
    vals = jnp.zeros((k, x.shape[1]), F32)
    idxs = jnp.zeros((k, x.shape[1]), F32)
    for kk in range(k):
        m = jnp.max(x, axis=0, keepdims=True)
        idx = jnp.min(jnp.where(x == m, code, float(2 ** 24)), axis=0, keepdims=True)
        vals = jnp.where(slot == kk, m, vals)
        idxs = jnp.where(slot == kk, idx, idxs)
        x = jnp.where(code == idx, -jnp.inf, x)
    return vals, idxs.astype(I32)


def _pair_candidates(v0, v1):
    k = PEER_TOPK
    assert k == 16 and v0.shape[0] == k
    half = k // 2
    blocks = [v0[0:1] + v1] + [v0[a:a + 1] + v1[0:half] for a in range(1, half)] + [v0[half:] + v1[0:1]]
    r = lax.broadcasted_iota(I32, (k + (half - 1) * half + half, v0.shape[1]), 0)
    mid = r - k
    code = jnp.where(r < k, r,
                     jnp.where(mid < (half - 1) * half, (mid // half + 1) * k + mid % half,
                               (half + mid - (half - 1) * half) * k))
    return jnp.concatenate(blocks, axis=0), code.astype(F32)


def _peer_topk_body(s_ref, i1_ref, i2_ref, g_ref, sv_ref, si_ref, i1t_ref, i2t_ref, gt_ref):
    nsub = sv_ref.shape[0]
    k = PEER_TOPK
    nk = PEER_NKEYS

    def sub_key(gi, c):
        x = s_ref[pl.ds(pl.multiple_of(gi * nk, nk), nk), :]
        v, i = _topk_rows(x, k, lax.broadcasted_iota(I32, x.shape, 0).astype(F32))
        sv_ref[gi] = v
        si_ref[gi] = i
        return c

    lax.fori_loop(0, nsub, sub_key, 0)

    def head(h, c):
        comb, code = _pair_candidates(sv_ref[2 * h], sv_ref[2 * h + 1])
        cv, ci = _topk_rows(comb, k, code)
        a = ci // k
        b = ci - a * k
        s0 = si_ref[2 * h]
        s1 = si_ref[2 * h + 1]
        i1 = jnp.zeros(ci.shape, I32)
        i2 = jnp.zeros(ci.shape, I32)
        for q in range(k):
            i1 = jnp.where(a == q, s0[q:q + 1], i1)
            i2 = jnp.where(b == q, s1[q:q + 1], i2)
        e = jnp.exp(cv - cv[0:1])
        rows = pl.ds(pl.multiple_of(h * k, k), k)
        i1t_ref[rows, :] = i1
        i2t_ref[rows, :] = i2
        gt_ref[rows, :] = e / jnp.sum(e, axis=0, keepdims=True)
        return c

    lax.fori_loop(0, nsub // 2, head, 0)
    i1_ref[...] = i1t_ref[...].T
    i2_ref[...] = i2t_ref[...].T
    g_ref[...] = gt_ref[...].T


def _peer_topk(s_t, *, nh, tl):
    rows, n = s_t.shape
    npair = nh * PEER_TOPK
    assert rows == nh * 2 * PEER_NKEYS and n % tl == 0 and tl % LANES == 0
    out = pl.BlockSpec((tl, npair), lambda i: (i, 0))
    return pl.pallas_call(
        _peer_topk_body,
        grid=(n // tl,),
        in_specs=[pl.BlockSpec((rows, tl), lambda i: (0, i))],
        out_specs=[out, out, out],
        out_shape=[jax.ShapeDtypeStruct((n, npair), I32), jax.ShapeDtypeStruct((n, npair), I32),
                   jax.ShapeDtypeStruct((n, npair), F32)],
        scratch_shapes=[pltpu.VMEM((nh * 2, PEER_TOPK, tl), F32), pltpu.VMEM((nh * 2, PEER_TOPK, tl), I32),
                        pltpu.VMEM((npair, tl), I32), pltpu.VMEM((npair, tl), I32), pltpu.VMEM((npair, tl), F32)],
        compiler_params=_params("parallel"),
        name="peer_topk",
    )(s_t)


def _peer_w_body(a_ref, b_ref, g_ref, w_ref):
    tt = a_ref.shape[0]
    npair = a_ref.shape[2]
    io = lax.broadcasted_iota(I32, (tt, PEER_NKEYS, npair), 1)
    one_a = jnp.where(a_ref[...] == io, 1.0, 0.0).astype(BF)
    g = g_ref[...]
    g_hi = g.astype(BF).astype(F32)
    g_lo = g - g_hi
    hit = b_ref[...] == io
    dims = (((2,), (2,)), ((0,), (0,)))
    w = lax.dot_general(one_a, jnp.where(hit, g_hi, 0.0).astype(BF), dims, preferred_element_type=F32)
    w = w + lax.dot_general(one_a, jnp.where(hit, g_lo, 0.0).astype(BF), dims, preferred_element_type=F32)
    w_ref[...] = w


def _peer_w(i1, i2, gate, *, tt):
    n, npair = i1.shape
    assert n % tt == 0
    blk = pl.BlockSpec((tt, 1, npair), lambda i: (i, 0, 0))
    return pl.pallas_call(
        _peer_w_body,
        grid=(n // tt,),
        in_specs=[blk, blk, blk],
        out_specs=pl.BlockSpec((tt, PEER_NKEYS, PEER_NKEYS), lambda i: (i, 0, 0)),
        out_shape=jax.ShapeDtypeStruct((n, PEER_NKEYS, PEER_NKEYS), F32),
        compiler_params=_params("parallel"),
        name="peer_w",
    )(i1.reshape(n, 1, npair), i2.reshape(n, 1, npair), gate.reshape(n, 1, npair))


def _peer_dense_body(xn_ref, xr_ref, u_ref, v_ref, w_ref, o_ref):
    @pl.when(pl.program_id(1) == 0)
    def _():
        o_ref[...] = xr_ref[...]

    nw = w_ref.shape[1]
    s = _dot_nt(xn_ref[...], u_ref[...])
    act = 0.5 * s * (1.0 + lax.erf(s * (2.0 ** -0.5)))
    wact = jnp.concatenate(
        [act[:, r * PEER_NKEYS:(r + 1) * PEER_NKEYS] * w_ref[:, r, :] for r in range(nw)], axis=1)
    o_ref[...] += _dot(wact.astype(BF), v_ref[...])


def _peer_dense(xn, x_res, u_bf, v_bf, w, *, tn, ec):
    n, d = xn.shape
    ne = u_bf.shape[0]
    nw = ec // PEER_NKEYS
    assert n % tn == 0 and ne % ec == 0 and ec % PEER_NKEYS == 0 and nw % 8 == 0
    return pl.pallas_call(
        _peer_dense_body,
        grid=(n // tn, ne // ec),
        in_specs=[
            pl.BlockSpec((tn, d), lambda i, j: (i, 0)),
            pl.BlockSpec((tn, d), lambda i, j: (i, 0)),
            pl.BlockSpec((ec, d), lambda i, j: (j, 0)),
            pl.BlockSpec((ec, d), lambda i, j: (j, 0)),
            pl.BlockSpec((tn, nw, PEER_NKEYS), lambda i, j: (i, j, 0)),
        ],
        out_specs=pl.BlockSpec((tn, d), lambda i, j: (i, 0)),
        out_shape=jax.ShapeDtypeStruct((n, d), F32),
        compiler_params=_params("parallel", "arbitrary"),
        name="peer_dense",
    )(xn, x_res, u_bf, v_bf, w)


def _peer_ffn(x, gain, wq, keys, u_tab, v_tab, *, tm):
    n, d = x.shape
    nh = keys.shape[0]
    s_t, xn = _peer_scores(x, gain, wq, keys.reshape(nh * 2, PEER_NKEYS, keys.shape[-1]), tm=512)
    i1, i2, gate = _peer_topk(s_t, nh=nh, tl=256)
    w = _peer_w(i1, i2, gate, tt=32)
    return _peer_dense(xn, x, u_tab.astype(BF), v_tab.astype(BF), w, tn=tm // 2, ec=1024)


def _reorder_att_w_in(w_in):
    d = w_in.shape[0]
    o2 = _DQ0
    ngate = 3 * NSA_HPG
    gates = [jnp.pad(w_in[:, o2 + g * ngate:o2 + (g + 1) * ngate], ((0, 0), (0, LANES - ngate)))
             for g in range(NSA_KV_HEADS)]
    return jnp.concatenate([w_in[:, :o2], w_in[:, o2 + 3 * NSA_HEADS:]] + gates, axis=1)


def _attention_layer(x, li, n_prompt, batch, seq, nb, ts, cache_nsa_l, cache_diff_l, win_state_l, page_table,
                     norm_g, w_in, w_out, q_gain, k_gain, cmp_w, dq_gain, dk_gain, lam_vec, out_gain, *, tm):
    hd = HEAD_DIM
    ng = NSA_KV_HEADS
    hg = NSA_HPG
    nh = DIFF_HEADS
    lam_init = 0.8 - 0.6 * math.exp(-0.3 * li)

    z = _mm(x, _reorder_att_w_in(w_in), gain=norm_g, tm=tm, tn=256, name="att_in")
    qn, rows, win, gates, dqn, drows = _att_post(z, q_gain, k_gain, dq_gain, dk_gain, tm=272)

    cmp = _nsa_cmp(rows, cmp_w, batch=batch, seq=seq)
    o_nsa_p = _nsa_prompt(qn, gates, cmp, rows, win, batch=batch, seq=seq, tq=128)
    o_diff_p = _diff_prompt(dqn, drows, lam_vec, out_gain, batch=batch, seq=seq, tq=256, lam_init=lam_init)

    tsp = 8
    assert ts <= tsp
    pad_t = lambda a, ax: jnp.pad(a, [(0, tsp - ts) if k == ax else (0, 0) for k in range(a.ndim)])
    pad8 = lambda a: pad_t(a.reshape(nb, ts, a.shape[-1]), 1)
    rows_s, win_s, drows_s = rows[n_prompt:], win[n_prompt:], drows[n_prompt:]
    q_s = pad_t(qn[n_prompt:].reshape(nb, ts, ng, hg, hd).transpose(0, 2, 3, 1, 4), 3).reshape(nb, ng, hg * tsp, hd)
    g_s = gates[n_prompt:].reshape(nb, ts, ng, LANES)[..., :3 * hg].reshape(nb, ts, ng, hg, 3)
    g_s = pad_t(g_s.transpose(0, 2, 3, 1, 4), 3).reshape(nb, ng, hg * tsp, 3)
    g_s = jnp.pad(g_s, ((0, 0), (0, 0), (0, 0), (0, LANES - 3)))
    o_nsa_s = _nsa_sample(q_s, g_s, cache_nsa_l, page_table, win_state_l, pad8(rows_s), pad8(win_s), cmp_w, ts=ts)
    o_nsa_s = o_nsa_s.reshape(nb, ng, hg, tsp, hd)[:, :, :, :ts]
    o_nsa_s = o_nsa_s.transpose(0, 3, 1, 2, 4).reshape(nb * ts, ng * hg * hd)

    dq_s = pad_t(dqn[n_prompt:].reshape(nb, ts, nh, hd).transpose(0, 2, 1, 3), 2)
    lo = jnp.arange(hd) < DIFF_DK
    dq_s = jnp.concatenate([jnp.where(lo, dq_s, 0), jnp.where(lo, 0, dq_s)], axis=2).astype(BF)
    o_diff_s = _diff_sample(dq_s, cache_diff_l, page_table, pad8(drows_s), lam_vec, out_gain, ts=ts, lam_init=lam_init)
    o_diff_s = o_diff_s[:, :, :ts].transpose(0, 2, 1, 3).reshape(nb * ts, nh * hd)

    mix = jnp.concatenate([jnp.concatenate([o_nsa_p, o_diff_p], axis=1),
                           jnp.concatenate([o_nsa_s, o_diff_s], axis=1).astype(BF)], axis=0)
    x = _mm(mix, w_out, res=x, tm=tm, tn=512, name="att_out")

    rows_p = rows[:n_prompt].reshape(batch, seq, 4, ng, hd)
    win_p = win[:n_prompt].reshape(batch, seq, 2, ng, hd)
    keep_p = min(WINDOW, seq)
    win_new_s = win_s.reshape(nb, ts, 2, ng, hd)
    keep_s = win_state_l.shape[1]
    state = (rows_p, rows_s.reshape(nb, ts, 4, ng, hd),
             win_p[:, seq - keep_p:], jnp.concatenate([win_state_l, win_new_s], axis=1)[:, -keep_s:],
             drows[:n_prompt].reshape(batch, seq, 2, nh, hd), drows_s.reshape(nb, ts, 2, nh, hd))
    return x, state


def _pool_conv_layer(x, n_prompt, batch, seq, nb, ts, past, pool_hist, conv_hist,
                     norm_g, w_in, w_out, pool_w, pool_scale, conv_w, *, tm):
    cw = pool_scale.shape[0]
    z = _mm(x, w_in, gain=norm_g, tm=tm, tn=512, name="pc_in")
    mix_p = _pc_prompt(z, pool_w, pool_scale, conv_w, batch=batch, seq=seq, ts=256)

    zs = z[n_prompt:].reshape(nb, ts, 4 * cw)
    p_s = zs[..., :cw]
    u_s = zs[..., 2 * cw:3 * cw] * zs[..., 3 * cw:]
    tpad = 8
    front = lambda hist: jnp.pad(hist, ((0, 0), (HALO - hist.shape[1], 0), (0, 0)))
    back = lambda a: jnp.pad(a, ((0, 0), (0, tpad - ts), (0, 0)))
    p_ext = jnp.concatenate([front(pool_hist), back(p_s)], axis=1)
    u_ext = jnp.concatenate([front(conv_hist), back(u_s)], axis=1)
    mix_s = _pc_sample(p_ext, u_ext, back(zs[..., cw:2 * cw]), pool_w, pool_scale, conv_w, past=past, tb=8)
    mix = jnp.concatenate([mix_p, mix_s[:, :ts].reshape(nb * ts, 2 * cw)], axis=0)
    x = _mm(mix, w_out, res=x, tm=tm, tn=512, name="pc_out")

    zp = z[:n_prompt].reshape(batch, seq, 4 * cw)
    u_tail = zp[:, seq - (CONV_K - 1):, 2 * cw:3 * cw] * zp[:, seq - (CONV_K - 1):, 3 * cw:]
    state = (zp[:, seq - POOL_HIST:, :cw],
             jnp.concatenate([pool_hist, p_s], axis=1)[:, -POOL_HIST:],
             u_tail,
             jnp.concatenate([conv_hist, u_s], axis=1)[:, -(CONV_K - 1):])
    return x, state


def kernel(x_prompt, x_sample, cache_nsa, cache_diff, state_nsa_win, state_pool, state_conv, page_table,
           att_norm_g, att_w_in, att_w_out, nsa_q_gain, nsa_k_gain, nsa_cmp_w, diff_q_gain, diff_k_gain,
           diff_lambda, diff_out_gain, pc_norm_g, pc_w_in, pc_w_out, pool_w, pool_scale, conv_w,
           ffn_norm_g, peer_wq, peer_keys, peer_u, peer_v):
    batch, seq, d = x_prompt.shape
    nb, ts, _ = x_sample.shape
    n_prompt = batch * seq
    n = n_prompt + nb * ts
    depth = ffn_norm_g.shape[0]
    past = page_table.shape[1] * cache_nsa.shape[2]
    tm = n // 8
    assert n % 8 == 0 and tm % 16 == 0
    x = jnp.concatenate([x_prompt.reshape(n_prompt, d), x_sample.reshape(nb * ts, d)], axis=0)
    att_states, pc_states = [], []
    for li in range(depth):
        i = li // 2
        if li % 2 == 0:
            x, st = _attention_layer(
                x, li, n_prompt, batch, seq, nb, ts, cache_nsa[i], cache_diff[i], state_nsa_win[i], page_table,
                att_norm_g[i], att_w_in[i], att_w_out[i], nsa_q_gain[i], nsa_k_gain[i], nsa_cmp_w[i],
                diff_q_gain[i], diff_k_gain[i], diff_lambda[i], diff_out_gain[i], tm=tm)
            att_states.append(st)
        else:
            x, st = _pool_conv_layer(
                x, n_prompt, batch, seq, nb, ts, past, state_pool[i], state_conv[i],
                pc_norm_g[i], pc_w_in[i], pc_w_out[i], pool_w[i], pool_scale[i], conv_w[i], tm=tm)
            pc_states.append(st)
        x = _peer_ffn(x, ffn_norm_g[li], peer_wq[li], peer_keys[li], peer_u[li], peer_v[li], tm=tm)
    stack = lambda states, k: jnp.stack([s[k] for s in states])
    return (x[:n_prompt].reshape(batch, seq, d), x[n_prompt:].reshape(nb, ts, d),
            stack(att_states, 0), stack(att_states, 1), stack(att_states, 2), stack(att_states, 3),
            stack(att_states, 4), stack(att_states, 5),
            stack(pc_states, 0), stack(pc_states, 1), stack(pc_states, 2), stack(pc_states, 3))
```

```python
import functools
import math

import jax
import jax.numpy as jnp
from jax import lax
from jax.experimental import pallas as pl
from jax.experimental.pallas import tpu as pltpu

F32 = jnp.float32
BF = jnp.bfloat16
I32 = jnp.int32

HEAD_DIM = 128
NSA_HEADS = 8
NSA_KV_HEADS = 2
NSA_HPG = NSA_HEADS // NSA_KV_HEADS
L_CMP = 32
L_SEL = 64
N_SEL = 16
WINDOW = 512
FORCE_BONUS = 1e4
DIFF_HEADS = 8
DIFF_DK = HEAD_DIM // 2
POOL_WINDOWS = (2, 4, 8, 16)
POOL_HIST = max(POOL_WINDOWS) - 1
CONV_K = 3
PEER_HEADS = 8
PEER_NKEYS = 128
PEER_TOPK = 16
EPS = 1e-6
NEG = -1e30
LANES = 128
HALO = 16
VMEM_LIMIT = 56 * 1024 * 1024

_NT = (((1,), (1,)), ((), ()))


def _params(*sem):
    return pltpu.CompilerParams(dimension_semantics=sem, vmem_limit_bytes=VMEM_LIMIT)


def _dot_nt(a, b):
    return lax.dot_general(a, b, _NT, preferred_element_type=F32)


def _dot(a, b):
    return jnp.dot(a, b, preferred_element_type=F32)


def _pow2_neg(k):
    return lax.bitcast_convert_type((127 - k) << 23, F32)


def _rms(x, g):
    return x * lax.rsqrt(jnp.mean(x * x, axis=-1, keepdims=True) + EPS) * g


def _rms_halves(x, g):
    lo = lax.broadcasted_iota(I32, x.shape, 1) < DIFF_DK
    x2 = x * x
    s_lo = jnp.sum(jnp.where(lo, x2, 0.0), axis=-1, keepdims=True)
    s_hi = jnp.sum(jnp.where(lo, 0.0, x2), axis=-1, keepdims=True)
    ms = jnp.where(lo, s_lo, s_hi) * (1.0 / DIFF_DK)
    return x * lax.rsqrt(ms + EPS) * g


def _msoftmax(s, mask):
    sm = jnp.where(mask, s, NEG)
    m = jnp.max(sm, axis=-1, keepdims=True)
    e = jnp.where(mask, jnp.exp(sm - m), 0.0)
    d = jnp.sum(e, axis=-1, keepdims=True)
    return e / jnp.where(d > 0.0, d, 1.0)


def _msoftmax2(s1, m1, s2, m2):
    a1 = jnp.where(m1, s1, NEG)
    a2 = jnp.where(m2, s2, NEG)
    m = jnp.maximum(jnp.max(a1, axis=-1, keepdims=True), jnp.max(a2, axis=-1, keepdims=True))
    e1 = jnp.where(m1, jnp.exp(a1 - m), 0.0)
    e2 = jnp.where(m2, jnp.exp(a2 - m), 0.0)
    d = jnp.sum(e1, axis=-1, keepdims=True) + jnp.sum(e2, axis=-1, keepdims=True)
    inv = 1.0 / jnp.where(d > 0.0, d, 1.0)
    return e1 * inv, e2 * inv


def _mm_body(*refs, norm, has_res):
    if norm:
        x_ref, g_ref, w_ref = refs[:3]
        rest = refs[3:]
    else:
        x_ref, w_ref = refs[:2]
        rest = refs[2:]
    if has_res:
        r_ref, o_ref, xb_ref = rest
    else:
        o_ref, xb_ref = rest

    @pl.when(pl.program_id(1) == 0)
    def _():
        x = x_ref[...]
        if norm:
            x = _rms(x.astype(F32), g_ref[...])
        xb_ref[...] = x.astype(BF)

    y = _dot(xb_ref[...], w_ref[...].astype(BF))
    if has_res:
        y = y + r_ref[...]
    o_ref[...] = y.astype(o_ref.dtype)


def _mm(x, w, *, gain=None, res=None, tm, tn, name):
    m, k = x.shape
    n = w.shape[1]
    assert m % tm == 0 and n % tn == 0
    norm = gain is not None
    in_specs = [pl.BlockSpec((tm, k), lambda i, j: (i, 0))]
    args = [x]
    if norm:
        in_specs.append(pl.BlockSpec((1, k), lambda i, j: (0, 0)))
        args.append(gain.reshape(1, k))
    in_specs.append(pl.BlockSpec((k, tn), lambda i, j: (0, j)))
    args.append(w)
    if res is not None:
        in_specs.append(pl.BlockSpec((tm, tn), lambda i, j: (i, j)))
        args.append(res)
    return pl.pallas_call(
        functools.partial(_mm_body, norm=norm, has_res=res is not None),
        grid=(m // tm, n // tn),
        in_specs=in_specs,
        out_specs=pl.BlockSpec((tm, tn), lambda i, j: (i, j)),
        out_shape=jax.ShapeDtypeStruct((m, n), F32),
        scratch_shapes=[pltpu.VMEM((tm, k), BF)],
        compiler_params=_params("parallel", "arbitrary"),
        name=name,
    )(*args)


_Q0 = 0
_KV0 = NSA_HEADS * HEAD_DIM
_DQ0 = _KV0 + 6 * NSA_KV_HEADS * HEAD_DIM
_DK0 = _DQ0 + DIFF_HEADS * HEAD_DIM
_DV0 = _DK0 + DIFF_HEADS * HEAD_DIM
_GT0 = _DV0 + DIFF_HEADS * HEAD_DIM
_ZW = _GT0 + NSA_KV_HEADS * LANES


def _att_post_body(z_ref, qg_ref, kg_ref, dqg_ref, dkg_ref,
                   qn_ref, rows_ref, win_ref, gate_ref, dqn_ref, drows_ref):
    hd = HEAD_DIM
    qg = qg_ref[...]
    for h in range(NSA_HEADS):
        x = z_ref[:, _Q0 + h * hd:_Q0 + (h + 1) * hd]
        qn_ref[:, h * hd:(h + 1) * hd] = (_rms(x, qg) * (HEAD_DIM ** -0.5)).astype(qn_ref.dtype)
    gw = NSA_KV_HEADS * hd
    for br in range(3):
        for g in range(NSA_KV_HEADS):
            ko = _KV0 + br * 2 * gw + g * hd
            k = _rms(z_ref[:, ko:ko + hd], kg_ref[br:br + 1, :])
            v = z_ref[:, ko + gw:ko + gw + hd]
            if br < 2:
                rows_ref[:, br * 2 * gw + g * hd:br * 2 * gw + (g + 1) * hd] = k
                rows_ref[:, br * 2 * gw + gw + g * hd:br * 2 * gw + gw + (g + 1) * hd] = v
            else:
                win_ref[:, g * hd:(g + 1) * hd] = k
                win_ref[:, gw + g * hd:gw + (g + 1) * hd] = v
    gate_ref[...] = jax.nn.sigmoid(z_ref[:, _GT0:_ZW])
    dqg = dqg_ref[...]
    dkg = dkg_ref[...]
    for h in range(DIFF_HEADS):
        x = z_ref[:, _DQ0 + h * hd:_DQ0 + (h + 1) * hd]
        dqn_ref[:, h * hd:(h + 1) * hd] = (_rms_halves(x, dqg) * (DIFF_DK ** -0.5)).astype(dqn_ref.dtype)
        x = z_ref[:, _DK0 + h * hd:_DK0 + (h + 1) * hd]
        drows_ref[:, h * hd:(h + 1) * hd] = _rms_halves(x, dkg)
    dvw = DIFF_HEADS * hd
    drows_ref[:, dvw:2 * dvw] = z_ref[:, _DV0:_DV0 + dvw]


def _att_post(z, q_gain, k_gain, dq_gain, dk_gain, *, tm):
    n = z.shape[0]
    assert n % tm == 0 and z.shape[1] == _ZW
    hd = HEAD_DIM
    widths = (NSA_HEADS * hd, 4 * NSA_KV_HEADS * hd, 2 * NSA_KV_HEADS * hd, NSA_KV_HEADS * LANES,
              DIFF_HEADS * hd, 2 * DIFF_HEADS * hd)
    dtypes = (BF, F32, F32, F32, BF, F32)
    small = lambda r: pl.BlockSpec((r, hd), lambda i: (0, 0))
    return pl.pallas_call(
        _att_post_body,
        grid=(n // tm,),
        in_specs=[pl.BlockSpec((tm, _ZW), lambda i: (i, 0)), small(1), small(3), small(1), small(1)],
        out_specs=[pl.BlockSpec((tm, w), lambda i: (i, 0)) for w in widths],
        out_shape=[jax.ShapeDtypeStruct((n, w), d) for w, d in zip(widths, dtypes)],
        compiler_params=_params("parallel"),
        name="att_post",
    )(z, q_gain.reshape(1, hd), k_gain, dq_gain.reshape(1, hd), dk_gain.reshape(1, hd))


def _compress(x, w):
    length, c = x.shape
    npair = length // (2 * L_CMP)
    x3 = x.reshape(npair, 2 * L_CMP, c)
    even = jnp.sum(x3[:, :L_CMP, :] * w, axis=1)
    odd = jnp.sum(x3[:, L_CMP:, :] * w, axis=1)
    pad = jnp.zeros((LANES - 2 * npair, c), F32)
    return jnp.concatenate([even, odd, pad], axis=0)


def _nsa_cmp_body(x_ref, w_ref, o_ref):
    o_ref[...] = _compress(x_ref[...], w_ref[...])


def _nsa_cmp(rows, cmp_w, *, batch, seq):
    gw = NSA_KV_HEADS * HEAD_DIM
    assert seq % (2 * L_CMP) == 0 and seq // L_CMP <= LANES
    w = jnp.concatenate([jnp.tile(cmp_w[0], (1, NSA_KV_HEADS)), jnp.tile(cmp_w[1], (1, NSA_KV_HEADS))], axis=1)
    return pl.pallas_call(
        _nsa_cmp_body,
        grid=(batch,),
        in_specs=[pl.BlockSpec((seq, 2 * gw), lambda b: (b, 0)),
                  pl.BlockSpec((L_CMP, 2 * gw), lambda b: (0, 0))],
        out_specs=pl.BlockSpec((None, LANES, 2 * gw), lambda b: (b, 0, 0)),
        out_shape=jax.ShapeDtypeStruct((batch, LANES, 2 * gw), F32),
        compiler_params=_params("parallel"),
        name="nsa_cmp",
    )(rows, w)


def _cmp_positions(ncmp):
    c = lax.broadcasted_iota(I32, (1, LANES), 1)
    half = ncmp // 2
    blk = jnp.where(c < half, 2 * c, 2 * (c - half) + 1)
    return c, blk * L_CMP + (L_CMP - 1), c < ncmp


def _select_blocks(p_heads, t1, ncmp, nsel_blocks):
    c = lax.broadcasted_iota(I32, (1, LANES), 1)
    half = ncmp // 2
    imp = p_heads[0]
    for p in p_heads[1:]:
        imp = imp + p
    imp = jnp.where(c < half, imp + pltpu.roll(imp, LANES - half, 1), 0.0)
    cur = t1 // L_SEL
    forced = (c == 0) | (c == cur) | (c == cur - 1)
    valid = (c * L_SEL <= t1) & (c < nsel_blocks)
    score = jnp.where(valid, imp + jnp.where(forced, FORCE_BONUS, 0.0), NEG)
    rank = jnp.zeros(score.shape, I32)
    for jp in range(nsel_blocks):
        sj = score[:, jp:jp + 1]
        beats = (sj > score) | ((sj == score) & (c > jp))
        rank = rank + beats.astype(I32)
    return (rank < min(N_SEL, nsel_blocks)) & (c < nsel_blocks)


def _expand_blocks(sel01, length):
    j = lax.broadcasted_iota(I32, (LANES, length), 0)
    l = lax.broadcasted_iota(I32, (LANES, length), 1)
    e = jnp.where(l // L_SEL == j, 1.0, 0.0).astype(BF)
    return _dot(sel01.astype(BF), e)


CAUSAL_STEP = 512


def _causal_variants(need, seq, fn, out_ref):
    step = min(CAUSAL_STEP, seq)
    assert seq % step == 0
    for klen in range(step, seq + 1, step):
        @pl.when((need <= klen) & (need > klen - step))
        def _(klen=klen):
            out_ref[...] = fn(klen)


def _nsa_prompt_body(q_ref, gate_ref, kc_ref, vc_ref, ks_ref, vs_ref, kw_ref, vw_ref, o_ref, os_ref, *, tq, seq):
    g = pl.program_id(1)
    t0 = pl.program_id(2) * tq
    hg = NSA_HPG
    hd = HEAD_DIM
    rows = hg * tq
    q = q_ref[...]
    qs = jnp.concatenate([q[:, h * hd:(h + 1) * hd] for h in range(hg)], axis=0)
    r_io = lax.broadcasted_iota(I32, (rows, 1), 0)
    hrow = r_io // tq
    tpos = t0 + (r_io - hrow * tq)
    slope = _pow2_neg(g * hg + hrow + 1)
    t1 = t0 + lax.broadcasted_iota(I32, (tq, 1), 0)
    ncmp = seq // L_CMP
    nblk = -(-seq // L_SEL)

    _, pc, cvalid = _cmp_positions(ncmp)
    s = _dot_nt(qs, kc_ref[...].astype(BF)) - slope * (tpos - pc).astype(F32)
    p = _msoftmax(s, cvalid & (pc <= tpos))
    o_c = _dot(p.astype(BF), vc_ref[...].astype(BF))
    sel = _select_blocks([p[h * tq:(h + 1) * tq] for h in range(hg)], t1, ncmp, nblk)
    sel01 = jnp.where(sel, 1.0, 0.0)

    def selected(klen):
        selm = jnp.concatenate([_expand_blocks(sel01, klen)] * hg, axis=0)
        dist = tpos - lax.broadcasted_iota(I32, (1, klen), 1)
        s = _dot_nt(qs, ks_ref[0:klen, :].astype(BF)) - slope * dist.astype(F32)
        p = _msoftmax(s, (dist >= 0) & (selm > 0.5))
        return _dot(p.astype(BF), vs_ref[0:klen, :].astype(BF))

    _causal_variants(t0 + tq, seq, selected, os_ref)
    o_s = os_ref[...]

    wl = WINDOW + tq
    start = pl.multiple_of(jnp.maximum(t0 - WINDOW, 0), tq)
    posw = start + lax.broadcasted_iota(I32, (1, wl), 1)
    dist = tpos - posw
    s = _dot_nt(qs, kw_ref[pl.ds(start, wl), :].astype(BF)) - slope * dist.astype(F32)
    p = _msoftmax(s, (dist >= 0) & (dist < WINDOW))
    o_w = _dot(p.astype(BF), vw_ref[pl.ds(start, wl), :].astype(BF))

    gt = gate_ref[...]
    for h in range(hg):
        rs = slice(h * tq, (h + 1) * tq)
        o = gt[:, 3 * h:3 * h + 1] * o_c[rs] + gt[:, 3 * h + 1:3 * h + 2] * o_s[rs] + gt[:, 3 * h + 2:3 * h + 3] * o_w[rs]
        o_ref[:, h * hd:(h + 1) * hd] = o.astype(o_ref.dtype)


def _nsa_prompt(qn, gates, cmp, rows, win, *, batch, seq, tq):
    hd = HEAD_DIM
    ng = NSA_KV_HEADS
    nq = seq // tq
    assert seq % tq == 0 and tq % LANES == 0 and WINDOW % tq == 0 and seq >= WINDOW + tq
    gq = NSA_HPG * hd
    seq_blk = lambda col: pl.BlockSpec((seq, hd), lambda b, g, i, col=col: (b, col + g))
    return pl.pallas_call(
        functools.partial(_nsa_prompt_body, tq=tq, seq=seq),
        grid=(batch, ng, nq),
        in_specs=[
            pl.BlockSpec((tq, gq), lambda b, g, i: (b * nq + i, g)),
            pl.BlockSpec((tq, LANES), lambda b, g, i: (b * nq + i, g)),
            pl.BlockSpec((None, LANES, hd), lambda b, g, i: (b, 0, g)),
            pl.BlockSpec((None, LANES, hd), lambda b, g, i: (b, 0, ng + g)),
            seq_blk(2 * ng), seq_blk(3 * ng),
            seq_blk(0), seq_blk(ng),
        ],
        out_specs=pl.BlockSpec((tq, gq), lambda b, g, i: (b * nq + i, g)),
        out_shape=jax.ShapeDtypeStruct((batch * seq, ng * gq), BF),
        scratch_shapes=[pltpu.VMEM((NSA_HPG * tq, hd), F32)],
        compiler_params=_params("parallel", "parallel", "arbitrary"),
        name="nsa_prompt",
    )(qn, gates, cmp, cmp, rows, rows, win, win)


def _lambda(lv, lam_init):
    a = jnp.sum(lv[0:1, :] * lv[1:2, :], axis=-1, keepdims=True)
    b = jnp.sum(lv[2:3, :] * lv[3:4, :], axis=-1, keepdims=True)
    return jnp.exp(a) - jnp.exp(b) + lam_init


def _diff_prompt_body(lv_ref, og_ref, q_ref, k_ref, v_ref, o_ref, acc_ref, *, tq, seq, lam_init):
    h = pl.program_id(1)
    t0 = pl.program_id(2) * tq
    q = q_ref[...].astype(F32)
    lo = lax.broadcasted_iota(I32, q.shape, 1) < DIFF_DK
    qs = jnp.concatenate([jnp.where(lo, q, 0.0), jnp.where(lo, 0.0, q)], axis=0).astype(BF)
    r_io = lax.broadcasted_iota(I32, (2 * tq, 1), 0)
    tpos = t0 + jnp.where(r_io >= tq, r_io - tq, r_io)
    slope = _pow2_neg(jnp.full((1, 1), h + 1, I32))
    lam = _lambda(lv_ref[...], lam_init)

    def attend(klen):
        dist = tpos - lax.broadcasted_iota(I32, (1, klen), 1)
        s = _dot_nt(qs, k_ref[0:klen, :].astype(BF)) - slope * dist.astype(F32)
        p = _msoftmax(s, dist >= 0)
        a = p[:tq] - lam * p[tq:]
        return _dot(a.astype(BF), v_ref[0:klen, :].astype(BF))

    _causal_variants(t0 + tq, seq, attend, acc_ref)
    o_ref[...] = (_rms(acc_ref[...], og_ref[...]) * (1.0 - lam_init)).astype(o_ref.dtype)


def _diff_prompt(dqn, drows, lam_vec, out_gain, *, batch, seq, tq, lam_init):
    hd = HEAD_DIM
    nh = DIFF_HEADS
    nq = seq // tq
    assert seq % tq == 0
    return pl.pallas_call(
        functools.partial(_diff_prompt_body, tq=tq, seq=seq, lam_init=lam_init),
        grid=(batch, nh, nq),
        in_specs=[
            pl.BlockSpec((4, DIFF_DK), lambda b, h, i: (0, 0)),
            pl.BlockSpec((1, hd), lambda b, h, i: (0, 0)),
            pl.BlockSpec((tq, hd), lambda b, h, i: (b * nq + i, h)),
            pl.BlockSpec((seq, hd), lambda b, h, i: (b, h)),
            pl.BlockSpec((seq, hd), lambda b, h, i: (b, nh + h)),
        ],
        out_specs=pl.BlockSpec((tq, hd), lambda b, h, i: (b * nq + i, h)),
        out_shape=jax.ShapeDtypeStruct((batch * seq, nh * hd), BF),
        scratch_shapes=[pltpu.VMEM((tq, hd), F32)],
        compiler_params=_params("parallel", "parallel", "arbitrary"),
        name="diff_prompt",
    )(lam_vec, out_gain.reshape(1, hd), dqn, drows, drows)


def _page_copies(pt_ref, cache_ref, buf_ref, sem_ref, b, slot, start):
    npages = pt_ref.shape[1]
    page, nkind, nhead = cache_ref.shape[1:4]

    def one_page(p, c):
        phys = pt_ref[b, p]
        dst_rows = pl.ds(pl.multiple_of(p * page, page), page)
        for kind in range(nkind):
            for h in range(nhead):
                cp = pltpu.make_async_copy(cache_ref.at[phys, :, kind, h, :],
                                           buf_ref.at[slot, kind, h, dst_rows, :], sem_ref.at[slot])
                if start:
                    cp.start()
                else:
                    cp.wait()
        return c

    lax.fori_loop(0, npages, one_page, 0)


def _state_copies(state_ref, buf_ref, sem_ref, b, slot, start):
    nkind, nhead = state_ref.shape[2:4]
    for kind in range(nkind):
        for h in range(nhead):
            cp = pltpu.make_async_copy(state_ref.at[b, :, kind, h, :], buf_ref.at[slot, kind, h], sem_ref.at[slot])
            if start:
                cp.start()
            else:
                cp.wait()


def _double_buffered(fetch):
    b = pl.program_id(0)
    slot = b % 2

    @pl.when(b == 0)
    def _():
        fetch(b, slot, True)

    @pl.when(b + 1 < pl.num_programs(0))
    def _():
        fetch(b + 1, 1 - slot, True)

    fetch(b, slot, False)
    return slot


def _nsa_sample_body(pt_ref, q_ref, gate_ref, nrow_ref, nwin_ref, cw_ref, cache_ref, wstate_ref, o_ref,
                     kv_buf, w_buf, kv_sem, w_sem, *, past, ts):
    def fetch(b, slot, start):
        _page_copies(pt_ref, cache_ref, kv_buf, kv_sem, b, slot, start)
        _state_copies(wstate_ref, w_buf, w_sem, b, slot, start)

    slot = _double_buffered(fetch)
    past_ref = kv_buf.at[slot]
    wst_ref = w_buf.at[slot]
    hg = NSA_HPG
    hd = HEAD_DIM
    ng = NSA_KV_HEADS
    rows = q_ref.shape[1]
    tsp = rows // hg
    npad = nrow_ref.shape[0]
    gw = ng * hd
    ncmp = (past + ts) // L_CMP
    nblk = -(-(past + ts) // L_SEL)
    new_blk = past // L_SEL
    r_io = lax.broadcasted_iota(I32, (rows, 1), 0)
    hrow = r_io // tsp
    trow = r_io - hrow * tsp
    tpos = past + trow
    t1 = past + lax.broadcasted_iota(I32, (tsp, 1), 0)
    jn = lax.broadcasted_iota(I32, (1, npad), 1)
    new_ok = (jn <= trow) & (jn < ts)
    dist_new = (trow - jn).astype(F32)
    _, pc, cvalid = _cmp_positions(ncmp)
    pos = lax.broadcasted_iota(I32, (1, past), 1)
    nwst = wst_ref.shape[2]
    posw = past - nwst + lax.broadcasted_iota(I32, (1, nwst), 1)
    for g in range(ng):
        qs = q_ref[g]
        slope = _pow2_neg(g * hg + hrow + 1)
        kc = _compress(past_ref[0, g], cw_ref[0]).astype(BF)
        vc = _compress(past_ref[1, g], cw_ref[1]).astype(BF)
        s = _dot_nt(qs, kc) - slope * (tpos - pc).astype(F32)
        p = _msoftmax(s, cvalid & (pc <= tpos))
        o_c = _dot(p.astype(BF), vc)
        sel = _select_blocks([p[h * tsp:(h + 1) * tsp] for h in range(hg)], t1, ncmp, nblk)
        sel01 = jnp.concatenate([jnp.where(sel, 1.0, 0.0)] * hg, axis=0)
        selm = _expand_blocks(sel01, past)
        sel_new = sel01[:, new_blk:new_blk + 1] > 0.5
        kn = nrow_ref[:, 2 * gw + g * hd:2 * gw + (g + 1) * hd].astype(BF)
        vn = nrow_ref[:, 3 * gw + g * hd:3 * gw + (g + 1) * hd].astype(BF)
        s1 = _dot_nt(qs, past_ref[2, g].astype(BF)) - slope * (tpos - pos).astype(F32)
        s2 = _dot_nt(qs, kn) - slope * dist_new
        p1, p2 = _msoftmax2(s1, selm > 0.5, s2, new_ok & sel_new)
        o_s = _dot(p1.astype(BF), past_ref[3, g].astype(BF)) + _dot(p2.astype(BF), vn)
        kn = nwin_ref[:, g * hd:(g + 1) * hd].astype(BF)
        vn = nwin_ref[:, gw + g * hd:gw + (g + 1) * hd].astype(BF)
        dist = tpos - posw
        s1 = _dot_nt(qs, wst_ref[0, g].astype(BF)) - slope * dist.astype(F32)
        s2 = _dot_nt(qs, kn) - slope * dist_new
        p1, p2 = _msoftmax2(s1, (dist >= 0) & (dist < WINDOW), s2, new_ok)
        o_w = _dot(p1.astype(BF), wst_ref[1, g].astype(BF)) + _dot(p2.astype(BF), vn)
        gt = gate_ref[g]
        o_ref[g] = gt[:, 0:1] * o_c + gt[:, 1:2] * o_s + gt[:, 2:3] * o_w


def _nsa_sample(q, gates, cache, page_table, win_state, new_rows, new_win, cmp_w, *, ts):
    nb, ng, rows, hd = q.shape
    past = page_table.shape[1] * cache.shape[1]
    nwst = win_state.shape[1]
    npad = new_rows.shape[1]
    assert past % (2 * L_CMP) == 0 and past % L_SEL == 0 and nwst == WINDOW and (past + ts) // L_CMP == past // L_CMP
    assert cache.shape[2:] == (4, ng, hd) and win_state.shape[2:] == (2, ng, hd)
    return pl.pallas_call(
        functools.partial(_nsa_sample_body, past=past, ts=ts),
        grid_spec=pltpu.PrefetchScalarGridSpec(
            num_scalar_prefetch=1,
            grid=(nb,),
            in_specs=[
                pl.BlockSpec((None, ng, rows, hd), lambda b, pt: (b, 0, 0, 0)),
                pl.BlockSpec((None, ng, rows, LANES), lambda b, pt: (b, 0, 0, 0)),
                pl.BlockSpec((None, npad, new_rows.shape[2]), lambda b, pt: (b, 0, 0)),
                pl.BlockSpec((None, npad, new_win.shape[2]), lambda b, pt: (b, 0, 0)),
                pl.BlockSpec((2, L_CMP, hd), lambda b, pt: (0, 0, 0)),
                pl.BlockSpec(memory_space=pl.ANY),
                pl.BlockSpec(memory_space=pl.ANY),
            ],
            out_specs=pl.BlockSpec((None, ng, rows, hd), lambda b, pt: (b, 0, 0, 0)),
            scratch_shapes=[pltpu.VMEM((2, 4, ng, past, hd), F32), pltpu.VMEM((2, 2, ng, nwst, hd), F32),
                            pltpu.SemaphoreType.DMA((2,)), pltpu.SemaphoreType.DMA((2,))],
        ),
        out_shape=jax.ShapeDtypeStruct((nb, ng, rows, hd), F32),
        compiler_params=_params("arbitrary"),
        name="nsa_sample",
    )(page_table, q, gates, new_rows, new_win, cmp_w, cache, win_state)


def _diff_sample_body(pt_ref, lv_ref, og_ref, q_ref, ndrow_ref, cache_ref, o_ref, kv_buf, kv_sem, *, past, ts, lam_init):
    slot = _double_buffered(functools.partial(_page_copies, pt_ref, cache_ref, kv_buf, kv_sem))
    past_ref = kv_buf.at[slot]
    hd = HEAD_DIM
    nh = DIFF_HEADS
    npad = ndrow_ref.shape[0]
    rows = q_ref.shape[1]
    tsp = rows // 2
    r_io = lax.broadcasted_iota(I32, (rows, 1), 0)
    trow = jnp.where(r_io >= tsp, r_io - tsp, r_io)
    tpos = past + trow
    dist = (tpos - lax.broadcasted_iota(I32, (1, past), 1)).astype(F32)
    jn = lax.broadcasted_iota(I32, (1, npad), 1)
    new_ok = (jn <= trow) & (jn < ts)
    dist_new = (trow - jn).astype(F32)
    lam = _lambda(lv_ref[...], lam_init)
    og = og_ref[...]
    for h in range(nh):
        slope = 2.0 ** -(h + 1)
        qs = q_ref[h]
        kn = ndrow_ref[:, h * hd:(h + 1) * hd].astype(BF)
        vn = ndrow_ref[:, (nh + h) * hd:(nh + h + 1) * hd].astype(BF)
        s1 = _dot_nt(qs, past_ref[0, h].astype(BF)) - slope * dist
        s2 = _dot_nt(qs, kn) - slope * dist_new
        p1, p2 = _msoftmax2(s1, dist >= 0.0, s2, new_ok)
        a1 = p1 - lam * jnp.concatenate([p1[tsp:], p1[:tsp]], axis=0)
        a2 = p2 - lam * jnp.concatenate([p2[tsp:], p2[:tsp]], axis=0)
        o = _dot(a1.astype(BF), past_ref[1, h].astype(BF)) + _dot(a2.astype(BF), vn)
        o_ref[h] = _rms(o, og) * (1.0 - lam_init)


def _diff_sample(q, cache, page_table, new_drows, lam_vec, out_gain, *, ts, lam_init):
    nb, nh, rows, hd = q.shape
    past = page_table.shape[1] * cache.shape[1]
    npad = new_drows.shape[1]
    assert cache.shape[2:] == (2, nh, hd)
    return pl.pallas_call(
        functools.partial(_diff_sample_body, past=past, ts=ts, lam_init=lam_init),
        grid_spec=pltpu.PrefetchScalarGridSpec(
            num_scalar_prefetch=1,
            grid=(nb,),
            in_specs=[
                pl.BlockSpec((4, DIFF_DK), lambda b, pt: (0, 0)),
                pl.BlockSpec((1, hd), lambda b, pt: (0, 0)),
                pl.BlockSpec((None, nh, rows, hd), lambda b, pt: (b, 0, 0, 0)),
                pl.BlockSpec((None, npad, new_drows.shape[2]), lambda b, pt: (b, 0, 0)),
                pl.BlockSpec(memory_space=pl.ANY),
            ],
            out_specs=pl.BlockSpec((None, nh, rows, hd), lambda b, pt: (b, 0, 0, 0)),
            scratch_shapes=[pltpu.VMEM((2, 2, nh, past, hd), F32), pltpu.SemaphoreType.DMA((2,))],
        ),
        out_shape=jax.ShapeDtypeStruct((nb, nh, rows, hd), F32),
        compiler_params=_params("arbitrary"),
        name="diff_sample",
    )(page_table, lam_vec, out_gain.reshape(1, hd), q, new_drows, cache)


def _rows(a, lo, n):
    return lax.slice_in_dim(a, lo, lo + n, axis=a.ndim - 2)


def _pool_conv(p_ext, u_ext, b_gate, cnt_pos, pool_w_ref, pool_scale, conv_w, ts):
    gwid = p_ext.shape[-1] // len(POOL_WINDOWS)
    lead = p_ext.shape[:-2]
    outs = []
    for gi, w in enumerate(POOL_WINDOWS):
        x = p_ext[..., gi * gwid:(gi + 1) * gwid]
        acc = x
        span = 1
        while span < w:
            n = acc.shape[-2]
            acc = _rows(acc, span, n - span) + _rows(acc, 0, n - span)
            span *= 2
        win_sum = _rows(acc, HALO - (w - 1), ts)
        cnt = jnp.minimum(w, cnt_pos + 1).astype(F32)
        m = win_sum / cnt - _rows(x, HALO, ts)
        m2 = m.reshape((-1, gwid)).astype(BF)
        outs.append(_dot(m2, pool_w_ref[gi].astype(BF)).reshape(lead + (ts, gwid)))
    y_pool = jnp.concatenate(outs, axis=-1) * pool_scale
    conv = None
    for j in range(CONV_K):
        term = _rows(u_ext, HALO - (CONV_K - 1) + j, ts) * conv_w[j:j + 1, :]
        conv = term if conv is None else conv + term
    return jnp.concatenate([y_pool, b_gate * conv], axis=-1).astype(BF)


def _pc_prompt_body(zc_ref, zh_ref, pw_ref, ps_ref, cw_ref, o_ref, *, ts, cw):
    i = pl.program_id(1)
    keep = i > 0
    p_ext = jnp.concatenate([jnp.where(keep, zh_ref[:, 0:cw], 0.0), zc_ref[:, 0:cw]], axis=0)
    u_h = jnp.where(keep, zh_ref[:, 2 * cw:3 * cw] * zh_ref[:, 3 * cw:4 * cw], 0.0)
    u_ext = jnp.concatenate([u_h, zc_ref[:, 2 * cw:3 * cw] * zc_ref[:, 3 * cw:4 * cw]], axis=0)
    cnt_pos = i * ts + lax.broadcasted_iota(I32, (ts, 1), 0)
    o_ref[...] = _pool_conv(p_ext, u_ext, zc_ref[:, cw:2 * cw], cnt_pos, pw_ref, ps_ref[...], cw_ref[...], ts)


def _pc_prompt(z, pool_w, pool_scale, conv_w, *, batch, seq, ts):
    cw = pool_scale.shape[0]
    nt = seq // ts
    assert seq % ts == 0 and ts % HALO == 0 and z.shape[1] == 4 * cw
    hb = ts // HALO
    return pl.pallas_call(
        functools.partial(_pc_prompt_body, ts=ts, cw=cw),
        grid=(batch, nt),
        in_specs=[
            pl.BlockSpec((ts, 4 * cw), lambda b, i: (b * nt + i, 0)),
            pl.BlockSpec((HALO, 4 * cw), lambda b, i: (jnp.maximum((b * nt + i) * hb - 1, 0), 0)),
            pl.BlockSpec(pool_w.shape, lambda b, i: (0, 0, 0)),
            pl.BlockSpec((1, cw), lambda b, i: (0, 0)),
            pl.BlockSpec((CONV_K, cw), lambda b, i: (0, 0)),
        ],
        out_specs=pl.BlockSpec((ts, 2 * cw), lambda b, i: (b * nt + i, 0)),
        out_shape=jax.ShapeDtypeStruct((batch * seq, 2 * cw), BF),
        compiler_params=_params("parallel", "arbitrary"),
        name="pool_conv_prompt",
    )(z, z, pool_w, pool_scale.reshape(1, cw), conv_w)


def _pc_sample_body(pe_ref, ue_ref, bg_ref, pw_ref, ps_ref, cw_ref, o_ref, *, ts, past):
    cnt_pos = past + lax.broadcasted_iota(I32, (ts, 1), 0)
    o_ref[...] = _pool_conv(pe_ref[...], ue_ref[...], bg_ref[...], cnt_pos, pw_ref, ps_ref[...], cw_ref[...], ts)


def _pc_sample(p_ext, u_ext, b_gate, pool_w, pool_scale, conv_w, *, past, tb):
    nb, ext, cw = p_ext.shape
    ts = ext - HALO
    assert nb % tb == 0
    blk = lambda r: pl.BlockSpec((tb, r, cw), lambda b: (b, 0, 0))
    return pl.pallas_call(
        functools.partial(_pc_sample_body, ts=ts, past=past),
        grid=(nb // tb,),
        in_specs=[blk(ext), blk(ext), blk(ts),
                  pl.BlockSpec(pool_w.shape, lambda b: (0, 0, 0)),
                  pl.BlockSpec((1, cw), lambda b: (0, 0)),
                  pl.BlockSpec((CONV_K, cw), lambda b: (0, 0))],
        out_specs=pl.BlockSpec((tb, ts, 2 * cw), lambda b: (b, 0, 0)),
        out_shape=jax.ShapeDtypeStruct((nb, ts, 2 * cw), BF),
        compiler_params=_params("parallel"),
        name="pool_conv_sample",
    )(p_ext, u_ext, b_gate, pool_w, pool_scale.reshape(1, cw), conv_w)


def _peer_score_body(x_ref, g_ref, wq_ref, keys_ref, s_ref, xn_ref):
    @pl.when(pl.program_id(1) == 0)
    def _():
        xn_ref[...] = _rms(x_ref[...], g_ref[...]).astype(xn_ref.dtype)

    z = _dot(xn_ref[...], wq_ref[...].astype(BF)).astype(BF)
    nk = keys_ref.shape[1]
    dk = keys_ref.shape[2]
    for r in range(keys_ref.shape[0]):
        s_ref[r * nk:(r + 1) * nk, :] = _dot_nt(keys_ref[r].astype(BF), z[:, r * dk:(r + 1) * dk])


def _peer_scores(x, gain, wq, keys, li, *, tm):
    n, d = x.shape
    nsub, nk, dk = keys.shape[1:]
    per = 2
    assert n % tm == 0 and tm % LANES == 0 and nsub % per == 0 and wq.shape[2] == nsub * dk
    return pl.pallas_call(
        _peer_score_body,
        grid=(n // tm, nsub // per),
        in_specs=[
            pl.BlockSpec((tm, d), lambda i, j: (i, 0)),
            pl.BlockSpec((1, d), lambda i, j: (0, 0)),
            pl.BlockSpec((None, d, per * dk), lambda i, j: (li, 0, j)),
            pl.BlockSpec((None, per, nk, dk), lambda i, j: (li, j, 0, 0)),
        ],
        out_specs=[pl.BlockSpec((per * nk, tm), lambda i, j: (j, i)),
                   pl.BlockSpec((tm, d), lambda i, j: (i, 0))],
        out_shape=[jax.ShapeDtypeStruct((nsub * nk, n), F32), jax.ShapeDtypeStruct((n, d), BF)],
        compiler_params=_params("parallel", "arbitrary"),
        name="peer_scores",
    )(x, gain.reshape(1, d), wq, keys)


def _topk_rows(x, k, code):
    slot = lax.broadcasted_iota(I32, (k, x.shape[1]), 0)
    vals = jnp.zeros((k, x.shape[1]), F32)
    idxs = jnp.zeros((k, x.shape[1]), F32)
    for kk in range(k):
        m = jnp.max(x, axis=0, keepdims=True)
        idx = jnp.min(jnp.where(x == m, code, float(2 ** 24)), axis=0, keepdims=True)
        vals = jnp.where(slot == kk, m, vals)
        idxs = jnp.where(slot == kk, idx, idxs)
        x = jnp.where(code == idx, -jnp.inf, x)
    return vals, idxs.astype(I32)


def _pair_candidates(v0, v1):
    k = PEER_TOPK
    assert k == 16 and v0.shape[0] == k
    half = k // 2
    blocks = [v0[0:1] + v1] + [v0[a:a + 1] + v1[0:half] for a in range(1, half)] + [v0[half:] + v1[0:1]]
    r = lax.broadcasted_iota(I32, (k + (half - 1) * half + half, v0.shape[1]), 0)
    mid = r - k
    code = jnp.where(r < k, r,
                     jnp.where(mid < (half - 1) * half, (mid // half + 1) * k + mid % half,
                               (half + mid - (half - 1) * half) * k))
    return jnp.concatenate(blocks, axis=0), code.astype(F32)


def _peer_topk_body(s_ref, i1_ref, i2_ref, g_ref, sv_ref, si_ref, i1t_ref, i2t_ref, gt_ref):
    nsub = sv_ref.shape[0]
    k = PEER_TOPK
    nk = PEER_NKEYS

    def sub_key(gi, c):
        x = s_ref[pl.ds(pl.multiple_of(gi * nk, nk), nk), :]
        v, i = _topk_rows(x, k, lax.broadcasted_iota(I32, x.shape, 0).astype(F32))
        sv_ref[gi] = v
        si_ref[gi] = i
        return c

    lax.fori_loop(0, nsub, sub_key, 0)

    def head(h, c):
        comb, code = _pair_candidates(sv_ref[2 * h], sv_ref[2 * h + 1])
        cv, ci = _topk_rows(comb, k, code)
        a = ci // k
        b = ci - a * k
        s0 = si_ref[2 * h]
        s1 = si_ref[2 * h + 1]
        i1 = jnp.zeros(ci.shape, I32)
        i2 = jnp.zeros(ci.shape, I32)
        for q in range(k):
            i1 = jnp.where(a == q, s0[q:q + 1], i1)
            i2 = jnp.where(b == q, s1[q:q + 1], i2)
        e = jnp.exp(cv - cv[0:1])
        rows = pl.ds(pl.multiple_of(h * k, k), k)
        i1t_ref[rows, :] = i1
        i2t_ref[rows, :] = i2
        gt_ref[rows, :] = e / jnp.sum(e, axis=0, keepdims=True)
        return c

    lax.fori_loop(0, nsub // 2, head, 0)
    i1_ref[...] = i1t_ref[...].T
    i2_ref[...] = i2t_ref[...].T
    g_ref[...] = gt_ref[...].T


def _peer_topk(s_t, *, nh, tl):
    rows, n = s_t.shape
    npair = nh * PEER_TOPK
    assert rows == nh * 2 * PEER_NKEYS and n % tl == 0 and tl % LANES == 0
    out = pl.BlockSpec((tl, npair), lambda i: (i, 0))
    return pl.pallas_call(
        _peer_topk_body,
        grid=(n // tl,),
        in_specs=[pl.BlockSpec((rows, tl), lambda i: (0, i))],
        out_specs=[out, out, out],
        out_shape=[jax.ShapeDtypeStruct((n, npair), I32), jax.ShapeDtypeStruct((n, npair), I32),
                   jax.ShapeDtypeStruct((n, npair), F32)],
        scratch_shapes=[pltpu.VMEM((nh * 2, PEER_TOPK, tl), F32), pltpu.VMEM((nh * 2, PEER_TOPK, tl), I32),
                        pltpu.VMEM((npair, tl), I32), pltpu.VMEM((npair, tl), I32), pltpu.VMEM((npair, tl), F32)],
        compiler_params=_params("parallel"),
        name="peer_topk",
    )(s_t)


W_ROWS = 8


def _peer_w_body(a_ref, b_ref, g_ref, w_ref):
    tt = a_ref.shape[0]
    npair = a_ref.shape[2]
    io = lax.broadcasted_iota(I32, (tt, PEER_NKEYS, npair), 1)
    one_a = jnp.where(a_ref[...] == io, 1.0, 0.0).astype(BF)
    g = g_ref[...]
    g_hi = g.astype(BF).astype(F32)
    g_lo = g - g_hi
    hit = b_ref[...] == io
    dims = (((2,), (2,)), ((0,), (0,)))
    lhs = jnp.concatenate([one_a, one_a], axis=2)
    rhs = jnp.concatenate([jnp.where(hit, g_hi, 0.0).astype(BF), jnp.where(hit, g_lo, 0.0).astype(BF)], axis=2)
    w = lax.dot_general(lhs, rhs, dims, preferred_element_type=F32)
    nchunk = w_ref.shape[0]
    w4 = w.reshape(tt, nchunk, PEER_NKEYS // nchunk, PEER_NKEYS)
    for j in range(nchunk):
        w_ref[j] = w4[:, j]


def _peer_w(i1, i2, gate, *, tt):
    n, npair = i1.shape
    assert n % tt == 0
    blk = pl.BlockSpec((tt, 1, npair), lambda i: (i, 0, 0))
    return pl.pallas_call(
        _peer_w_body,
        grid=(n // tt,),
        in_specs=[blk, blk, blk],
        out_specs=pl.BlockSpec((PEER_NKEYS // W_ROWS, tt, W_ROWS, PEER_NKEYS), lambda i: (0, i, 0, 0)),
        out_shape=jax.ShapeDtypeStruct((PEER_NKEYS // W_ROWS, n, W_ROWS, PEER_NKEYS), F32),
        compiler_params=_params("parallel"),
        name="peer_w",
    )(i1.reshape(n, 1, npair), i2.reshape(n, 1, npair), gate.reshape(n, 1, npair))


def _peer_dense_body(xn_ref, xr_ref, u_ref, v_ref, w_ref, o_ref):
    @pl.when(pl.program_id(1) == 0)
    def _():
        o_ref[...] = xr_ref[...]

    tn = xn_ref.shape[0]
    nk = w_ref.shape[1]
    nw = w_ref.shape[0] // tn
    s = _dot_nt(xn_ref[...], u_ref[...])
    act = 0.5 * s * (1.0 + lax.erf(s * (2.0 ** -0.5)))
    wact = jnp.concatenate(
        [act[:, r * nk:(r + 1) * nk] * w_ref[pl.ds(r, tn, stride=nw), :] for r in range(nw)], axis=1)
    o_ref[...] += _dot(wact.astype(BF), v_ref[...])


def _peer_dense(xn, x_res, u_bf, v_bf, w, li, *, tn, ec):
    n, d = xn.shape
    ne = u_bf.shape[1]
    nchunk, _, nw, nk = w.shape
    assert n % tn == 0 and ne == nchunk * ec and ec == nw * nk
    w = w.reshape(nchunk, n * nw, nk)
    return pl.pallas_call(
        _peer_dense_body,
        grid=(n // tn, ne // ec),
        in_specs=[
            pl.BlockSpec((tn, d), lambda i, j: (i, 0)),
            pl.BlockSpec((tn, d), lambda i, j: (i, 0)),
            pl.BlockSpec((None, ec, d), lambda i, j: (li, j, 0)),
            pl.BlockSpec((None, ec, d), lambda i, j: (li, j, 0)),
            pl.BlockSpec((None, tn * nw, nk), lambda i, j: (j, i, 0)),
        ],
        out_specs=pl.BlockSpec((tn, d), lambda i, j: (i, 0)),
        out_shape=jax.ShapeDtypeStruct((n, d), F32),
        compiler_params=_params("parallel", "arbitrary"),
        name="peer_dense",
    )(xn, x_res, u_bf, v_bf, w)


def _peer_ffn(x, gain, wq, keys, u_bf, v_bf, li, *, tm):
    nl, nh = keys.shape[:2]
    s_t, xn = _peer_scores(x, gain, wq, keys.reshape(nl, nh * 2, PEER_NKEYS, keys.shape[-1]), li, tm=512)
    i1, i2, gate = _peer_topk(s_t, nh=nh, tl=256)
    w = _peer_w(i1, i2, gate, tt=32)
    return _peer_dense(xn, x, u_bf, v_bf, w, li, tn=tm // 2, ec=W_ROWS * PEER_NKEYS)


def _reorder_att_w_in(w_in):
    d = w_in.shape[0]
    o2 = _DQ0
    ngate = 3 * NSA_HPG
    gates = [jnp.pad(w_in[:, o2 + g * ngate:o2 + (g + 1) * ngate], ((0, 0), (0, LANES - ngate)))
             for g in range(NSA_KV_HEADS)]
    return jnp.concatenate([w_in[:, :o2], w_in[:, o2 + 3 * NSA_HEADS:]] + gates, axis=1)


def _attention_layer(x, li, n_prompt, batch, seq, nb, ts, cache_nsa_l, cache_diff_l, win_state_l, page_table,
                     norm_g, w_in, w_out, q_gain, k_gain, cmp_w, dq_gain, dk_gain, lam_vec, out_gain, *, tm):
    hd = HEAD_DIM
    ng = NSA_KV_HEADS
    hg = NSA_HPG
    nh = DIFF_HEADS
    lam_init = 0.8 - 0.6 * math.exp(-0.3 * li)

    z = _mm(x, _reorder_att_w_in(w_in), gain=norm_g, tm=tm, tn=256, name="att_in")
    qn, rows, win, gates, dqn, drows = _att_post(z, q_gain, k_gain, dq_gain, dk_gain, tm=272)

    cmp = _nsa_cmp(rows, cmp_w, batch=batch, seq=seq)
    o_nsa_p = _nsa_prompt(qn, gates, cmp, rows, win, batch=batch, seq=seq, tq=128)
    o_diff_p = _diff_prompt(dqn, drows, lam_vec, out_gain, batch=batch, seq=seq, tq=256, lam_init=lam_init)

    tsp = 8
    assert ts <= tsp
    pad_t = lambda a, ax: jnp.pad(a, [(0, tsp - ts) if k == ax else (0, 0) for k in range(a.ndim)])
    pad8 = lambda a: pad_t(a.reshape(nb, ts, a.shape[-1]), 1)
    rows_s, win_s, drows_s = rows[n_prompt:], win[n_prompt:], drows[n_prompt:]
    q_s = pad_t(qn[n_prompt:].reshape(nb, ts, ng, hg, hd).transpose(0, 2, 3, 1, 4), 3).reshape(nb, ng, hg * tsp, hd)
    g_s = gates[n_prompt:].reshape(nb, ts, ng, LANES)[..., :3 * hg].reshape(nb, ts, ng, hg, 3)
    g_s = pad_t(g_s.transpose(0, 2, 3, 1, 4), 3).reshape(nb, ng, hg * tsp, 3)
    g_s = jnp.pad(g_s, ((0, 0), (0, 0), (0, 0), (0, LANES - 3)))
    o_nsa_s = _nsa_sample(q_s, g_s, cache_nsa_l, page_table, win_state_l, pad8(rows_s), pad8(win_s), cmp_w, ts=ts)
    o_nsa_s = o_nsa_s.reshape(nb, ng, hg, tsp, hd)[:, :, :, :ts]
    o_nsa_s = o_nsa_s.transpose(0, 3, 1, 2, 4).reshape(nb * ts, ng * hg * hd)

    dq_s = pad_t(dqn[n_prompt:].reshape(nb, ts, nh, hd).transpose(0, 2, 1, 3), 2)
    lo = jnp.arange(hd) < DIFF_DK
    dq_s = jnp.concatenate([jnp.where(lo, dq_s, 0), jnp.where(lo, 0, dq_s)], axis=2).astype(BF)
    o_diff_s = _diff_sample(dq_s, cache_diff_l, page_table, pad8(drows_s), lam_vec, out_gain, ts=ts, lam_init=lam_init)
    o_diff_s = o_diff_s[:, :, :ts].transpose(0, 2, 1, 3).reshape(nb * ts, nh * hd)

    mix = jnp.concatenate([jnp.concatenate([o_nsa_p, o_diff_p], axis=1),
                           jnp.concatenate([o_nsa_s, o_diff_s], axis=1).astype(BF)], axis=0)
    x = _mm(mix, w_out, res=x, tm=tm, tn=512, name="att_out")

    rows_p = rows[:n_prompt].reshape(batch, seq, 4, ng, hd)
    win_p = win[:n_prompt].reshape(batch, seq, 2, ng, hd)
    keep_p = min(WINDOW, seq)
    win_new_s = win_s.reshape(nb, ts, 2, ng, hd)
    keep_s = win_state_l.shape[1]
    state = (rows_p, rows_s.reshape(nb, ts, 4, ng, hd),
             win_p[:, seq - keep_p:], jnp.concatenate([win_state_l, win_new_s], axis=1)[:, -keep_s:],
             drows[:n_prompt].reshape(batch, seq, 2, nh, hd), drows_s.reshape(nb, ts, 2, nh, hd))
    return x, state


def _pool_conv_layer(x, n_prompt, batch, seq, nb, ts, past, pool_hist, conv_hist,
                     norm_g, w_in, w_out, pool_w, pool_scale, conv_w, *, tm):
    cw = pool_scale.shape[0]
    z = _mm(x, w_in, gain=norm_g, tm=tm, tn=512, name="pc_in")
    mix_p = _pc_prompt(z, pool_w, pool_scale, conv_w, batch=batch, seq=seq, ts=256)

    zs = z[n_prompt:].reshape(nb, ts, 4 * cw)
    p_s = zs[..., :cw]
    u_s = zs[..., 2 * cw:3 * cw] * zs[..., 3 * cw:]
    tpad = 8
    front = lambda hist: jnp.pad(hist, ((0, 0), (HALO - hist.shape[1], 0), (0, 0)))
    back = lambda a: jnp.pad(a, ((0, 0), (0, tpad - ts), (0, 0)))
    p_ext = jnp.concatenate([front(pool_hist), back(p_s)], axis=1)
    u_ext = jnp.concatenate([front(conv_hist), back(u_s)], axis=1)
    mix_s = _pc_sample(p_ext, u_ext, back(zs[..., cw:2 * cw]), pool_w, pool_scale, conv_w, past=past, tb=8)
    mix = jnp.concatenate([mix_p, mix_s[:, :ts].reshape(nb * ts, 2 * cw)], axis=0)
    x = _mm(mix, w_out, res=x, tm=tm, tn=512, name="pc_out")

    tail = jnp.stack([z[(b + 1) * seq - POOL_HIST:(b + 1) * seq] for b in range(batch)])
    ct = tail[:, POOL_HIST - (CONV_K - 1):]
    u_tail = ct[..., 2 * cw:3 * cw] * ct[..., 3 * cw:]
    state = (tail[..., :cw],
             jnp.concatenate([pool_hist, p_s], axis=1)[:, -POOL_HIST:],
             u_tail,
             jnp.concatenate([conv_hist, u_s], axis=1)[:, -(CONV_K - 1):])
    return x, state


def kernel(x_prompt, x_sample, cache_nsa, cache_diff, state_nsa_win, state_pool, state_conv, page_table,
           att_norm_g, att_w_in, att_w_out, nsa_q_gain, nsa_k_gain, nsa_cmp_w, diff_q_gain, diff_k_gain,
           diff_lambda, diff_out_gain, pc_norm_g, pc_w_in, pc_w_out, pool_w, pool_scale, conv_w,
           ffn_norm_g, peer_wq, peer_keys, peer_u, peer_v):
    batch, seq, d = x_prompt.shape
    nb, ts, _ = x_sample.shape
    n_prompt = batch * seq
    n = n_prompt + nb * ts
    depth = ffn_norm_g.shape[0]
    past = page_table.shape[1] * cache_nsa.shape[2]
    tm = n // 8
    assert n % 8 == 0 and tm % 16 == 0
    x = jnp.concatenate([x_prompt.reshape(n_prompt, d), x_sample.reshape(nb * ts, d)], axis=0)
    u_bf, v_bf = peer_u.astype(BF), peer_v.astype(BF)
    att_states, pc_states = [], []
    for li in range(depth):
        i = li // 2
        if li % 2 == 0:
            x, st = _attention_layer(
                x, li, n_prompt, batch, seq, nb, ts, cache_nsa[i], cache_diff[i], state_nsa_win[i], page_table,
                att_norm_g[i], att_w_in[i], att_w_out[i], nsa_q_gain[i], nsa_k_gain[i], nsa_cmp_w[i],
                diff_q_gain[i], diff_k_gain[i], diff_lambda[i], diff_out_gain[i], tm=tm)
            att_states.append(st)
        else:
            x, st = _pool_conv_layer(
                x, n_prompt, batch, seq, nb, ts, past, state_pool[i], state_conv[i],
                pc_norm_g[i], pc_w_in[i], pc_w_out[i], pool_w[i], pool_scale[i], conv_w[i], tm=tm)
            pc_states.append(st)
        x = _peer_ffn(x, ffn_norm_g[li], peer_wq, peer_keys, u_bf, v_bf, li, tm=tm)
    stack = lambda states, k: jnp.stack([s[k] for s in states])
    return (x[:n_prompt].reshape(batch, seq, d), x[n_prompt:].reshape(nb, ts, d),
            stack(att_states, 0), stack(att_states, 1), stack(att_states, 2), stack(att_states, 3),
            stack(att_states, 4), stack(att_states, 5),
            stack(pc_states, 0), stack(pc_states, 1), stack(pc_states, 2), stack(pc_states, 3))
```

```python
import functools
import math

import jax
import jax.numpy as jnp
from jax import lax
from jax.experimental import pallas as pl
from jax.experimental.pallas import tpu as pltpu

F32 = jnp.float32
BF = jnp.bfloat16
I32 = jnp.int32

HEAD_DIM = 128
NSA_HEADS = 8
NSA_KV_HEADS = 2
NSA_HPG = NSA_HEADS // NSA_KV_HEADS
L_CMP = 32
L_SEL = 64
N_SEL = 16
WINDOW = 512
FORCE_BONUS = 1e4
DIFF_HEADS = 8
DIFF_DK = HEAD_DIM // 2
POOL_WINDOWS = (2, 4, 8, 16)
POOL_HIST = max(POOL_WINDOWS) - 1
CONV_K = 3
PEER_HEADS = 8
PEER_NKEYS = 128
PEER_TOPK = 16
EPS = 1e-6
NEG = -1e30
LANES = 128
HALO = 16
VMEM_LIMIT = 56 * 1024 * 1024

_NT = (((1,), (1,)), ((), ()))


def _params(*sem):
    return pltpu.CompilerParams(dimension_semantics=sem, vmem_limit_bytes=VMEM_LIMIT)


def _dot_nt(a, b):
    return lax.dot_general(a, b, _NT, preferred_element_type=F32)


def _dot(a, b):
    return jnp.dot(a, b, preferred_element_type=F32)


def _pow2_neg(k):
    return lax.bitcast_convert_type((127 - k) << 23, F32)


def _rms(x, g):
    return x * lax.rsqrt(jnp.mean(x * x, axis=-1, keepdims=True) + EPS) * g


def _rms_halves(x, g):
    lo = lax.broadcasted_iota(I32, x.shape, 1) < DIFF_DK
    x2 = x * x
    s_lo = jnp.sum(jnp.where(lo, x2, 0.0), axis=-1, keepdims=True)
    s_hi = jnp.sum(jnp.where(lo, 0.0, x2), axis=-1, keepdims=True)
    ms = jnp.where(lo, s_lo, s_hi) * (1.0 / DIFF_DK)
    return x * lax.rsqrt(ms + EPS) * g


def _msoftmax(s, mask):
    sm = jnp.where(mask, s, NEG)
    m = jnp.max(sm, axis=-1, keepdims=True)
    e = jnp.where(mask, jnp.exp(sm - m), 0.0)
    d = jnp.sum(e, axis=-1, keepdims=True)
    return e / jnp.where(d > 0.0, d, 1.0)


def _msoftmax2(s1, m1, s2, m2):
    a1 = jnp.where(m1, s1, NEG)
    a2 = jnp.where(m2, s2, NEG)
    m = jnp.maximum(jnp.max(a1, axis=-1, keepdims=True), jnp.max(a2, axis=-1, keepdims=True))
    e1 = jnp.where(m1, jnp.exp(a1 - m), 0.0)
    e2 = jnp.where(m2, jnp.exp(a2 - m), 0.0)
    d = jnp.sum(e1, axis=-1, keepdims=True) + jnp.sum(e2, axis=-1, keepdims=True)
    inv = 1.0 / jnp.where(d > 0.0, d, 1.0)
    return e1 * inv, e2 * inv


def _mm_body(*refs, norm, has_res):
    if norm:
        x_ref, g_ref, w_ref = refs[:3]
        rest = refs[3:]
    else:
        x_ref, w_ref = refs[:2]
        rest = refs[2:]
    if has_res:
        r_ref, o_ref, xb_ref = rest
    else:
        o_ref, xb_ref = rest

    @pl.when(pl.program_id(1) == 0)
    def _():
        x = x_ref[...]
        if norm:
            x = _rms(x.astype(F32), g_ref[...])
        xb_ref[...] = x.astype(BF)

    y = _dot(xb_ref[...], w_ref[...].astype(BF))
    if has_res:
        y = y + r_ref[...]
    o_ref[...] = y.astype(o_ref.dtype)


def _mm(x, w, *, gain=None, res=None, tm, tn, name):
    m, k = x.shape
    n = w.shape[1]
    assert m % tm == 0 and n % tn == 0
    norm = gain is not None
    in_specs = [pl.BlockSpec((tm, k), lambda i, j: (i, 0))]
    args = [x]
    if norm:
        in_specs.append(pl.BlockSpec((1, k), lambda i, j: (0, 0)))
        args.append(gain.reshape(1, k))
    in_specs.append(pl.BlockSpec((k, tn), lambda i, j: (0, j)))
    args.append(w)
    if res is not None:
        in_specs.append(pl.BlockSpec((tm, tn), lambda i, j: (i, j)))
        args.append(res)
    return pl.pallas_call(
        functools.partial(_mm_body, norm=norm, has_res=res is not None),
        grid=(m // tm, n // tn),
        in_specs=in_specs,
        out_specs=pl.BlockSpec((tm, tn), lambda i, j: (i, j)),
        out_shape=jax.ShapeDtypeStruct((m, n), F32),
        scratch_shapes=[pltpu.VMEM((tm, k), BF)],
        compiler_params=_params("parallel", "arbitrary"),
        name=name,
    )(*args)


_Q0 = 0
_KV0 = NSA_HEADS * HEAD_DIM
_DQ0 = _KV0 + 6 * NSA_KV_HEADS * HEAD_DIM
_DK0 = _DQ0 + DIFF_HEADS * HEAD_DIM
_DV0 = _DK0 + DIFF_HEADS * HEAD_DIM
_GT0 = _DV0 + DIFF_HEADS * HEAD_DIM
_ZW = _GT0 + NSA_KV_HEADS * LANES


def _att_post_body(z_ref, qg_ref, kg_ref, dqg_ref, dkg_ref,
                   qn_ref, rows_ref, win_ref, gate_ref, dqn_ref, drows_ref):
    hd = HEAD_DIM
    qg = qg_ref[...]
    for h in range(NSA_HEADS):
        x = z_ref[:, _Q0 + h * hd:_Q0 + (h + 1) * hd]
        qn_ref[:, h * hd:(h + 1) * hd] = (_rms(x, qg) * (HEAD_DIM ** -0.5)).astype(qn_ref.dtype)
    gw = NSA_KV_HEADS * hd
    for br in range(3):
        for g in range(NSA_KV_HEADS):
            ko = _KV0 + br * 2 * gw + g * hd
            k = _rms(z_ref[:, ko:ko + hd], kg_ref[br:br + 1, :])
            v = z_ref[:, ko + gw:ko + gw + hd]
            if br < 2:
                rows_ref[:, br * 2 * gw + g * hd:br * 2 * gw + (g + 1) * hd] = k
                rows_ref[:, br * 2 * gw + gw + g * hd:br * 2 * gw + gw + (g + 1) * hd] = v
            else:
                win_ref[:, g * hd:(g + 1) * hd] = k
                win_ref[:, gw + g * hd:gw + (g + 1) * hd] = v
    gate_ref[...] = jax.nn.sigmoid(z_ref[:, _GT0:_ZW])
    dqg = dqg_ref[...]
    dkg = dkg_ref[...]
    for h in range(DIFF_HEADS):
        x = z_ref[:, _DQ0 + h * hd:_DQ0 + (h + 1) * hd]
        dqn_ref[:, h * hd:(h + 1) * hd] = (_rms_halves(x, dqg) * (DIFF_DK ** -0.5)).astype(dqn_ref.dtype)
        x = z_ref[:, _DK0 + h * hd:_DK0 + (h + 1) * hd]
        drows_ref[:, h * hd:(h + 1) * hd] = _rms_halves(x, dkg)
    dvw = DIFF_HEADS * hd
    drows_ref[:, dvw:2 * dvw] = z_ref[:, _DV0:_DV0 + dvw]


def _att_post(z, q_gain, k_gain, dq_gain, dk_gain, *, tm):
    n = z.shape[0]
    assert n % tm == 0 and z.shape[1] == _ZW
    hd = HEAD_DIM
    widths = (NSA_HEADS * hd, 4 * NSA_KV_HEADS * hd, 2 * NSA_KV_HEADS * hd, NSA_KV_HEADS * LANES,
              DIFF_HEADS * hd, 2 * DIFF_HEADS * hd)
    dtypes = (BF, F32, F32, F32, BF, F32)
    small = lambda r: pl.BlockSpec((r, hd), lambda i: (0, 0))
    return pl.pallas_call(
        _att_post_body,
        grid=(n // tm,),
        in_specs=[pl.BlockSpec((tm, _ZW), lambda i: (i, 0)), small(1), small(3), small(1), small(1)],
        out_specs=[pl.BlockSpec((tm, w), lambda i: (i, 0)) for w in widths],
        out_shape=[jax.ShapeDtypeStruct((n, w), d) for w, d in zip(widths, dtypes)],
        compiler_params=_params("parallel"),
        name="att_post",
    )(z, q_gain.reshape(1, hd), k_gain, dq_gain.reshape(1, hd), dk_gain.reshape(1, hd))


def _compress(x, w):
    length, c = x.shape
    npair = length // (2 * L_CMP)
    x3 = x.reshape(npair, 2 * L_CMP, c)
    even = jnp.sum(x3[:, :L_CMP, :] * w, axis=1)
    odd = jnp.sum(x3[:, L_CMP:, :] * w, axis=1)
    pad = jnp.zeros((LANES - 2 * npair, c), F32)
    return jnp.concatenate([even, odd, pad], axis=0)


def _nsa_cmp_body(x_ref, w_ref, o_ref):
    o_ref[...] = _compress(x_ref[...], w_ref[...])


def _nsa_cmp(rows, cmp_w, *, batch, seq):
    gw = NSA_KV_HEADS * HEAD_DIM
    assert seq % (2 * L_CMP) == 0 and seq // L_CMP <= LANES
    w = jnp.concatenate([jnp.tile(cmp_w[0], (1, NSA_KV_HEADS)), jnp.tile(cmp_w[1], (1, NSA_KV_HEADS))], axis=1)
    return pl.pallas_call(
        _nsa_cmp_body,
        grid=(batch,),
        in_specs=[pl.BlockSpec((seq, 2 * gw), lambda b: (b, 0)),
                  pl.BlockSpec((L_CMP, 2 * gw), lambda b: (0, 0))],
        out_specs=pl.BlockSpec((None, LANES, 2 * gw), lambda b: (b, 0, 0)),
        out_shape=jax.ShapeDtypeStruct((batch, LANES, 2 * gw), F32),
        compiler_params=_params("parallel"),
        name="nsa_cmp",
    )(rows, w)


def _cmp_positions(ncmp):
    c = lax.broadcasted_iota(I32, (1, LANES), 1)
    half = ncmp // 2
    blk = jnp.where(c < half, 2 * c, 2 * (c - half) + 1)
    return c, blk * L_CMP + (L_CMP - 1), c < ncmp


def _select_blocks(p_heads, t1, ncmp, nsel_blocks):
    c = lax.broadcasted_iota(I32, (1, LANES), 1)
    half = ncmp // 2
    imp = p_heads[0]
    for p in p_heads[1:]:
        imp = imp + p
    imp = jnp.where(c < half, imp + pltpu.roll(imp, LANES - half, 1), 0.0)
    cur = t1 // L_SEL
    forced = (c == 0) | (c == cur) | (c == cur - 1)
    valid = (c * L_SEL <= t1) & (c < nsel_blocks)
    score = jnp.where(valid, imp + jnp.where(forced, FORCE_BONUS, 0.0), NEG)
    rank = jnp.zeros(score.shape, I32)
    for jp in range(nsel_blocks):
        sj = score[:, jp:jp + 1]
        beats = (sj > score) | ((sj == score) & (c > jp))
        rank = rank + beats.astype(I32)
    return (rank < min(N_SEL, nsel_blocks)) & (c < nsel_blocks)


def _expand_blocks(sel01, length):
    j = lax.broadcasted_iota(I32, (LANES, length), 0)
    l = lax.broadcasted_iota(I32, (LANES, length), 1)
    e = jnp.where(l // L_SEL == j, 1.0, 0.0).astype(BF)
    return _dot(sel01.astype(BF), e)


CAUSAL_STEP = 512


def _causal_variants(need, seq, fn, out_ref):
    step = min(CAUSAL_STEP, seq)
    assert seq % step == 0
    for klen in range(step, seq + 1, step):
        @pl.when((need <= klen) & (need > klen - step))
        def _(klen=klen):
            out_ref[...] = fn(klen)


def _nsa_prompt_body(q_ref, gate_ref, kc_ref, vc_ref, ks_ref, vs_ref, kw_ref, vw_ref, o_ref, os_ref, *, tq, seq):
    g = pl.program_id(1)
    t0 = pl.program_id(2) * tq
    hg = NSA_HPG
    hd = HEAD_DIM
    rows = hg * tq
    q = q_ref[...]
    qs = jnp.concatenate([q[:, h * hd:(h + 1) * hd] for h in range(hg)], axis=0)
    r_io = lax.broadcasted_iota(I32, (rows, 1), 0)
    hrow = r_io // tq
    tpos = t0 + (r_io - hrow * tq)
    slope = _pow2_neg(g * hg + hrow + 1)
    t1 = t0 + lax.broadcasted_iota(I32, (tq, 1), 0)
    ncmp = seq // L_CMP
    nblk = -(-seq // L_SEL)

    _, pc, cvalid = _cmp_positions(ncmp)
    s = _dot_nt(qs, kc_ref[...].astype(BF)) - slope * (tpos - pc).astype(F32)
    p = _msoftmax(s, cvalid & (pc <= tpos))
    o_c = _dot(p.astype(BF), vc_ref[...].astype(BF))
    sel = _select_blocks([p[h * tq:(h + 1) * tq] for h in range(hg)], t1, ncmp, nblk)
    sel01 = jnp.where(sel, 1.0, 0.0)

    head_slope = [_pow2_neg(jnp.full((1, 1), g * hg + h + 1, I32)) for h in range(hg)]

    def attend(k, v, pos, ok):
        s = _dot_nt(qs, k)
        posf = pos.astype(F32)
        masked = jnp.where(ok, 0.0, NEG)
        ps = []
        for h in range(hg):
            logit = s[h * tq:(h + 1) * tq] + (masked + head_slope[h] * posf)
            e = jnp.exp(logit - jnp.max(logit, axis=-1, keepdims=True))
            ps.append((e * (1.0 / jnp.sum(e, axis=-1, keepdims=True))).astype(BF))
        return _dot(jnp.concatenate(ps, axis=0), v)

    def selected(klen):
        pos = lax.broadcasted_iota(I32, (1, klen), 1)
        ok = (pos <= t1) & (_expand_blocks(sel01, klen) > 0.5)
        return attend(ks_ref[0:klen, :].astype(BF), vs_ref[0:klen, :].astype(BF), pos, ok)

    _causal_variants(t0 + tq, seq, selected, os_ref)
    o_s = os_ref[...]

    wl = WINDOW + tq
    start = pl.multiple_of(jnp.maximum(t0 - WINDOW, 0), tq)
    posw = start + lax.broadcasted_iota(I32, (1, wl), 1)
    o_w = attend(kw_ref[pl.ds(start, wl), :].astype(BF), vw_ref[pl.ds(start, wl), :].astype(BF), posw,
                 (posw <= t1) & (posw > t1 - WINDOW))

    gt = gate_ref[...]
    for h in range(hg):
        rs = slice(h * tq, (h + 1) * tq)
        o = gt[:, 3 * h:3 * h + 1] * o_c[rs] + gt[:, 3 * h + 1:3 * h + 2] * o_s[rs] + gt[:, 3 * h + 2:3 * h + 3] * o_w[rs]
        o_ref[:, h * hd:(h + 1) * hd] = o.astype(o_ref.dtype)


def _nsa_prompt(qn, gates, cmp, rows, win, *, batch, seq, tq):
    hd = HEAD_DIM
    ng = NSA_KV_HEADS
    nq = seq // tq
    assert seq % tq == 0 and tq % LANES == 0 and WINDOW % tq == 0 and seq >= WINDOW + tq
    gq = NSA_HPG * hd
    seq_blk = lambda col: pl.BlockSpec((seq, hd), lambda b, g, i, col=col: (b, col + g))
    return pl.pallas_call(
        functools.partial(_nsa_prompt_body, tq=tq, seq=seq),
        grid=(batch, ng, nq),
        in_specs=[
            pl.BlockSpec((tq, gq), lambda b, g, i: (b * nq + i, g)),
            pl.BlockSpec((tq, LANES), lambda b, g, i: (b * nq + i, g)),
            pl.BlockSpec((None, LANES, hd), lambda b, g, i: (b, 0, g)),
            pl.BlockSpec((None, LANES, hd), lambda b, g, i: (b, 0, ng + g)),
            seq_blk(2 * ng), seq_blk(3 * ng),
            seq_blk(0), seq_blk(ng),
        ],
        out_specs=pl.BlockSpec((tq, gq), lambda b, g, i: (b * nq + i, g)),
        out_shape=jax.ShapeDtypeStruct((batch * seq, ng * gq), BF),
        scratch_shapes=[pltpu.VMEM((NSA_HPG * tq, hd), F32)],
        compiler_params=_params("parallel", "parallel", "arbitrary"),
        name="nsa_prompt",
    )(qn, gates, cmp, cmp, rows, rows, win, win)


def _lambda(lv, lam_init):
    a = jnp.sum(lv[0:1, :] * lv[1:2, :], axis=-1, keepdims=True)
    b = jnp.sum(lv[2:3, :] * lv[3:4, :], axis=-1, keepdims=True)
    return jnp.exp(a) - jnp.exp(b) + lam_init


def _diff_prompt_body(lv_ref, og_ref, q_ref, k_ref, v_ref, o_ref, acc_ref, *, tq, seq, lam_init):
    h = pl.program_id(1)
    t0 = pl.program_id(2) * tq
    q = q_ref[...].astype(F32)
    lo = lax.broadcasted_iota(I32, q.shape, 1) < DIFF_DK
    qs = jnp.concatenate([jnp.where(lo, q, 0.0), jnp.where(lo, 0.0, q)], axis=0).astype(BF)
    t1 = t0 + lax.broadcasted_iota(I32, (tq, 1), 0)
    slope = _pow2_neg(jnp.full((1, 1), h + 1, I32))
    lam = _lambda(lv_ref[...], lam_init)

    def attend(klen):
        pos = lax.broadcasted_iota(I32, (1, klen), 1)
        bias = jnp.where(pos <= t1, slope * pos.astype(F32), NEG)
        s = _dot_nt(qs, k_ref[0:klen, :].astype(BF))
        parts = []
        for mp in range(2):
            logit = s[mp * tq:(mp + 1) * tq] + bias
            e = jnp.exp(logit - jnp.max(logit, axis=-1, keepdims=True))
            parts.append((e, 1.0 / jnp.sum(e, axis=-1, keepdims=True)))
        a = parts[0][0] * parts[0][1] - parts[1][0] * (lam * parts[1][1])
        return _dot(a.astype(BF), v_ref[0:klen, :].astype(BF))

    _causal_variants(t0 + tq, seq, attend, acc_ref)
    o_ref[...] = (_rms(acc_ref[...], og_ref[...]) * (1.0 - lam_init)).astype(o_ref.dtype)


def _diff_prompt(dqn, drows, lam_vec, out_gain, *, batch, seq, tq, lam_init):
    hd = HEAD_DIM
    nh = DIFF_HEADS
    nq = seq // tq
    assert seq % tq == 0
    return pl.pallas_call(
        functools.partial(_diff_prompt_body, tq=tq, seq=seq, lam_init=lam_init),
        grid=(batch, nh, nq),
        in_specs=[
            pl.BlockSpec((4, DIFF_DK), lambda b, h, i: (0, 0)),
            pl.BlockSpec((1, hd), lambda b, h, i: (0, 0)),
            pl.BlockSpec((tq, hd), lambda b, h, i: (b * nq + i, h)),
            pl.BlockSpec((seq, hd), lambda b, h, i: (b, h)),
            pl.BlockSpec((seq, hd), lambda b, h, i: (b, nh + h)),
        ],
        out_specs=pl.BlockSpec((tq, hd), lambda b, h, i: (b * nq + i, h)),
        out_shape=jax.ShapeDtypeStruct((batch * seq, nh * hd), BF),
        scratch_shapes=[pltpu.VMEM((tq, hd), F32)],
        compiler_params=_params("parallel", "parallel", "arbitrary"),
        name="diff_prompt",
    )(lam_vec, out_gain.reshape(1, hd), dqn, drows, drows)


def _page_copies(pt_ref, cache_ref, buf_ref, sem_ref, b, slot, start):
    npages = pt_ref.shape[1]
    page, nkind, nhead = cache_ref.shape[1:4]

    def one_page(p, c):
        phys = pt_ref[b, p]
        dst_rows = pl.ds(pl.multiple_of(p * page, page), page)
        for kind in range(nkind):
            for h in range(nhead):
                cp = pltpu.make_async_copy(cache_ref.at[phys, :, kind, h, :],
                                           buf_ref.at[slot, kind, h, dst_rows, :], sem_ref.at[slot])
                if start:
                    cp.start()
                else:
                    cp.wait()
        return c

    lax.fori_loop(0, npages, one_page, 0)


def _state_copies(state_ref, buf_ref, sem_ref, b, slot, start):
    nkind, nhead = state_ref.shape[2:4]
    for kind in range(nkind):
        for h in range(nhead):
            cp = pltpu.make_async_copy(state_ref.at[b, :, kind, h, :], buf_ref.at[slot, kind, h], sem_ref.at[slot])
            if start:
                cp.start()
            else:
                cp.wait()


def _double_buffered(fetch):
    b = pl.program_id(0)
    slot = b % 2

    @pl.when(b == 0)
    def _():
        fetch(b, slot, True)

    @pl.when(b + 1 < pl.num_programs(0))
    def _():
        fetch(b + 1, 1 - slot, True)

    fetch(b, slot, False)
    return slot


def _nsa_sample_body(pt_ref, q_ref, gate_ref, nrow_ref, nwin_ref, cw_ref, cache_ref, wstate_ref, o_ref,
                     kv_buf, w_buf, kv_sem, w_sem, *, past, ts):
    def fetch(b, slot, start):
        _page_copies(pt_ref, cache_ref, kv_buf, kv_sem, b, slot, start)
        _state_copies(wstate_ref, w_buf, w_sem, b, slot, start)

    slot = _double_buffered(fetch)
    past_ref = kv_buf.at[slot]
    wst_ref = w_buf.at[slot]
    hg = NSA_HPG
    hd = HEAD_DIM
    ng = NSA_KV_HEADS
    rows = q_ref.shape[1]
    tsp = rows // hg
    npad = nrow_ref.shape[0]
    gw = ng * hd
    ncmp = (past + ts) // L_CMP
    nblk = -(-(past + ts) // L_SEL)
    new_blk = past // L_SEL
    r_io = lax.broadcasted_iota(I32, (rows, 1), 0)
    hrow = r_io // tsp
    trow = r_io - hrow * tsp
    tpos = past + trow
    t1 = past + lax.broadcasted_iota(I32, (tsp, 1), 0)
    jn = lax.broadcasted_iota(I32, (1, npad), 1)
    new_ok = (jn <= trow) & (jn < ts)
    dist_new = (trow - jn).astype(F32)
    _, pc, cvalid = _cmp_positions(ncmp)
    pos = lax.broadcasted_iota(I32, (1, past), 1)
    nwst = wst_ref.shape[2]
    posw = past - nwst + lax.broadcasted_iota(I32, (1, nwst), 1)
    for g in range(ng):
        qs = q_ref[g]
        slope = _pow2_neg(g * hg + hrow + 1)
        kc = _compress(past_ref[0, g], cw_ref[0]).astype(BF)
        vc = _compress(past_ref[1, g], cw_ref[1]).astype(BF)
        s = _dot_nt(qs, kc) - slope * (tpos - pc).astype(F32)
        p = _msoftmax(s, cvalid & (pc <= tpos))
        o_c = _dot(p.astype(BF), vc)
        sel = _select_blocks([p[h * tsp:(h + 1) * tsp] for h in range(hg)], t1, ncmp, nblk)
        sel01 = jnp.concatenate([jnp.where(sel, 1.0, 0.0)] * hg, axis=0)
        selm = _expand_blocks(sel01, past)
        sel_new = sel01[:, new_blk:new_blk + 1] > 0.5
        kn = nrow_ref[:, 2 * gw + g * hd:2 * gw + (g + 1) * hd].astype(BF)
        vn = nrow_ref[:, 3 * gw + g * hd:3 * gw + (g + 1) * hd].astype(BF)
        s1 = _dot_nt(qs, past_ref[2, g].astype(BF)) - slope * (tpos - pos).astype(F32)
        s2 = _dot_nt(qs, kn) - slope * dist_new
        p1, p2 = _msoftmax2(s1, selm > 0.5, s2, new_ok & sel_new)
        o_s = _dot(p1.astype(BF), past_ref[3, g].astype(BF)) + _dot(p2.astype(BF), vn)
        kn = nwin_ref[:, g * hd:(g + 1) * hd].astype(BF)
        vn = nwin_ref[:, gw + g * hd:gw + (g + 1) * hd].astype(BF)
        dist = tpos - posw
        s1 = _dot_nt(qs, wst_ref[0, g].astype(BF)) - slope * dist.astype(F32)
        s2 = _dot_nt(qs, kn) - slope * dist_new
        p1, p2 = _msoftmax2(s1, (dist >= 0) & (dist < WINDOW), s2, new_ok)
        o_w = _dot(p1.astype(BF), wst_ref[1, g].astype(BF)) + _dot(p2.astype(BF), vn)
        gt = gate_ref[g]
        o_ref[g] = gt[:, 0:1] * o_c + gt[:, 1:2] * o_s + gt[:, 2:3] * o_w


def _nsa_sample(q, gates, cache, page_table, win_state, new_rows, new_win, cmp_w, *, ts):
    nb, ng, rows, hd = q.shape
    past = page_table.shape[1] * cache.shape[1]
    nwst = win_state.shape[1]
    npad = new_rows.shape[1]
    assert past % (2 * L_CMP) == 0 and past % L_SEL == 0 and nwst == WINDOW and (past + ts) // L_CMP == past // L_CMP
    assert cache.shape[2:] == (4, ng, hd) and win_state.shape[2:] == (2, ng, hd)
    return pl.pallas_call(
        functools.partial(_nsa_sample_body, past=past, ts=ts),
        grid_spec=pltpu.PrefetchScalarGridSpec(
            num_scalar_prefetch=1,
            grid=(nb,),
            in_specs=[
                pl.BlockSpec((None, ng, rows, hd), lambda b, pt: (b, 0, 0, 0)),
                pl.BlockSpec((None, ng, rows, LANES), lambda b, pt: (b, 0, 0, 0)),
                pl.BlockSpec((None, npad, new_rows.shape[2]), lambda b, pt: (b, 0, 0)),
                pl.BlockSpec((None, npad, new_win.shape[2]), lambda b, pt: (b, 0, 0)),
                pl.BlockSpec((2, L_CMP, hd), lambda b, pt: (0, 0, 0)),
                pl.BlockSpec(memory_space=pl.ANY),
                pl.BlockSpec(memory_space=pl.ANY),
            ],
            out_specs=pl.BlockSpec((None, ng, rows, hd), lambda b, pt: (b, 0, 0, 0)),
            scratch_shapes=[pltpu.VMEM((2, 4, ng, past, hd), F32), pltpu.VMEM((2, 2, ng, nwst, hd), F32),
                            pltpu.SemaphoreType.DMA((2,)), pltpu.SemaphoreType.DMA((2,))],
        ),
        out_shape=jax.ShapeDtypeStruct((nb, ng, rows, hd), F32),
        compiler_params=_params("arbitrary"),
        name="nsa_sample",
    )(page_table, q, gates, new_rows, new_win, cmp_w, cache, win_state)


def _diff_sample_body(pt_ref, lv_ref, og_ref, q_ref, ndrow_ref, cache_ref, o_ref, kv_buf, kv_sem, *, past, ts, lam_init):
    slot = _double_buffered(functools.partial(_page_copies, pt_ref, cache_ref, kv_buf, kv_sem))
    past_ref = kv_buf.at[slot]
    hd = HEAD_DIM
    nh = DIFF_HEADS
    npad = ndrow_ref.shape[0]
    rows = q_ref.shape[1]
    tsp = rows // 2
    r_io = lax.broadcasted_iota(I32, (rows, 1), 0)
    trow = jnp.where(r_io >= tsp, r_io - tsp, r_io)
    tpos = past + trow
    dist = (tpos - lax.broadcasted_iota(I32, (1, past), 1)).astype(F32)
    jn = lax.broadcasted_iota(I32, (1, npad), 1)
    new_ok = (jn <= trow) & (jn < ts)
    dist_new = (trow - jn).astype(F32)
    lam = _lambda(lv_ref[...], lam_init)
    og = og_ref[...]
    for h in range(nh):
        slope = 2.0 ** -(h + 1)
        qs = q_ref[h]
        kn = ndrow_ref[:, h * hd:(h + 1) * hd].astype(BF)
        vn = ndrow_ref[:, (nh + h) * hd:(nh + h + 1) * hd].astype(BF)
        s1 = _dot_nt(qs, past_ref[0, h].astype(BF)) - slope * dist
        s2 = _dot_nt(qs, kn) - slope * dist_new
        p1, p2 = _msoftmax2(s1, dist >= 0.0, s2, new_ok)
        a1 = p1 - lam * jnp.concatenate([p1[tsp:], p1[:tsp]], axis=0)
        a2 = p2 - lam * jnp.concatenate([p2[tsp:], p2[:tsp]], axis=0)
        o = _dot(a1.astype(BF), past_ref[1, h].astype(BF)) + _dot(a2.astype(BF), vn)
        o_ref[h] = _rms(o, og) * (1.0 - lam_init)


def _diff_sample(q, cache, page_table, new_drows, lam_vec, out_gain, *, ts, lam_init):
    nb, nh, rows, hd = q.shape
    past = page_table.shape[1] * cache.shape[1]
    npad = new_drows.shape[1]
    assert cache.shape[2:] == (2, nh, hd)
    return pl.pallas_call(
        functools.partial(_diff_sample_body, past=past, ts=ts, lam_init=lam_init),
        grid_spec=pltpu.PrefetchScalarGridSpec(
            num_scalar_prefetch=1,
            grid=(nb,),
            in_specs=[
                pl.BlockSpec((4, DIFF_DK), lambda b, pt: (0, 0)),
                pl.BlockSpec((1, hd), lambda b, pt: (0, 0)),
                pl.BlockSpec((None, nh, rows, hd), lambda b, pt: (b, 0, 0, 0)),
                pl.BlockSpec((None, npad, new_drows.shape[2]), lambda b, pt: (b, 0, 0)),
                pl.BlockSpec(memory_space=pl.ANY),
            ],
            out_specs=pl.BlockSpec((None, nh, rows, hd), lambda b, pt: (b, 0, 0, 0)),
            scratch_shapes=[pltpu.VMEM((2, 2, nh, past, hd), F32), pltpu.SemaphoreType.DMA((2,))],
        ),
        out_shape=jax.ShapeDtypeStruct((nb, nh, rows, hd), F32),
        compiler_params=_params("arbitrary"),
        name="diff_sample",
    )(page_table, lam_vec, out_gain.reshape(1, hd), q, new_drows, cache)


def _rows(a, lo, n):
    return lax.slice_in_dim(a, lo, lo + n, axis=a.ndim - 2)


def _pool_conv(p_ext, u_ext, b_gate, cnt_pos, pool_w_ref, pool_scale, conv_w, ts):
    gwid = p_ext.shape[-1] // len(POOL_WINDOWS)
    lead = p_ext.shape[:-2]
    outs = []
    for gi, w in enumerate(POOL_WINDOWS):
        x = p_ext[..., gi * gwid:(gi + 1) * gwid]
        acc = x
        span = 1
        while span < w:
            n = acc.shape[-2]
            acc = _rows(acc, span, n - span) + _rows(acc, 0, n - span)
            span *= 2
        win_sum = _rows(acc, HALO - (w - 1), ts)
        cnt = jnp.minimum(w, cnt_pos + 1).astype(F32)
        m = win_sum / cnt - _rows(x, HALO, ts)
        m2 = m.reshape((-1, gwid)).astype(BF)
        outs.append(_dot(m2, pool_w_ref[gi].astype(BF)).reshape(lead + (ts, gwid)))
    y_pool = jnp.concatenate(outs, axis=-1) * pool_scale
    conv = None
    for j in range(CONV_K):
        term = _rows(u_ext, HALO - (CONV_K - 1) + j, ts) * conv_w[j:j + 1, :]
        conv = term if conv is None else conv + term
    return jnp.concatenate([y_pool, b_gate * conv], axis=-1).astype(BF)


def _pc_prompt_body(zc_ref, zh_ref, pw_ref, ps_ref, cw_ref, o_ref, *, ts, cw):
    i = pl.program_id(1)
    keep = i > 0
    p_ext = jnp.concatenate([jnp.where(keep, zh_ref[:, 0:cw], 0.0), zc_ref[:, 0:cw]], axis=0)
    u_h = jnp.where(keep, zh_ref[:, 2 * cw:3 * cw] * zh_ref[:, 3 * cw:4 * cw], 0.0)
    u_ext = jnp.concatenate([u_h, zc_ref[:, 2 * cw:3 * cw] * zc_ref[:, 3 * cw:4 * cw]], axis=0)
    cnt_pos = i * ts + lax.broadcasted_iota(I32, (ts, 1), 0)
    o_ref[...] = _pool_conv(p_ext, u_ext, zc_ref[:, cw:2 * cw], cnt_pos, pw_ref, ps_ref[...], cw_ref[...], ts)


def _pc_prompt(z, pool_w, pool_scale, conv_w, *, batch, seq, ts):
    cw = pool_scale.shape[0]
    nt = seq // ts
    assert seq % ts == 0 and ts % HALO == 0 and z.shape[1] == 4 * cw
    hb = ts // HALO
    return pl.pallas_call(
        functools.partial(_pc_prompt_body, ts=ts, cw=cw),
        grid=(batch, nt),
        in_specs=[
            pl.BlockSpec((ts, 4 * cw), lambda b, i: (b * nt + i, 0)),
            pl.BlockSpec((HALO, 4 * cw), lambda b, i: (jnp.maximum((b * nt + i) * hb - 1, 0), 0)),
            pl.BlockSpec(pool_w.shape, lambda b, i: (0, 0, 0)),
            pl.BlockSpec((1, cw), lambda b, i: (0, 0)),
            pl.BlockSpec((CONV_K, cw), lambda b, i: (0, 0)),
        ],
        out_specs=pl.BlockSpec((ts, 2 * cw), lambda b, i: (b * nt + i, 0)),
        out_shape=jax.ShapeDtypeStruct((batch * seq, 2 * cw), BF),
        compiler_params=_params("parallel", "arbitrary"),
        name="pool_conv_prompt",
    )(z, z, pool_w, pool_scale.reshape(1, cw), conv_w)


def _pc_sample_body(pe_ref, ue_ref, bg_ref, pw_ref, ps_ref, cw_ref, o_ref, *, ts, past):
    cnt_pos = past + lax.broadcasted_iota(I32, (ts, 1), 0)
    o_ref[...] = _pool_conv(pe_ref[...], ue_ref[...], bg_ref[...], cnt_pos, pw_ref, ps_ref[...], cw_ref[...], ts)


def _pc_sample(p_ext, u_ext, b_gate, pool_w, pool_scale, conv_w, *, past, tb):
    nb, ext, cw = p_ext.shape
    ts = ext - HALO
    assert nb % tb == 0
    blk = lambda r: pl.BlockSpec((tb, r, cw), lambda b: (b, 0, 0))
    return pl.pallas_call(
        functools.partial(_pc_sample_body, ts=ts, past=past),
        grid=(nb // tb,),
        in_specs=[blk(ext), blk(ext), blk(ts),
                  pl.BlockSpec(pool_w.shape, lambda b: (0, 0, 0)),
                  pl.BlockSpec((1, cw), lambda b: (0, 0)),
                  pl.BlockSpec((CONV_K, cw), lambda b: (0, 0))],
        out_specs=pl.BlockSpec((tb, ts, 2 * cw), lambda b: (b, 0, 0)),
        out_shape=jax.ShapeDtypeStruct((nb, ts, 2 * cw), BF),
        compiler_params=_params("parallel"),
        name="pool_conv_sample",
    )(p_ext, u_ext, b_gate, pool_w, pool_scale.reshape(1, cw), conv_w)


def _peer_score_body(x_ref, g_ref, wq_ref, keys_ref, s_ref, xn_ref):
    @pl.when(pl.program_id(1) == 0)
    def _():
        xn_ref[...] = _rms(x_ref[...], g_ref[...]).astype(xn_ref.dtype)

    z = _dot(xn_ref[...], wq_ref[...].astype(BF)).astype(BF)
    nk = keys_ref.shape[1]
    dk = keys_ref.shape[2]
    for r in range(keys_ref.shape[0]):
        s_ref[r * nk:(r + 1) * nk, :] = _dot_nt(keys_ref[r].astype(BF), z[:, r * dk:(r + 1) * dk])


def _peer_scores(x, gain, wq, keys, li, *, tm):
    n, d = x.shape
    nsub, nk, dk = keys.shape[1:]
    per = 4
    assert n % tm == 0 and tm % LANES == 0 and nsub % per == 0 and wq.shape[2] == nsub * dk
    return pl.pallas_call(
        _peer_score_body,
        grid=(n // tm, nsub // per),
        in_specs=[
            pl.BlockSpec((tm, d), lambda i, j: (i, 0)),
            pl.BlockSpec((1, d), lambda i, j: (0, 0)),
            pl.BlockSpec((None, d, per * dk), lambda i, j: (li, 0, j)),
            pl.BlockSpec((None, per, nk, dk), lambda i, j: (li, j, 0, 0)),
        ],
        out_specs=[pl.BlockSpec((per * nk, tm), lambda i, j: (j, i)),
                   pl.BlockSpec((tm, d), lambda i, j: (i, 0))],
        out_shape=[jax.ShapeDtypeStruct((nsub * nk, n), F32), jax.ShapeDtypeStruct((n, d), BF)],
        compiler_params=_params("parallel", "arbitrary"),
        name="peer_scores",
    )(x, gain.reshape(1, d), wq, keys)


def _topk_rows(x, k, code):
    slot = lax.broadcasted_iota(I32, (k, x.shape[1]), 0)
    vals = jnp.zeros((k, x.shape[1]), F32)
    idxs = jnp.zeros((k, x.shape[1]), F32)
    for kk in range(k):
        m = jnp.max(x, axis=0, keepdims=True)
        idx = jnp.min(jnp.where(x == m, code, float(2 ** 24)), axis=0, keepdims=True)
        vals = jnp.where(slot == kk, m, vals)
        idxs = jnp.where(slot == kk, idx, idxs)
        x = jnp.where(code == idx, -jnp.inf, x)
    return vals, idxs.astype(I32)


def _pair_candidates(v0, v1):
    k = PEER_TOPK
    assert k == 16 and v0.shape[0] == k
    half = k // 2
    blocks = [v0[0:1] + v1] + [v0[a:a + 1] + v1[0:half] for a in range(1, half)] + [v0[half:] + v1[0:1]]
    r = lax.broadcasted_iota(I32, (k + (half - 1) * half + half, v0.shape[1]), 0)
    mid = r - k
    code = jnp.where(r < k, r,
                     jnp.where(mid < (half - 1) * half, (mid // half + 1) * k + mid % half,
                               (half + mid - (half - 1) * half) * k))
    return jnp.concatenate(blocks, axis=0), code.astype(F32)


def _peer_topk_body(s_ref, i1_ref, i2_ref, g_ref, sv_ref, si_ref, i1t_ref, i2t_ref, gt_ref):
    nsub = sv_ref.shape[0]
    k = PEER_TOPK
    nk = PEER_NKEYS

    def sub_key(gi, c):
        x = s_ref[pl.ds(pl.multiple_of(gi * nk, nk), nk), :]
        v, i = _topk_rows(x, k, lax.broadcasted_iota(I32, x.shape, 0).astype(F32))
        sv_ref[gi] = v
        si_ref[gi] = i
        return c

    lax.fori_loop(0, nsub, sub_key, 0)

    def head(h, c):
        comb, code = _pair_candidates(sv_ref[2 * h], sv_ref[2 * h + 1])
        cv, ci = _topk_rows(comb, k, code)
        a = ci // k
        b = ci - a * k
        s0 = si_ref[2 * h]
        s1 = si_ref[2 * h + 1]
        i1 = jnp.zeros(ci.shape, I32)
        i2 = jnp.zeros(ci.shape, I32)
        for q in range(k):
            i1 = jnp.where(a == q, s0[q:q + 1], i1)
            i2 = jnp.where(b == q, s1[q:q + 1], i2)
        e = jnp.exp(cv - cv[0:1])
        rows = pl.ds(pl.multiple_of(h * k, k), k)
        i1t_ref[rows, :] = i1
        i2t_ref[rows, :] = i2
        gt_ref[rows, :] = e / jnp.sum(e, axis=0, keepdims=True)
        return c

    lax.fori_loop(0, nsub // 2, head, 0)
    i1_ref[...] = i1t_ref[...].T
    i2_ref[...] = i2t_ref[...].T
    g_ref[...] = gt_ref[...].T


def _peer_topk(s_t, *, nh, tl):
    rows, n = s_t.shape
    npair = nh * PEER_TOPK
    assert rows == nh * 2 * PEER_NKEYS and n % tl == 0 and tl % LANES == 0
    out = pl.BlockSpec((tl, npair), lambda i: (i, 0))
    return pl.pallas_call(
        _peer_topk_body,
        grid=(n // tl,),
        in_specs=[pl.BlockSpec((rows, tl), lambda i: (0, i))],
        out_specs=[out, out, out],
        out_shape=[jax.ShapeDtypeStruct((n, npair), I32), jax.ShapeDtypeStruct((n, npair), I32),
                   jax.ShapeDtypeStruct((n, npair), F32)],
        scratch_shapes=[pltpu.VMEM((nh * 2, PEER_TOPK, tl), F32), pltpu.VMEM((nh * 2, PEER_TOPK, tl), I32),
                        pltpu.VMEM((npair, tl), I32), pltpu.VMEM((npair, tl), I32), pltpu.VMEM((npair, tl), F32)],
        compiler_params=_params("parallel"),
        name="peer_topk",
    )(s_t)


W_ROWS = 8


def _peer_w_body(a_ref, b_ref, g_ref, w_ref):
    tt = a_ref.shape[0]
    npair = a_ref.shape[2]
    io = lax.broadcasted_iota(I32, (tt, PEER_NKEYS, npair), 1)
    one_a = jnp.where(a_ref[...] == io, 1.0, 0.0).astype(BF)
    g = g_ref[...]
    g_hi = g.astype(BF).astype(F32)
    g_lo = g - g_hi
    hit = b_ref[...] == io
    dims = (((2,), (2,)), ((0,), (0,)))
    lhs = jnp.concatenate([one_a, one_a], axis=2)
    rhs = jnp.concatenate([jnp.where(hit, g_hi, 0.0).astype(BF), jnp.where(hit, g_lo, 0.0).astype(BF)], axis=2)
    w = lax.dot_general(lhs, rhs, dims, preferred_element_type=F32)
    nchunk = w_ref.shape[0]
    w4 = w.reshape(tt, nchunk, PEER_NKEYS // nchunk, PEER_NKEYS)
    for j in range(nchunk):
        w_ref[j] = w4[:, j]


def _peer_w(i1, i2, gate, *, tt):
    n, npair = i1.shape
    assert n % tt == 0
    blk = pl.BlockSpec((tt, 1, npair), lambda i: (i, 0, 0))
    return pl.pallas_call(
        _peer_w_body,
        grid=(n // tt,),
        in_specs=[blk, blk, blk],
        out_specs=pl.BlockSpec((PEER_NKEYS // W_ROWS, tt, W_ROWS, PEER_NKEYS), lambda i: (0, i, 0, 0)),
        out_shape=jax.ShapeDtypeStruct((PEER_NKEYS // W_ROWS, n, W_ROWS, PEER_NKEYS), F32),
        compiler_params=_params("parallel"),
        name="peer_w",
    )(i1.reshape(n, 1, npair), i2.reshape(n, 1, npair), gate.reshape(n, 1, npair))


def _peer_dense_body(xn_ref, xr_ref, u_ref, v_ref, w_ref, o_ref):
    @pl.when(pl.program_id(1) == 0)
    def _():
        o_ref[...] = xr_ref[...]

    tn = xn_ref.shape[0]
    nk = w_ref.shape[1]
    nw = w_ref.shape[0] // tn
    s = _dot_nt(xn_ref[...], u_ref[...])
    act = 0.5 * s * (1.0 + lax.erf(s * (2.0 ** -0.5)))
    wact = jnp.concatenate(
        [act[:, r * nk:(r + 1) * nk] * w_ref[pl.ds(r, tn, stride=nw), :] for r in range(nw)], axis=1)
    o_ref[...] += _dot(wact.astype(BF), v_ref[...])


def _peer_dense(xn, x_res, u_bf, v_bf, w, li, *, tn, ec):
    n, d = xn.shape
    ne = u_bf.shape[1]
    nchunk, _, nw, nk = w.shape
    assert n % tn == 0 and ne == nchunk * ec and ec == nw * nk
    w = w.reshape(nchunk, n * nw, nk)
    return pl.pallas_call(
        _peer_dense_body,
        grid=(n // tn, ne // ec),
        in_specs=[
            pl.BlockSpec((tn, d), lambda i, j: (i, 0)),
            pl.BlockSpec((tn, d), lambda i, j: (i, 0)),
            pl.BlockSpec((None, ec, d), lambda i, j: (li, j, 0)),
            pl.BlockSpec((None, ec, d), lambda i, j: (li, j, 0)),
            pl.BlockSpec((None, tn * nw, nk), lambda i, j: (j, i, 0)),
        ],
        out_specs=pl.BlockSpec((tn, d), lambda i, j: (i, 0)),
        out_shape=jax.ShapeDtypeStruct((n, d), F32),
        compiler_params=_params("parallel", "arbitrary"),
        name="peer_dense",
    )(xn, x_res, u_bf, v_bf, w)


def _peer_ffn(x, gain, wq, keys, u_bf, v_bf, li, *, tm):
    nl, nh = keys.shape[:2]
    s_t, xn = _peer_scores(x, gain, wq, keys.reshape(nl, nh * 2, PEER_NKEYS, keys.shape[-1]), li, tm=512)
    i1, i2, gate = _peer_topk(s_t, nh=nh, tl=256)
    w = _peer_w(i1, i2, gate, tt=32)
    return _peer_dense(xn, x, u_bf, v_bf, w, li, tn=tm // 2, ec=W_ROWS * PEER_NKEYS)


def _reorder_att_w_in(w_in):
    d = w_in.shape[0]
    o2 = _DQ0
    ngate = 3 * NSA_HPG
    gates = [jnp.pad(w_in[:, o2 + g * ngate:o2 + (g + 1) * ngate], ((0, 0), (0, LANES - ngate)))
             for g in range(NSA_KV_HEADS)]
    return jnp.concatenate([w_in[:, :o2], w_in[:, o2 + 3 * NSA_HEADS:]] + gates, axis=1)


def _attention_layer(x, li, n_prompt, batch, seq, nb, ts, cache_nsa_l, cache_diff_l, win_state_l, page_table,
                     norm_g, w_in, w_out, q_gain, k_gain, cmp_w, dq_gain, dk_gain, lam_vec, out_gain, *, tm):
    hd = HEAD_DIM
    ng = NSA_KV_HEADS
    hg = NSA_HPG
    nh = DIFF_HEADS
    lam_init = 0.8 - 0.6 * math.exp(-0.3 * li)

    z = _mm(x, _reorder_att_w_in(w_in), gain=norm_g, tm=tm, tn=256, name="att_in")
    qn, rows, win, gates, dqn, drows = _att_post(z, q_gain, k_gain, dq_gain, dk_gain, tm=272)

    cmp = _nsa_cmp(rows, cmp_w, batch=batch, seq=seq)
    o_nsa_p = _nsa_prompt(qn, gates, cmp, rows, win, batch=batch, seq=seq, tq=128)
    o_diff_p = _diff_prompt(dqn, drows, lam_vec, out_gain, batch=batch, seq=seq, tq=256, lam_init=lam_init)

    tsp = 8
    assert ts <= tsp
    pad_t = lambda a, ax: jnp.pad(a, [(0, tsp - ts) if k == ax else (0, 0) for k in range(a.ndim)])
    pad8 = lambda a: pad_t(a.reshape(nb, ts, a.shape[-1]), 1)
    rows_s, win_s, drows_s = rows[n_prompt:], win[n_prompt:], drows[n_prompt:]
    q_s = pad_t(qn[n_prompt:].reshape(nb, ts, ng, hg, hd).transpose(0, 2, 3, 1, 4), 3).reshape(nb, ng, hg * tsp, hd)
    g_s = gates[n_prompt:].reshape(nb, ts, ng, LANES)[..., :3 * hg].reshape(nb, ts, ng, hg, 3)
    g_s = pad_t(g_s.transpose(0, 2, 3, 1, 4), 3).reshape(nb, ng, hg * tsp, 3)
    g_s = jnp.pad(g_s, ((0, 0), (0, 0), (0, 0), (0, LANES - 3)))
    o_nsa_s = _nsa_sample(q_s, g_s, cache_nsa_l, page_table, win_state_l, pad8(rows_s), pad8(win_s), cmp_w, ts=ts)
    o_nsa_s = o_nsa_s.reshape(nb, ng, hg, tsp, hd)[:, :, :, :ts]
    o_nsa_s = o_nsa_s.transpose(0, 3, 1, 2, 4).reshape(nb * ts, ng * hg * hd)

    dq_s = pad_t(dqn[n_prompt:].reshape(nb, ts, nh, hd).transpose(0, 2, 1, 3), 2)
    lo = jnp.arange(hd) < DIFF_DK
    dq_s = jnp.concatenate([jnp.where(lo, dq_s, 0), jnp.where(lo, 0, dq_s)], axis=2).astype(BF)
    o_diff_s = _diff_sample(dq_s, cache_diff_l, page_table, pad8(drows_s), lam_vec, out_gain, ts=ts, lam_init=lam_init)
    o_diff_s = o_diff_s[:, :, :ts].transpose(0, 2, 1, 3).reshape(nb * ts, nh * hd)

    mix = jnp.concatenate([jnp.concatenate([o_nsa_p, o_diff_p], axis=1),
                           jnp.concatenate([o_nsa_s, o_diff_s], axis=1).astype(BF)], axis=0)
    x = _mm(mix, w_out, res=x, tm=tm, tn=512, name="att_out")

    rows_p = rows[:n_prompt].reshape(batch, seq, 4, ng, hd)
    win_p = win[:n_prompt].reshape(batch, seq, 2, ng, hd)
    keep_p = min(WINDOW, seq)
    win_new_s = win_s.reshape(nb, ts, 2, ng, hd)
    keep_s = win_state_l.shape[1]
    state = (rows_p, rows_s.reshape(nb, ts, 4, ng, hd),
             win_p[:, seq - keep_p:], jnp.concatenate([win_state_l, win_new_s], axis=1)[:, -keep_s:],
             drows[:n_prompt].reshape(batch, seq, 2, nh, hd), drows_s.reshape(nb, ts, 2, nh, hd))
    return x, state


def _pool_conv_layer(x, n_prompt, batch, seq, nb, ts, past, pool_hist, conv_hist,
                     norm_g, w_in, w_out, pool_w, pool_scale, conv_w, *, tm):
    cw = pool_scale.shape[0]
    z = _mm(x, w_in, gain=norm_g, tm=tm, tn=512, name="pc_in")
    mix_p = _pc_prompt(z, pool_w, pool_scale, conv_w, batch=batch, seq=seq, ts=256)

    zs = z[n_prompt:].reshape(nb, ts, 4 * cw)
    p_s = zs[..., :cw]
    u_s = zs[..., 2 * cw:3 * cw] * zs[..., 3 * cw:]
    tpad = 8
    front = lambda hist: jnp.pad(hist, ((0, 0), (HALO - hist.shape[1], 0), (0, 0)))
    back = lambda a: jnp.pad(a, ((0, 0), (0, tpad - ts), (0, 0)))
    p_ext = jnp.concatenate([front(pool_hist), back(p_s)], axis=1)
    u_ext = jnp.concatenate([front(conv_hist), back(u_s)], axis=1)
    mix_s = _pc_sample(p_ext, u_ext, back(zs[..., cw:2 * cw]), pool_w, pool_scale, conv_w, past=past, tb=8)
    mix = jnp.concatenate([mix_p, mix_s[:, :ts].reshape(nb * ts, 2 * cw)], axis=0)
    x = _mm(mix, w_out, res=x, tm=tm, tn=512, name="pc_out")

    tail = jnp.stack([z[(b + 1) * seq - POOL_HIST:(b + 1) * seq] for b in range(batch)])
    ct = tail[:, POOL_HIST - (CONV_K - 1):]
    u_tail = ct[..., 2 * cw:3 * cw] * ct[..., 3 * cw:]
    state = (tail[..., :cw],
             jnp.concatenate([pool_hist, p_s], axis=1)[:, -POOL_HIST:],
             u_tail,
             jnp.concatenate([conv_hist, u_s], axis=1)[:, -(CONV_K - 1):])
    return x, state


def kernel(x_prompt, x_sample, cache_nsa, cache_diff, state_nsa_win, state_pool, state_conv, page_table,
           att_norm_g, att_w_in, att_w_out, nsa_q_gain, nsa_k_gain, nsa_cmp_w, diff_q_gain, diff_k_gain,
           diff_lambda, diff_out_gain, pc_norm_g, pc_w_in, pc_w_out, pool_w, pool_scale, conv_w,
           ffn_norm_g, peer_wq, peer_keys, peer_u, peer_v):
    batch, seq, d = x_prompt.shape
    nb, ts, _ = x_sample.shape
    n_prompt = batch * seq
    n = n_prompt + nb * ts
    depth = ffn_norm_g.shape[0]
    past = page_table.shape[1] * cache_nsa.shape[2]
    tm = n // 8
    assert n % 8 == 0 and tm % 16 == 0
    x = jnp.concatenate([x_prompt.reshape(n_prompt, d), x_sample.reshape(nb * ts, d)], axis=0)
    u_bf, v_bf = peer_u.astype(BF), peer_v.astype(BF)
    att_states, pc_states = [], []
    for li in range(depth):
        i = li // 2
        if li % 2 == 0:
            x, st = _attention_layer(
                x, li, n_prompt, batch, seq, nb, ts, cache_nsa[i], cache_diff[i], state_nsa_win[i], page_table,
                att_norm_g[i], att_w_in[i], att_w_out[i], nsa_q_gain[i], nsa_k_gain[i], nsa_cmp_w[i],
                diff_q_gain[i], diff_k_gain[i], diff_lambda[i], diff_out_gain[i], tm=tm)
            att_states.append(st)
        else:
            x, st = _pool_conv_layer(
                x, n_prompt, batch, seq, nb, ts, past, state_pool[i], state_conv[i],
                pc_norm_g[i], pc_w_in[i], pc_w_out[i], pool_w[i], pool_scale[i], conv_w[i], tm=tm)
            pc_states.append(st)
        x = _peer_ffn(x, ffn_norm_g[li], peer_wq, peer_keys, u_bf, v_bf, li, tm=tm)
    stack = lambda states, k: jnp.stack([s[k] for s in states])
    return (x[:n_prompt].reshape(batch, seq, d), x[n_prompt:].reshape(nb, ts, d),
            stack(att_states, 0), stack(att_states, 1), stack(att_states, 2), stack(att_states, 3),
            stack(att_states, 4), stack(att_states, 5),
            stack(pc_states, 0), stack(pc_states, 1), stack(pc_states, 2), stack(pc_states, 3))
```

```python
import functools
import math

import jax
import jax.numpy as jnp
from jax import lax
from jax.experimental import pallas as pl
from jax.experimental.pallas import tpu as pltpu

F32 = jnp.float32
BF = jnp.bfloat16
I32 = jnp.int32

HEAD_DIM = 128
NSA_HEADS = 8
NSA_KV_HEADS = 2
NSA_HPG = NSA_HEADS // NSA_KV_HEADS
L_CMP = 32
L_SEL = 64
N_SEL = 16
WINDOW = 512
FORCE_BONUS = 1e4
DIFF_HEADS = 8
DIFF_DK = HEAD_DIM // 2
POOL_WINDOWS = (2, 4, 8, 16)
POOL_HIST = max(POOL_WINDOWS) - 1
CONV_K = 3
PEER_HEADS = 8
PEER_NKEYS = 128
PEER_TOPK = 16
EPS = 1e-6
NEG = -1e30
LANES = 128
HALO = 16
VMEM_LIMIT = 56 * 1024 * 1024

_NT = (((1,), (1,)), ((), ()))


def _params(*sem):
    return pltpu.CompilerParams(dimension_semantics=sem, vmem_limit_bytes=VMEM_LIMIT)


def _dot_nt(a, b):
    return lax.dot_general(a, b, _NT, preferred_element_type=F32)


def _dot(a, b):
    return jnp.dot(a, b, preferred_element_type=F32)


def _pow2_neg(k):
    return lax.bitcast_convert_type((127 - k) << 23, F32)


def _rms(x, g):
    return x * lax.rsqrt(jnp.mean(x * x, axis=-1, keepdims=True) + EPS) * g


def _rms_halves(x, g):
    lo = lax.broadcasted_iota(I32, x.shape, 1) < DIFF_DK
    x2 = x * x
    s_lo = jnp.sum(jnp.where(lo, x2, 0.0), axis=-1, keepdims=True)
    s_hi = jnp.sum(jnp.where(lo, 0.0, x2), axis=-1, keepdims=True)
    ms = jnp.where(lo, s_lo, s_hi) * (1.0 / DIFF_DK)
    return x * lax.rsqrt(ms + EPS) * g


def _msoftmax(s, mask):
    sm = jnp.where(mask, s, NEG)
    m = jnp.max(sm, axis=-1, keepdims=True)
    e = jnp.where(mask, jnp.exp(sm - m), 0.0)
    d = jnp.sum(e, axis=-1, keepdims=True)
    return e / jnp.where(d > 0.0, d, 1.0)


def _msoftmax2(s1, m1, s2, m2):
    a1 = jnp.where(m1, s1, NEG)
    a2 = jnp.where(m2, s2, NEG)
    m = jnp.maximum(jnp.max(a1, axis=-1, keepdims=True), jnp.max(a2, axis=-1, keepdims=True))
    e1 = jnp.where(m1, jnp.exp(a1 - m), 0.0)
    e2 = jnp.where(m2, jnp.exp(a2 - m), 0.0)
    d = jnp.sum(e1, axis=-1, keepdims=True) + jnp.sum(e2, axis=-1, keepdims=True)
    inv = 1.0 / jnp.where(d > 0.0, d, 1.0)
    return e1 * inv, e2 * inv


def _mm_body(*refs, norm, has_res):
    if norm:
        x_ref, g_ref, w_ref = refs[:3]
        rest = refs[3:]
    else:
        x_ref, w_ref = refs[:2]
        rest = refs[2:]
    if has_res:
        r_ref, o_ref, xb_ref = rest
    else:
        o_ref, xb_ref = rest

    @pl.when(pl.program_id(1) == 0)
    def _():
        x = x_ref[...]
        if norm:
            x = _rms(x.astype(F32), g_ref[...])
        xb_ref[...] = x.astype(BF)

    y = _dot(xb_ref[...], w_ref[...])
    if has_res:
        y = y + r_ref[...]
    o_ref[...] = y.astype(o_ref.dtype)


def _mm(x, w, *, gain=None, res=None, tm, tn, name):
    m, k = x.shape
    n = w.shape[1]
    assert m % tm == 0 and n % tn == 0
    norm = gain is not None
    in_specs = [pl.BlockSpec((tm, k), lambda i, j: (i, 0))]
    args = [x]
    if norm:
        in_specs.append(pl.BlockSpec((1, k), lambda i, j: (0, 0)))
        args.append(gain.reshape(1, k))
    in_specs.append(pl.BlockSpec((k, tn), lambda i, j: (0, j)))
    args.append(w.astype(BF))
    if res is not None:
        in_specs.append(pl.BlockSpec((tm, tn), lambda i, j: (i, j)))
        args.append(res)
    return pl.pallas_call(
        functools.partial(_mm_body, norm=norm, has_res=res is not None),
        grid=(m // tm, n // tn),
        in_specs=in_specs,
        out_specs=pl.BlockSpec((tm, tn), lambda i, j: (i, j)),
        out_shape=jax.ShapeDtypeStruct((m, n), F32),
        scratch_shapes=[pltpu.VMEM((tm, k), BF)],
        compiler_params=_params("parallel", "arbitrary"),
        name=name,
    )(*args)


_Q0 = 0
_KV0 = NSA_HEADS * HEAD_DIM
_DQ0 = _KV0 + 6 * NSA_KV_HEADS * HEAD_DIM
_DK0 = _DQ0 + DIFF_HEADS * HEAD_DIM
_DV0 = _DK0 + DIFF_HEADS * HEAD_DIM
_GT0 = _DV0 + DIFF_HEADS * HEAD_DIM
_ZW = _GT0 + NSA_KV_HEADS * LANES


def _att_post_body(z_ref, qg_ref, kg_ref, dqg_ref, dkg_ref,
                   qn_ref, rows_ref, win_ref, gate_ref, dqn_ref, drows_ref):
    hd = HEAD_DIM
    qg = qg_ref[...]
    for h in range(NSA_HEADS):
        x = z_ref[:, _Q0 + h * hd:_Q0 + (h + 1) * hd]
        qn_ref[:, h * hd:(h + 1) * hd] = (_rms(x, qg) * (HEAD_DIM ** -0.5)).astype(qn_ref.dtype)
    gw = NSA_KV_HEADS * hd
    for br in range(3):
        for g in range(NSA_KV_HEADS):
            ko = _KV0 + br * 2 * gw + g * hd
            k = _rms(z_ref[:, ko:ko + hd], kg_ref[br:br + 1, :])
            v = z_ref[:, ko + gw:ko + gw + hd]
            if br < 2:
                rows_ref[:, br * 2 * gw + g * hd:br * 2 * gw + (g + 1) * hd] = k
                rows_ref[:, br * 2 * gw + gw + g * hd:br * 2 * gw + gw + (g + 1) * hd] = v
            else:
                win_ref[:, g * hd:(g + 1) * hd] = k
                win_ref[:, gw + g * hd:gw + (g + 1) * hd] = v
    gate_ref[...] = jax.nn.sigmoid(z_ref[:, _GT0:_ZW])
    dqg = dqg_ref[...]
    dkg = dkg_ref[...]
    for h in range(DIFF_HEADS):
        x = z_ref[:, _DQ0 + h * hd:_DQ0 + (h + 1) * hd]
        dqn_ref[:, h * hd:(h + 1) * hd] = (_rms_halves(x, dqg) * (DIFF_DK ** -0.5)).astype(dqn_ref.dtype)
        x = z_ref[:, _DK0 + h * hd:_DK0 + (h + 1) * hd]
        drows_ref[:, h * hd:(h + 1) * hd] = _rms_halves(x, dkg)
    dvw = DIFF_HEADS * hd
    drows_ref[:, dvw:2 * dvw] = z_ref[:, _DV0:_DV0 + dvw]


def _att_post(z, q_gain, k_gain, dq_gain, dk_gain, *, tm):
    n = z.shape[0]
    assert n % tm == 0 and z.shape[1] == _ZW
    hd = HEAD_DIM
    widths = (NSA_HEADS * hd, 4 * NSA_KV_HEADS * hd, 2 * NSA_KV_HEADS * hd, NSA_KV_HEADS * LANES,
              DIFF_HEADS * hd, 2 * DIFF_HEADS * hd)
    dtypes = (BF, F32, F32, F32, BF, F32)
    small = lambda r: pl.BlockSpec((r, hd), lambda i: (0, 0))
    return pl.pallas_call(
        _att_post_body,
        grid=(n // tm,),
        in_specs=[pl.BlockSpec((tm, _ZW), lambda i: (i, 0)), small(1), small(3), small(1), small(1)],
        out_specs=[pl.BlockSpec((tm, w), lambda i: (i, 0)) for w in widths],
        out_shape=[jax.ShapeDtypeStruct((n, w), d) for w, d in zip(widths, dtypes)],
        compiler_params=_params("parallel"),
        name="att_post",
    )(z, q_gain.reshape(1, hd), k_gain, dq_gain.reshape(1, hd), dk_gain.reshape(1, hd))


def _compress(x, w):
    length, c = x.shape
    npair = length // (2 * L_CMP)
    x3 = x.reshape(npair, 2 * L_CMP, c)
    even = jnp.sum(x3[:, :L_CMP, :] * w, axis=1)
    odd = jnp.sum(x3[:, L_CMP:, :] * w, axis=1)
    pad = jnp.zeros((LANES - 2 * npair, c), F32)
    return jnp.concatenate([even, odd, pad], axis=0)


def _nsa_cmp_body(x_ref, w_ref, o_ref):
    o_ref[...] = _compress(x_ref[...], w_ref[...])


def _nsa_cmp(rows, cmp_w, *, batch, seq):
    gw = NSA_KV_HEADS * HEAD_DIM
    assert seq % (2 * L_CMP) == 0 and seq // L_CMP <= LANES
    w = jnp.concatenate([jnp.tile(cmp_w[0], (1, NSA_KV_HEADS)), jnp.tile(cmp_w[1], (1, NSA_KV_HEADS))], axis=1)
    return pl.pallas_call(
        _nsa_cmp_body,
        grid=(batch,),
        in_specs=[pl.BlockSpec((seq, 2 * gw), lambda b: (b, 0)),
                  pl.BlockSpec((L_CMP, 2 * gw), lambda b: (0, 0))],
        out_specs=pl.BlockSpec((None, LANES, 2 * gw), lambda b: (b, 0, 0)),
        out_shape=jax.ShapeDtypeStruct((batch, LANES, 2 * gw), F32),
        compiler_params=_params("parallel"),
        name="nsa_cmp",
    )(rows, w)


def _cmp_positions(ncmp):
    c = lax.broadcasted_iota(I32, (1, LANES), 1)
    half = ncmp // 2
    blk = jnp.where(c < half, 2 * c, 2 * (c - half) + 1)
    return c, blk * L_CMP + (L_CMP - 1), c < ncmp


def _select_blocks(p_heads, t1, ncmp, nsel_blocks):
    c = lax.broadcasted_iota(I32, (1, LANES), 1)
    half = ncmp // 2
    imp = p_heads[0]
    for p in p_heads[1:]:
        imp = imp + p
    imp = jnp.where(c < half, imp + pltpu.roll(imp, LANES - half, 1), 0.0)
    cur = t1 // L_SEL
    forced = (c == 0) | (c == cur) | (c == cur - 1)
    valid = (c * L_SEL <= t1) & (c < nsel_blocks)
    score = jnp.where(valid, imp + jnp.where(forced, FORCE_BONUS, 0.0), NEG)
    rank = jnp.zeros(score.shape, I32)
    for jp in range(nsel_blocks):
        sj = score[:, jp:jp + 1]
        beats = (sj > score) | ((sj == score) & (c > jp))
        rank = rank + beats.astype(I32)
    return (rank < min(N_SEL, nsel_blocks)) & (c < nsel_blocks)


def _expand_blocks(sel01, length):
    j = lax.broadcasted_iota(I32, (LANES, length), 0)
    l = lax.broadcasted_iota(I32, (LANES, length), 1)
    e = jnp.where(l // L_SEL == j, 1.0, 0.0).astype(BF)
    return _dot(sel01.astype(BF), e)


CAUSAL_STEP = 512


def _causal_variants(need, seq, fn, out_ref):
    step = min(CAUSAL_STEP, seq)
    assert seq % step == 0
    for klen in range(step, seq + 1, step):
        @pl.when((need <= klen) & (need > klen - step))
        def _(klen=klen):
            out_ref[...] = fn(klen)


def _nsa_prompt_body(q_ref, gate_ref, kc_ref, vc_ref, ks_ref, vs_ref, kw_ref, vw_ref, o_ref, os_ref, *, tq, seq):
    g = pl.program_id(1)
    t0 = pl.program_id(2) * tq
    hg = NSA_HPG
    hd = HEAD_DIM
    rows = hg * tq
    q = q_ref[...]
    qs = jnp.concatenate([q[:, h * hd:(h + 1) * hd] for h in range(hg)], axis=0)
    r_io = lax.broadcasted_iota(I32, (rows, 1), 0)
    hrow = r_io // tq
    tpos = t0 + (r_io - hrow * tq)
    slope = _pow2_neg(g * hg + hrow + 1)
    t1 = t0 + lax.broadcasted_iota(I32, (tq, 1), 0)
    ncmp = seq // L_CMP
    nblk = -(-seq // L_SEL)

    _, pc, cvalid = _cmp_positions(ncmp)
    s = _dot_nt(qs, kc_ref[...].astype(BF)) - slope * (tpos - pc).astype(F32)
    p = _msoftmax(s, cvalid & (pc <= tpos))
    o_c = _dot(p.astype(BF), vc_ref[...].astype(BF))
    sel = _select_blocks([p[h * tq:(h + 1) * tq] for h in range(hg)], t1, ncmp, nblk)
    sel01 = jnp.where(sel, 1.0, 0.0)

    head_slope = [_pow2_neg(jnp.full((1, 1), g * hg + h + 1, I32)) for h in range(hg)]

    def attend(k, v, pos, ok):
        s = _dot_nt(qs, k)
        posf = pos.astype(F32)
        masked = jnp.where(ok, 0.0, NEG)
        ps = []
        for h in range(hg):
            logit = s[h * tq:(h + 1) * tq] + (masked + head_slope[h] * posf)
            e = jnp.exp(logit - jnp.max(logit, axis=-1, keepdims=True))
            ps.append((e * (1.0 / jnp.sum(e, axis=-1, keepdims=True))).astype(BF))
        return _dot(jnp.concatenate(ps, axis=0), v)

    def selected(klen):
        pos = lax.broadcasted_iota(I32, (1, klen), 1)
        ok = (pos <= t1) & (_expand_blocks(sel01, klen) > 0.5)
        return attend(ks_ref[0:klen, :].astype(BF), vs_ref[0:klen, :].astype(BF), pos, ok)

    _causal_variants(t0 + tq, seq, selected, os_ref)
    o_s = os_ref[...]

    wl = WINDOW + tq
    start = pl.multiple_of(jnp.maximum(t0 - WINDOW, 0), tq)
    posw = start + lax.broadcasted_iota(I32, (1, wl), 1)
    o_w = attend(kw_ref[pl.ds(start, wl), :].astype(BF), vw_ref[pl.ds(start, wl), :].astype(BF), posw,
                 (posw <= t1) & (posw > t1 - WINDOW))

    gt = gate_ref[...]
    for h in range(hg):
        rs = slice(h * tq, (h + 1) * tq)
        o = gt[:, 3 * h:3 * h + 1] * o_c[rs] + gt[:, 3 * h + 1:3 * h + 2] * o_s[rs] + gt[:, 3 * h + 2:3 * h + 3] * o_w[rs]
        o_ref[:, h * hd:(h + 1) * hd] = o.astype(o_ref.dtype)


def _nsa_prompt(qn, gates, cmp, rows, win, *, batch, seq, tq):
    hd = HEAD_DIM
    ng = NSA_KV_HEADS
    nq = seq // tq
    assert seq % tq == 0 and tq % LANES == 0 and WINDOW % tq == 0 and seq >= WINDOW + tq
    gq = NSA_HPG * hd
    seq_blk = lambda col: pl.BlockSpec((seq, hd), lambda b, g, i, col=col: (b, col + g))
    return pl.pallas_call(
        functools.partial(_nsa_prompt_body, tq=tq, seq=seq),
        grid=(batch, ng, nq),
        in_specs=[
            pl.BlockSpec((tq, gq), lambda b, g, i: (b * nq + i, g)),
            pl.BlockSpec((tq, LANES), lambda b, g, i: (b * nq + i, g)),
            pl.BlockSpec((None, LANES, hd), lambda b, g, i: (b, 0, g)),
            pl.BlockSpec((None, LANES, hd), lambda b, g, i: (b, 0, ng + g)),
            seq_blk(2 * ng), seq_blk(3 * ng),
            seq_blk(0), seq_blk(ng),
        ],
        out_specs=pl.BlockSpec((tq, gq), lambda b, g, i: (b * nq + i, g)),
        out_shape=jax.ShapeDtypeStruct((batch * seq, ng * gq), BF),
        scratch_shapes=[pltpu.VMEM((NSA_HPG * tq, hd), F32)],
        compiler_params=_params("parallel", "parallel", "arbitrary"),
        name="nsa_prompt",
    )(qn, gates, cmp, cmp, rows, rows, win, win)


def _lambda(lv, lam_init):
    a = jnp.sum(lv[0:1, :] * lv[1:2, :], axis=-1, keepdims=True)
    b = jnp.sum(lv[2:3, :] * lv[3:4, :], axis=-1, keepdims=True)
    return jnp.exp(a) - jnp.exp(b) + lam_init


def _diff_prompt_body(lv_ref, og_ref, q_ref, k_ref, v_ref, o_ref, acc_ref, *, tq, seq, lam_init):
    h = pl.program_id(1)
    t0 = pl.program_id(2) * tq
    q = q_ref[...].astype(F32)
    lo = lax.broadcasted_iota(I32, q.shape, 1) < DIFF_DK
    qs = jnp.concatenate([jnp.where(lo, q, 0.0), jnp.where(lo, 0.0, q)], axis=0).astype(BF)
    t1 = t0 + lax.broadcasted_iota(I32, (tq, 1), 0)
    slope = _pow2_neg(jnp.full((1, 1), h + 1, I32))
    lam = _lambda(lv_ref[...], lam_init)

    def attend(klen):
        pos = lax.broadcasted_iota(I32, (1, klen), 1)
        bias = jnp.where(pos <= t1, slope * pos.astype(F32), NEG)
        s = _dot_nt(qs, k_ref[0:klen, :].astype(BF))
        parts = []
        for mp in range(2):
            logit = s[mp * tq:(mp + 1) * tq] + bias
            e = jnp.exp(logit - jnp.max(logit, axis=-1, keepdims=True))
            parts.append((e, 1.0 / jnp.sum(e, axis=-1, keepdims=True)))
        a = parts[0][0] * parts[0][1] - parts[1][0] * (lam * parts[1][1])
        return _dot(a.astype(BF), v_ref[0:klen, :].astype(BF))

    _causal_variants(t0 + tq, seq, attend, acc_ref)
    o_ref[...] = (_rms(acc_ref[...], og_ref[...]) * (1.0 - lam_init)).astype(o_ref.dtype)


def _diff_prompt(dqn, drows, lam_vec, out_gain, *, batch, seq, tq, lam_init):
    hd = HEAD_DIM
    nh = DIFF_HEADS
    nq = seq // tq
    assert seq % tq == 0
    return pl.pallas_call(
        functools.partial(_diff_prompt_body, tq=tq, seq=seq, lam_init=lam_init),
        grid=(batch, nh, nq),
        in_specs=[
            pl.BlockSpec((4, DIFF_DK), lambda b, h, i: (0, 0)),
            pl.BlockSpec((1, hd), lambda b, h, i: (0, 0)),
            pl.BlockSpec((tq, hd), lambda b, h, i: (b * nq + i, h)),
            pl.BlockSpec((seq, hd), lambda b, h, i: (b, h)),
            pl.BlockSpec((seq, hd), lambda b, h, i: (b, nh + h)),
        ],
        out_specs=pl.BlockSpec((tq, hd), lambda b, h, i: (b * nq + i, h)),
        out_shape=jax.ShapeDtypeStruct((batch * seq, nh * hd), BF),
        scratch_shapes=[pltpu.VMEM((tq, hd), F32)],
        compiler_params=_params("parallel", "parallel", "arbitrary"),
        name="diff_prompt",
    )(lam_vec, out_gain.reshape(1, hd), dqn, drows, drows)


def _page_copies(pt_ref, cache_ref, buf_ref, sem_ref, b, slot, start):
    npages = pt_ref.shape[1]
    page, nkind, nhead = cache_ref.shape[1:4]

    def one_page(p, c):
        phys = pt_ref[b, p]
        dst_rows = pl.ds(pl.multiple_of(p * page, page), page)
        for kind in range(nkind):
            for h in range(nhead):
                cp = pltpu.make_async_copy(cache_ref.at[phys, :, kind, h, :],
                                           buf_ref.at[slot, kind, h, dst_rows, :], sem_ref.at[slot])
                if start:
                    cp.start()
                else:
                    cp.wait()
        return c

    lax.fori_loop(0, npages, one_page, 0)


def _state_copies(state_ref, buf_ref, sem_ref, b, slot, start):
    nkind, nhead = state_ref.shape[2:4]
    for kind in range(nkind):
        for h in range(nhead):
            cp = pltpu.make_async_copy(state_ref.at[b, :, kind, h, :], buf_ref.at[slot, kind, h], sem_ref.at[slot])
            if start:
                cp.start()
            else:
                cp.wait()


def _double_buffered(fetch):
    b = pl.program_id(0)
    slot = b % 2

    @pl.when(b == 0)
    def _():
        fetch(b, slot, True)

    @pl.when(b + 1 < pl.num_programs(0))
    def _():
        fetch(b + 1, 1 - slot, True)

    fetch(b, slot, False)
    return slot


def _nsa_sample_body(pt_ref, q_ref, gate_ref, nrow_ref, nwin_ref, cw_ref, cache_ref, wstate_ref, o_ref,
                     kv_buf, w_buf, kv_sem, w_sem, *, past, ts):
    def fetch(b, slot, start):
        _page_copies(pt_ref, cache_ref, kv_buf, kv_sem, b, slot, start)
        _state_copies(wstate_ref, w_buf, w_sem, b, slot, start)

    slot = _double_buffered(fetch)
    past_ref = kv_buf.at[slot]
    wst_ref = w_buf.at[slot]
    hg = NSA_HPG
    hd = HEAD_DIM
    ng = NSA_KV_HEADS
    rows = q_ref.shape[1]
    tsp = rows // hg
    npad = nrow_ref.shape[0]
    gw = ng * hd
    ncmp = (past + ts) // L_CMP
    nblk = -(-(past + ts) // L_SEL)
    new_blk = past // L_SEL
    r_io = lax.broadcasted_iota(I32, (rows, 1), 0)
    hrow = r_io // tsp
    trow = r_io - hrow * tsp
    tpos = past + trow
    t1 = past + lax.broadcasted_iota(I32, (tsp, 1), 0)
    jn = lax.broadcasted_iota(I32, (1, npad), 1)
    new_ok = (jn <= trow) & (jn < ts)
    dist_new = (trow - jn).astype(F32)
    _, pc, cvalid = _cmp_positions(ncmp)
    pos = lax.broadcasted_iota(I32, (1, past), 1)
    nwst = wst_ref.shape[2]
    posw = past - nwst + lax.broadcasted_iota(I32, (1, nwst), 1)
    for g in range(ng):
        qs = q_ref[g]
        slope = _pow2_neg(g * hg + hrow + 1)
        kc = _compress(past_ref[0, g], cw_ref[0]).astype(BF)
        vc = _compress(past_ref[1, g], cw_ref[1]).astype(BF)
        s = _dot_nt(qs, kc) - slope * (tpos - pc).astype(F32)
        p = _msoftmax(s, cvalid & (pc <= tpos))
        o_c = _dot(p.astype(BF), vc)
        sel = _select_blocks([p[h * tsp:(h + 1) * tsp] for h in range(hg)], t1, ncmp, nblk)
        sel01 = jnp.concatenate([jnp.where(sel, 1.0, 0.0)] * hg, axis=0)
        selm = _expand_blocks(sel01, past)
        sel_new = sel01[:, new_blk:new_blk + 1] > 0.5
        kn = nrow_ref[:, 2 * gw + g * hd:2 * gw + (g + 1) * hd].astype(BF)
        vn = nrow_ref[:, 3 * gw + g * hd:3 * gw + (g + 1) * hd].astype(BF)
        s1 = _dot_nt(qs, past_ref[2, g].astype(BF)) - slope * (tpos - pos).astype(F32)
        s2 = _dot_nt(qs, kn) - slope * dist_new
        p1, p2 = _msoftmax2(s1, selm > 0.5, s2, new_ok & sel_new)
        o_s = _dot(p1.astype(BF), past_ref[3, g].astype(BF)) + _dot(p2.astype(BF), vn)
        kn = nwin_ref[:, g * hd:(g + 1) * hd].astype(BF)
        vn = nwin_ref[:, gw + g * hd:gw + (g + 1) * hd].astype(BF)
        dist = tpos - posw
        s1 = _dot_nt(qs, wst_ref[0, g].astype(BF)) - slope * dist.astype(F32)
        s2 = _dot_nt(qs, kn) - slope * dist_new
        p1, p2 = _msoftmax2(s1, (dist >= 0) & (dist < WINDOW), s2, new_ok)
        o_w = _dot(p1.astype(BF), wst_ref[1, g].astype(BF)) + _dot(p2.astype(BF), vn)
        gt = gate_ref[g]
        o_ref[g] = gt[:, 0:1] * o_c + gt[:, 1:2] * o_s + gt[:, 2:3] * o_w


def _nsa_sample(q, gates, cache, page_table, win_state, new_rows, new_win, cmp_w, *, ts):
    nb, ng, rows, hd = q.shape
    past = page_table.shape[1] * cache.shape[1]
    nwst = win_state.shape[1]
    npad = new_rows.shape[1]
    assert past % (2 * L_CMP) == 0 and past % L_SEL == 0 and nwst == WINDOW and (past + ts) // L_CMP == past // L_CMP
    assert cache.shape[2:] == (4, ng, hd) and win_state.shape[2:] == (2, ng, hd)
    return pl.pallas_call(
        functools.partial(_nsa_sample_body, past=past, ts=ts),
        grid_spec=pltpu.PrefetchScalarGridSpec(
            num_scalar_prefetch=1,
            grid=(nb,),
            in_specs=[
                pl.BlockSpec((None, ng, rows, hd), lambda b, pt: (b, 0, 0, 0)),
                pl.BlockSpec((None, ng, rows, LANES), lambda b, pt: (b, 0, 0, 0)),
                pl.BlockSpec((None, npad, new_rows.shape[2]), lambda b, pt: (b, 0, 0)),
                pl.BlockSpec((None, npad, new_win.shape[2]), lambda b, pt: (b, 0, 0)),
                pl.BlockSpec((2, L_CMP, hd), lambda b, pt: (0, 0, 0)),
                pl.BlockSpec(memory_space=pl.ANY),
                pl.BlockSpec(memory_space=pl.ANY),
            ],
            out_specs=pl.BlockSpec((None, ng, rows, hd), lambda b, pt: (b, 0, 0, 0)),
            scratch_shapes=[pltpu.VMEM((2, 4, ng, past, hd), F32), pltpu.VMEM((2, 2, ng, nwst, hd), F32),
                            pltpu.SemaphoreType.DMA((2,)), pltpu.SemaphoreType.DMA((2,))],
        ),
        out_shape=jax.ShapeDtypeStruct((nb, ng, rows, hd), F32),
        compiler_params=_params("arbitrary"),
        name="nsa_sample",
    )(page_table, q, gates, new_rows, new_win, cmp_w, cache, win_state)


def _diff_sample_body(pt_ref, lv_ref, og_ref, q_ref, ndrow_ref, cache_ref, o_ref, kv_buf, kv_sem, *, past, ts, lam_init):
    slot = _double_buffered(functools.partial(_page_copies, pt_ref, cache_ref, kv_buf, kv_sem))
    past_ref = kv_buf.at[slot]
    hd = HEAD_DIM
    nh = DIFF_HEADS
    npad = ndrow_ref.shape[0]
    rows = q_ref.shape[1]
    tsp = rows // 2
    r_io = lax.broadcasted_iota(I32, (rows, 1), 0)
    trow = jnp.where(r_io >= tsp, r_io - tsp, r_io)
    tpos = past + trow
    dist = (tpos - lax.broadcasted_iota(I32, (1, past), 1)).astype(F32)
    jn = lax.broadcasted_iota(I32, (1, npad), 1)
    new_ok = (jn <= trow) & (jn < ts)
    dist_new = (trow - jn).astype(F32)
    lam = _lambda(lv_ref[...], lam_init)
    og = og_ref[...]
    for h in range(nh):
        slope = 2.0 ** -(h + 1)
        qs = q_ref[h]
        kn = ndrow_ref[:, h * hd:(h + 1) * hd].astype(BF)
        vn = ndrow_ref[:, (nh + h) * hd:(nh + h + 1) * hd].astype(BF)
        s1 = _dot_nt(qs, past_ref[0, h].astype(BF)) - slope * dist
        s2 = _dot_nt(qs, kn) - slope * dist_new
        p1, p2 = _msoftmax2(s1, dist >= 0.0, s2, new_ok)
        a1 = p1 - lam * jnp.concatenate([p1[tsp:], p1[:tsp]], axis=0)
        a2 = p2 - lam * jnp.concatenate([p2[tsp:], p2[:tsp]], axis=0)
        o = _dot(a1.astype(BF), past_ref[1, h].astype(BF)) + _dot(a2.astype(BF), vn)
        o_ref[h] = _rms(o, og) * (1.0 - lam_init)


def _diff_sample(q, cache, page_table, new_drows, lam_vec, out_gain, *, ts, lam_init):
    nb, nh, rows, hd = q.shape
    past = page_table.shape[1] * cache.shape[1]
    npad = new_drows.shape[1]
    assert cache.shape[2:] == (2, nh, hd)
    return pl.pallas_call(
        functools.partial(_diff_sample_body, past=past, ts=ts, lam_init=lam_init),
        grid_spec=pltpu.PrefetchScalarGridSpec(
            num_scalar_prefetch=1,
            grid=(nb,),
            in_specs=[
                pl.BlockSpec((4, DIFF_DK), lambda b, pt: (0, 0)),
                pl.BlockSpec((1, hd), lambda b, pt: (0, 0)),
                pl.BlockSpec((None, nh, rows, hd), lambda b, pt: (b, 0, 0, 0)),
                pl.BlockSpec((None, npad, new_drows.shape[2]), lambda b, pt: (b, 0, 0)),
                pl.BlockSpec(memory_space=pl.ANY),
            ],
            out_specs=pl.BlockSpec((None, nh, rows, hd), lambda b, pt: (b, 0, 0, 0)),
            scratch_shapes=[pltpu.VMEM((2, 2, nh, past, hd), F32), pltpu.SemaphoreType.DMA((2,))],
        ),
        out_shape=jax.ShapeDtypeStruct((nb, nh, rows, hd), F32),
        compiler_params=_params("arbitrary"),
        name="diff_sample",
    )(page_table, lam_vec, out_gain.reshape(1, hd), q, new_drows, cache)


def _rows(a, lo, n):
    return lax.slice_in_dim(a, lo, lo + n, axis=a.ndim - 2)


def _pool_conv(p_ext, u_ext, b_gate, cnt_pos, pool_w_ref, pool_scale, conv_w, ts):
    gwid = p_ext.shape[-1] // len(POOL_WINDOWS)
    lead = p_ext.shape[:-2]
    outs = []
    for gi, w in enumerate(POOL_WINDOWS):
        x = p_ext[..., gi * gwid:(gi + 1) * gwid]
        acc = x
        span = 1
        while span < w:
            n = acc.shape[-2]
            acc = _rows(acc, span, n - span) + _rows(acc, 0, n - span)
            span *= 2
        win_sum = _rows(acc, HALO - (w - 1), ts)
        cnt = jnp.minimum(w, cnt_pos + 1).astype(F32)
        m = win_sum / cnt - _rows(x, HALO, ts)
        m2 = m.reshape((-1, gwid)).astype(BF)
        outs.append(_dot(m2, pool_w_ref[gi].astype(BF)).reshape(lead + (ts, gwid)))
    y_pool = jnp.concatenate(outs, axis=-1) * pool_scale
    conv = None
    for j in range(CONV_K):
        term = _rows(u_ext, HALO - (CONV_K - 1) + j, ts) * conv_w[j:j + 1, :]
        conv = term if conv is None else conv + term
    return jnp.concatenate([y_pool, b_gate * conv], axis=-1).astype(BF)


def _pc_prompt_body(zc_ref, zh_ref, pw_ref, ps_ref, cw_ref, o_ref, *, ts, cw):
    i = pl.program_id(1)
    keep = i > 0
    p_ext = jnp.concatenate([jnp.where(keep, zh_ref[:, 0:cw], 0.0), zc_ref[:, 0:cw]], axis=0)
    u_h = jnp.where(keep, zh_ref[:, 2 * cw:3 * cw] * zh_ref[:, 3 * cw:4 * cw], 0.0)
    u_ext = jnp.concatenate([u_h, zc_ref[:, 2 * cw:3 * cw] * zc_ref[:, 3 * cw:4 * cw]], axis=0)
    cnt_pos = i * ts + lax.broadcasted_iota(I32, (ts, 1), 0)
    o_ref[...] = _pool_conv(p_ext, u_ext, zc_ref[:, cw:2 * cw], cnt_pos, pw_ref, ps_ref[...], cw_ref[...], ts)


def _pc_prompt(z, pool_w, pool_scale, conv_w, *, batch, seq, ts):
    cw = pool_scale.shape[0]
    nt = seq // ts
    assert seq % ts == 0 and ts % HALO == 0 and z.shape[1] == 4 * cw
    hb = ts // HALO
    return pl.pallas_call(
        functools.partial(_pc_prompt_body, ts=ts, cw=cw),
        grid=(batch, nt),
        in_specs=[
            pl.BlockSpec((ts, 4 * cw), lambda b, i: (b * nt + i, 0)),
            pl.BlockSpec((HALO, 4 * cw), lambda b, i: (jnp.maximum((b * nt + i) * hb - 1, 0), 0)),
            pl.BlockSpec(pool_w.shape, lambda b, i: (0, 0, 0)),
            pl.BlockSpec((1, cw), lambda b, i: (0, 0)),
            pl.BlockSpec((CONV_K, cw), lambda b, i: (0, 0)),
        ],
        out_specs=pl.BlockSpec((ts, 2 * cw), lambda b, i: (b * nt + i, 0)),
        out_shape=jax.ShapeDtypeStruct((batch * seq, 2 * cw), BF),
        compiler_params=_params("parallel", "arbitrary"),
        name="pool_conv_prompt",
    )(z, z, pool_w, pool_scale.reshape(1, cw), conv_w)


def _pc_sample_body(pe_ref, ue_ref, bg_ref, pw_ref, ps_ref, cw_ref, o_ref, *, ts, past):
    cnt_pos = past + lax.broadcasted_iota(I32, (ts, 1), 0)
    o_ref[...] = _pool_conv(pe_ref[...], ue_ref[...], bg_ref[...], cnt_pos, pw_ref, ps_ref[...], cw_ref[...], ts)


def _pc_sample(p_ext, u_ext, b_gate, pool_w, pool_scale, conv_w, *, past, tb):
    nb, ext, cw = p_ext.shape
    ts = ext - HALO
    assert nb % tb == 0
    blk = lambda r: pl.BlockSpec((tb, r, cw), lambda b: (b, 0, 0))
    return pl.pallas_call(
        functools.partial(_pc_sample_body, ts=ts, past=past),
        grid=(nb // tb,),
        in_specs=[blk(ext), blk(ext), blk(ts),
                  pl.BlockSpec(pool_w.shape, lambda b: (0, 0, 0)),
                  pl.BlockSpec((1, cw), lambda b: (0, 0)),
                  pl.BlockSpec((CONV_K, cw), lambda b: (0, 0))],
        out_specs=pl.BlockSpec((tb, ts, 2 * cw), lambda b: (b, 0, 0)),
        out_shape=jax.ShapeDtypeStruct((nb, ts, 2 * cw), BF),
        compiler_params=_params("parallel"),
        name="pool_conv_sample",
    )(p_ext, u_ext, b_gate, pool_w, pool_scale.reshape(1, cw), conv_w)


def _peer_score_body(x_ref, g_ref, wq_ref, keys_ref, s_ref, xn_ref):
    @pl.when(pl.program_id(1) == 0)
    def _():
        xn_ref[...] = _rms(x_ref[...], g_ref[...]).astype(xn_ref.dtype)

    z = _dot(xn_ref[...], wq_ref[...]).astype(BF)
    nk = keys_ref.shape[1]
    dk = keys_ref.shape[2]
    for r in range(keys_ref.shape[0]):
        s_ref[r * nk:(r + 1) * nk, :] = _dot_nt(keys_ref[r], z[:, r * dk:(r + 1) * dk])


def _peer_scores(x, gain, wq, keys, li, *, tm):
    n, d = x.shape
    nsub, nk, dk = keys.shape[1:]
    per = 4
    assert n % tm == 0 and tm % LANES == 0 and nsub % per == 0 and wq.shape[2] == nsub * dk
    return pl.pallas_call(
        _peer_score_body,
        grid=(n // tm, nsub // per),
        in_specs=[
            pl.BlockSpec((tm, d), lambda i, j: (i, 0)),
            pl.BlockSpec((1, d), lambda i, j: (0, 0)),
            pl.BlockSpec((None, d, per * dk), lambda i, j: (li, 0, j)),
            pl.BlockSpec((None, per, nk, dk), lambda i, j: (li, j, 0, 0)),
        ],
        out_specs=[pl.BlockSpec((per * nk, tm), lambda i, j: (j, i)),
                   pl.BlockSpec((tm, d), lambda i, j: (i, 0))],
        out_shape=[jax.ShapeDtypeStruct((nsub * nk, n), F32), jax.ShapeDtypeStruct((n, d), BF)],
        compiler_params=_params("parallel", "arbitrary"),
        name="peer_scores",
    )(x, gain.reshape(1, d), wq.astype(BF), keys.astype(BF))


def _topk_rows(x, k, code):
    slot = lax.broadcasted_iota(I32, (k, x.shape[1]), 0)
    vals = jnp.zeros((k, x.shape[1]), F32)
    idxs = jnp.zeros((k, x.shape[1]), F32)
    for kk in range(k):
        m = jnp.max(x, axis=0, keepdims=True)
        idx = jnp.min(jnp.where(x == m, code, float(2 ** 24)), axis=0, keepdims=True)
        vals = jnp.where(slot == kk, m, vals)
        idxs = jnp.where(slot == kk, idx, idxs)
        x = jnp.where(code == idx, -jnp.inf, x)
    return vals, idxs.astype(I32)


def _pair_candidates(v0, v1):
    k = PEER_TOPK
    assert k == 16 and v0.shape[0] == k
    half = k // 2
    blocks = [v0[0:1] + v1] + [v0[a:a + 1] + v1[0:half] for a in range(1, half)] + [v0[half:] + v1[0:1]]
    r = lax.broadcasted_iota(I32, (k + (half - 1) * half + half, v0.shape[1]), 0)
    mid = r - k
    code = jnp.where(r < k, r,
                     jnp.where(mid < (half - 1) * half, (mid // half + 1) * k + mid % half,
                               (half + mid - (half - 1) * half) * k))
    return jnp.concatenate(blocks, axis=0), code.astype(F32)


def _peer_topk_body(s_ref, i1_ref, i2_ref, g_ref, sv_ref, si_ref, i1t_ref, i2t_ref, gt_ref):
    nsub = sv_ref.shape[0]
    k = PEER_TOPK
    nk = PEER_NKEYS

    def sub_key(gi, c):
        x = s_ref[pl.ds(pl.multiple_of(gi * nk, nk), nk), :]
        v, i = _topk_rows(x, k, lax.broadcasted_iota(I32, x.shape, 0).astype(F32))
        sv_ref[gi] = v
        si_ref[gi] = i
        return c

    lax.fori_loop(0, nsub, sub_key, 0)

    def head(h, c):
        comb, code = _pair_candidates(sv_ref[2 * h], sv_ref[2 * h + 1])
        cv, ci = _topk_rows(comb, k, code)
        a = ci // k
        b = ci - a * k
        s0 = si_ref[2 * h]
        s1 = si_ref[2 * h + 1]
        i1 = jnp.zeros(ci.shape, I32)
        i2 = jnp.zeros(ci.shape, I32)
        for q in range(k):
            i1 = jnp.where(a == q, s0[q:q + 1], i1)
            i2 = jnp.where(b == q, s1[q:q + 1], i2)
        e = jnp.exp(cv - cv[0:1])
        rows = pl.ds(pl.multiple_of(h * k, k), k)
        i1t_ref[rows, :] = i1
        i2t_ref[rows, :] = i2
        gt_ref[rows, :] = e / jnp.sum(e, axis=0, keepdims=True)
        return c

    lax.fori_loop(0, nsub // 2, head, 0)
    i1_ref[...] = i1t_ref[...].T
    i2_ref[...] = i2t_ref[...].T
    g_ref[...] = gt_ref[...].T


def _peer_topk(s_t, *, nh, tl):
    rows, n = s_t.shape
    npair = nh * PEER_TOPK
    assert rows == nh * 2 * PEER_NKEYS and n % tl == 0 and tl % LANES == 0
    out = pl.BlockSpec((tl, npair), lambda i: (i, 0))
    return pl.pallas_call(
        _peer_topk_body,
        grid=(n // tl,),
        in_specs=[pl.BlockSpec((rows, tl), lambda i: (0, i))],
        out_specs=[out, out, out],
        out_shape=[jax.ShapeDtypeStruct((n, npair), I32), jax.ShapeDtypeStruct((n, npair), I32),
                   jax.ShapeDtypeStruct((n, npair), F32)],
        scratch_shapes=[pltpu.VMEM((nh * 2, PEER_TOPK, tl), F32), pltpu.VMEM((nh * 2, PEER_TOPK, tl), I32),
                        pltpu.VMEM((npair, tl), I32), pltpu.VMEM((npair, tl), I32), pltpu.VMEM((npair, tl), F32)],
        compiler_params=_params("parallel"),
        name="peer_topk",
    )(s_t)


W_ROWS = 8


def _peer_w_body(a_ref, b_ref, g_ref, w_ref):
    tt = a_ref.shape[0]
    npair = a_ref.shape[2]
    io = lax.broadcasted_iota(I32, (tt, PEER_NKEYS, npair), 1)
    one_a = jnp.where(a_ref[...] == io, 1.0, 0.0).astype(BF)
    g = g_ref[...]
    g_hi = g.astype(BF).astype(F32)
    g_lo = g - g_hi
    hit = b_ref[...] == io
    dims = (((2,), (2,)), ((0,), (0,)))
    lhs = jnp.concatenate([one_a, one_a], axis=2)
    rhs = jnp.concatenate([jnp.where(hit, g_hi, 0.0).astype(BF), jnp.where(hit, g_lo, 0.0).astype(BF)], axis=2)
    w = lax.dot_general(lhs, rhs, dims, preferred_element_type=F32)
    nchunk = w_ref.shape[0]
    w4 = w.reshape(tt, nchunk, PEER_NKEYS // nchunk, PEER_NKEYS)
    for j in range(nchunk):
        w_ref[j] = w4[:, j]


def _peer_w(i1, i2, gate, *, tt):
    n, npair = i1.shape
    assert n % tt == 0
    blk = pl.BlockSpec((tt, 1, npair), lambda i: (i, 0, 0))
    return pl.pallas_call(
        _peer_w_body,
        grid=(n // tt,),
        in_specs=[blk, blk, blk],
        out_specs=pl.BlockSpec((PEER_NKEYS // W_ROWS, tt, W_ROWS, PEER_NKEYS), lambda i: (0, i, 0, 0)),
        out_shape=jax.ShapeDtypeStruct((PEER_NKEYS // W_ROWS, n, W_ROWS, PEER_NKEYS), F32),
        compiler_params=_params("parallel"),
        name="peer_w",
    )(i1.reshape(n, 1, npair), i2.reshape(n, 1, npair), gate.reshape(n, 1, npair))


def _peer_dense_body(xn_ref, xr_ref, u_ref, v_ref, w_ref, o_ref):
    @pl.when(pl.program_id(1) == 0)
    def _():
        o_ref[...] = xr_ref[...]

    tn = xn_ref.shape[0]
    nk = w_ref.shape[1]
    nw = w_ref.shape[0] // tn
    s = _dot_nt(xn_ref[...], u_ref[...])
    act = 0.5 * s * (1.0 + lax.erf(s * (2.0 ** -0.5)))
    wact = jnp.concatenate(
        [act[:, r * nk:(r + 1) * nk] * w_ref[pl.ds(r, tn, stride=nw), :] for r in range(nw)], axis=1)
    o_ref[...] += _dot(wact.astype(BF), v_ref[...])


def _peer_dense(xn, x_res, u_bf, v_bf, w, li, *, tn, ec):
    n, d = xn.shape
    ne = u_bf.shape[1]
    nchunk, _, nw, nk = w.shape
    assert n % tn == 0 and ne == nchunk * ec and ec == nw * nk
    w = w.reshape(nchunk, n * nw, nk)
    return pl.pallas_call(
        _peer_dense_body,
        grid=(n // tn, ne // ec),
        in_specs=[
            pl.BlockSpec((tn, d), lambda i, j: (i, 0)),
            pl.BlockSpec((tn, d), lambda i, j: (i, 0)),
            pl.BlockSpec((None, ec, d), lambda i, j: (li, j, 0)),
            pl.BlockSpec((None, ec, d), lambda i, j: (li, j, 0)),
            pl.BlockSpec((None, tn * nw, nk), lambda i, j: (j, i, 0)),
        ],
        out_specs=pl.BlockSpec((tn, d), lambda i, j: (i, 0)),
        out_shape=jax.ShapeDtypeStruct((n, d), F32),
        compiler_params=_params("parallel", "arbitrary"),
        name="peer_dense",
    )(xn, x_res, u_bf, v_bf, w)


def _peer_ffn(x, gain, wq, keys, u_bf, v_bf, li, *, tm):
    nl, nh = keys.shape[:2]
    s_t, xn = _peer_scores(x, gain, wq, keys.reshape(nl, nh * 2, PEER_NKEYS, keys.shape[-1]), li, tm=512)
    i1, i2, gate = _peer_topk(s_t, nh=nh, tl=512)
    w = _peer_w(i1, i2, gate, tt=64)
    return _peer_dense(xn, x, u_bf, v_bf, w, li, tn=tm // 2, ec=W_ROWS * PEER_NKEYS)


def _reorder_att_w_in(w_in):
    d = w_in.shape[0]
    o2 = _DQ0
    ngate = 3 * NSA_HPG
    gates = [jnp.pad(w_in[:, o2 + g * ngate:o2 + (g + 1) * ngate], ((0, 0), (0, LANES - ngate)))
             for g in range(NSA_KV_HEADS)]
    return jnp.concatenate([w_in[:, :o2], w_in[:, o2 + 3 * NSA_HEADS:]] + gates, axis=1)


def _attention_layer(x, li, n_prompt, batch, seq, nb, ts, cache_nsa_l, cache_diff_l, win_state_l, page_table,
                     norm_g, w_in, w_out, q_gain, k_gain, cmp_w, dq_gain, dk_gain, lam_vec, out_gain, *, tm):
    hd = HEAD_DIM
    ng = NSA_KV_HEADS
    hg = NSA_HPG
    nh = DIFF_HEADS
    lam_init = 0.8 - 0.6 * math.exp(-0.3 * li)

    z = _mm(x, _reorder_att_w_in(w_in), gain=norm_g, tm=tm, tn=256, name="att_in")
    qn, rows, win, gates, dqn, drows = _att_post(z, q_gain, k_gain, dq_gain, dk_gain, tm=272)

    cmp = _nsa_cmp(rows, cmp_w, batch=batch, seq=seq)
    o_nsa_p = _nsa_prompt(qn, gates, cmp, rows, win, batch=batch, seq=seq, tq=128)
    o_diff_p = _diff_prompt(dqn, drows, lam_vec, out_gain, batch=batch, seq=seq, tq=256, lam_init=lam_init)

    tsp = 8
    assert ts <= tsp
    pad_t = lambda a, ax: jnp.pad(a, [(0, tsp - ts) if k == ax else (0, 0) for k in range(a.ndim)])
    pad8 = lambda a: pad_t(a.reshape(nb, ts, a.shape[-1]), 1)
    rows_s, win_s, drows_s = rows[n_prompt:], win[n_prompt:], drows[n_prompt:]
    q_s = pad_t(qn[n_prompt:].reshape(nb, ts, ng, hg, hd).transpose(0, 2, 3, 1, 4), 3).reshape(nb, ng, hg * tsp, hd)
    g_s = gates[n_prompt:].reshape(nb, ts, ng, LANES)[..., :3 * hg].reshape(nb, ts, ng, hg, 3)
    g_s = pad_t(g_s.transpose(0, 2, 3, 1, 4), 3).reshape(nb, ng, hg * tsp, 3)
    g_s = jnp.pad(g_s, ((0, 0), (0, 0), (0, 0), (0, LANES - 3)))
    o_nsa_s = _nsa_sample(q_s, g_s, cache_nsa_l, page_table, win_state_l, pad8(rows_s), pad8(win_s), cmp_w, ts=ts)
    o_nsa_s = o_nsa_s.reshape(nb, ng, hg, tsp, hd)[:, :, :, :ts]
    o_nsa_s = o_nsa_s.transpose(0, 3, 1, 2, 4).reshape(nb * ts, ng * hg * hd)

    dq_s = pad_t(dqn[n_prompt:].reshape(nb, ts, nh, hd).transpose(0, 2, 1, 3), 2)
    lo = jnp.arange(hd) < DIFF_DK
    dq_s = jnp.concatenate([jnp.where(lo, dq_s, 0), jnp.where(lo, 0, dq_s)], axis=2).astype(BF)
    o_diff_s = _diff_sample(dq_s, cache_diff_l, page_table, pad8(drows_s), lam_vec, out_gain, ts=ts, lam_init=lam_init)
    o_diff_s = o_diff_s[:, :, :ts].transpose(0, 2, 1, 3).reshape(nb * ts, nh * hd)

    mix = jnp.concatenate([jnp.concatenate([o_nsa_p, o_diff_p], axis=1),
                           jnp.concatenate([o_nsa_s, o_diff_s], axis=1).astype(BF)], axis=0)
    x = _mm(mix, w_out, res=x, tm=tm, tn=1024, name="att_out")

    rows_p = rows[:n_prompt].reshape(batch, seq, 4, ng, hd)
    win_p = win[:n_prompt].reshape(batch, seq, 2, ng, hd)
    keep_p = min(WINDOW, seq)
    win_new_s = win_s.reshape(nb, ts, 2, ng, hd)
    keep_s = win_state_l.shape[1]
    state = (rows_p, rows_s.reshape(nb, ts, 4, ng, hd),
             win_p[:, seq - keep_p:], jnp.concatenate([win_state_l, win_new_s], axis=1)[:, -keep_s:],
             drows[:n_prompt].reshape(batch, seq, 2, nh, hd), drows_s.reshape(nb, ts, 2, nh, hd))
    return x, state


def _pool_conv_layer(x, n_prompt, batch, seq, nb, ts, past, pool_hist, conv_hist,
                     norm_g, w_in, w_out, pool_w, pool_scale, conv_w, *, tm):
    cw = pool_scale.shape[0]
    z = _mm(x, w_in, gain=norm_g, tm=tm, tn=1024, name="pc_in")
    mix_p = _pc_prompt(z, pool_w, pool_scale, conv_w, batch=batch, seq=seq, ts=256)

    zs = z[n_prompt:].reshape(nb, ts, 4 * cw)
    p_s = zs[..., :cw]
    u_s = zs[..., 2 * cw:3 * cw] * zs[..., 3 * cw:]
    tpad = 8
    front = lambda hist: jnp.pad(hist, ((0, 0), (HALO - hist.shape[1], 0), (0, 0)))
    back = lambda a: jnp.pad(a, ((0, 0), (0, tpad - ts), (0, 0)))
    p_ext = jnp.concatenate([front(pool_hist), back(p_s)], axis=1)
    u_ext = jnp.concatenate([front(conv_hist), back(u_s)], axis=1)
    mix_s = _pc_sample(p_ext, u_ext, back(zs[..., cw:2 * cw]), pool_w, pool_scale, conv_w, past=past, tb=8)
    mix = jnp.concatenate([mix_p, mix_s[:, :ts].reshape(nb * ts, 2 * cw)], axis=0)
    x = _mm(mix, w_out, res=x, tm=tm, tn=1024, name="pc_out")

    tail = jnp.stack([z[(b + 1) * seq - POOL_HIST:(b + 1) * seq] for b in range(batch)])
    ct = tail[:, POOL_HIST - (CONV_K - 1):]
    u_tail = ct[..., 2 * cw:3 * cw] * ct[..., 3 * cw:]
    state = (tail[..., :cw],
             jnp.concatenate([pool_hist, p_s], axis=1)[:, -POOL_HIST:],
             u_tail,
             jnp.concatenate([conv_hist, u_s], axis=1)[:, -(CONV_K - 1):])
    return x, state


def kernel(x_prompt, x_sample, cache_nsa, cache_diff, state_nsa_win, state_pool, state_conv, page_table,
           att_norm_g, att_w_in, att_w_out, nsa_q_gain, nsa_k_gain, nsa_cmp_w, diff_q_gain, diff_k_gain,
           diff_lambda, diff_out_gain, pc_norm_g, pc_w_in, pc_w_out, pool_w, pool_scale, conv_w,
           ffn_norm_g, peer_wq, peer_keys, peer_u, peer_v):
    batch, seq, d = x_prompt.shape
    nb, ts, _ = x_sample.shape
    n_prompt = batch * seq
    n = n_prompt + nb * ts
    depth = ffn_norm_g.shape[0]
    past = page_table.shape[1] * cache_nsa.shape[2]
    tm = n // 8
    assert n % 8 == 0 and tm % 16 == 0
    x = jnp.concatenate([x_prompt.reshape(n_prompt, d), x_sample.reshape(nb * ts, d)], axis=0)
    u_bf, v_bf = peer_u.astype(BF), peer_v.astype(BF)
    att_states, pc_states = [], []
    for li in range(depth):
        i = li // 2
        if li % 2 == 0:
            x, st = _attention_layer(
                x, li, n_prompt, batch, seq, nb, ts, cache_nsa[i], cache_diff[i], state_nsa_win[i], page_table,
                att_norm_g[i], att_w_in[i], att_w_out[i], nsa_q_gain[i], nsa_k_gain[i], nsa_cmp_w[i],
                diff_q_gain[i], diff_k_gain[i], diff_lambda[i], diff_out_gain[i], tm=tm)
            att_states.append(st)
        else:
            x, st = _pool_conv_layer(
                x, n_prompt, batch, seq, nb, ts, past, state_pool[i], state_conv[i],
                pc_norm_g[i], pc_w_in[i], pc_w_out[i], pool_w[i], pool_scale[i], conv_w[i], tm=tm)
            pc_states.append(st)
        x = _peer_ffn(x, ffn_norm_g[li], peer_wq, peer_keys, u_bf, v_bf, li, tm=tm)
    stack = lambda states, k: jnp.stack([s[k] for s in states])
    return (x[:n_prompt].reshape(batch, seq, d), x[n_prompt:].reshape(nb, ts, d),
            stack(att_states, 0), stack(att_states, 1), stack(att_states, 2), stack(att_states, 3),
            stack(att_states, 4), stack(att_states, 5),
            stack(pc_states, 0), stack(pc_states, 1), stack(pc_states, 2), stack(pc_states, 3))
```

```python
import functools
import math

import jax
import jax.numpy as jnp
from jax import lax
from jax.experimental import pallas as pl
from jax.experimental.pallas import tpu as pltpu

F32 = jnp.float32
BF = jnp.bfloat16
I32 = jnp.int32

HEAD_DIM = 128
NSA_HEADS = 8
NSA_KV_HEADS = 2
NSA_HPG = NSA_HEADS // NSA_KV_HEADS
L_CMP = 32
L_SEL = 64
N_SEL = 16
WINDOW = 512
FORCE_BONUS = 1e4
DIFF_HEADS = 8
DIFF_DK = HEAD_DIM // 2
POOL_WINDOWS = (2, 4, 8, 16)
POOL_HIST = max(POOL_WINDOWS) - 1
CONV_K = 3
PEER_HEADS = 8
PEER_NKEYS = 128
PEER_TOPK = 16
EPS = 1e-6
NEG = -1e30
LANES = 128
HALO = 16
VMEM_LIMIT = 56 * 1024 * 1024

_NT = (((1,), (1,)), ((), ()))


def _params(*sem):
    return pltpu.CompilerParams(dimension_semantics=sem, vmem_limit_bytes=VMEM_LIMIT)


def _dot_nt(a, b):
    return lax.dot_general(a, b, _NT, preferred_element_type=F32)


def _dot(a, b):
    return jnp.dot(a, b, preferred_element_type=F32)


def _pow2_neg(k):
    return lax.bitcast_convert_type((127 - k) << 23, F32)


def _rms(x, g):
    return x * lax.rsqrt(jnp.mean(x * x, axis=-1, keepdims=True) + EPS) * g


def _rms_halves(x, g):
    lo = lax.broadcasted_iota(I32, x.shape, 1) < DIFF_DK
    x2 = x * x
    s_lo = jnp.sum(jnp.where(lo, x2, 0.0), axis=-1, keepdims=True)
    s_hi = jnp.sum(jnp.where(lo, 0.0, x2), axis=-1, keepdims=True)
    ms = jnp.where(lo, s_lo, s_hi) * (1.0 / DIFF_DK)
    return x * lax.rsqrt(ms + EPS) * g


def _msoftmax(s, mask):
    sm = jnp.where(mask, s, NEG)
    m = jnp.max(sm, axis=-1, keepdims=True)
    e = jnp.where(mask, jnp.exp(sm - m), 0.0)
    d = jnp.sum(e, axis=-1, keepdims=True)
    return e / jnp.where(d > 0.0, d, 1.0)


def _msoftmax2(s1, m1, s2, m2):
    a1 = jnp.where(m1, s1, NEG)
    a2 = jnp.where(m2, s2, NEG)
    m = jnp.maximum(jnp.max(a1, axis=-1, keepdims=True), jnp.max(a2, axis=-1, keepdims=True))
    e1 = jnp.where(m1, jnp.exp(a1 - m), 0.0)
    e2 = jnp.where(m2, jnp.exp(a2 - m), 0.0)
    d = jnp.sum(e1, axis=-1, keepdims=True) + jnp.sum(e2, axis=-1, keepdims=True)
    inv = 1.0 / jnp.where(d > 0.0, d, 1.0)
    return e1 * inv, e2 * inv


def _mm_body(*refs, norm, has_res):
    if norm:
        x_ref, g_ref, w_ref = refs[:3]
        rest = refs[3:]
    else:
        x_ref, w_ref = refs[:2]
        rest = refs[2:]
    if has_res:
        r_ref, o_ref, xb_ref = rest
    else:
        o_ref, xb_ref = rest

    @pl.when(pl.program_id(1) == 0)
    def _():
        x = x_ref[...]
        if norm:
            x = _rms(x.astype(F32), g_ref[...])
        xb_ref[...] = x.astype(BF)

    y = _dot(xb_ref[...], w_ref[...])
    if has_res:
        y = y + r_ref[...]
    o_ref[...] = y.astype(o_ref.dtype)


def _mm(x, w, *, gain=None, res=None, tm, tn, name):
    m, k = x.shape
    n = w.shape[1]
    assert m % tm == 0 and n % tn == 0
    norm = gain is not None
    in_specs = [pl.BlockSpec((tm, k), lambda i, j: (i, 0))]
    args = [x]
    if norm:
        in_specs.append(pl.BlockSpec((1, k), lambda i, j: (0, 0)))
        args.append(gain.reshape(1, k))
    in_specs.append(pl.BlockSpec((k, tn), lambda i, j: (0, j)))
    args.append(w.astype(BF))
    if res is not None:
        in_specs.append(pl.BlockSpec((tm, tn), lambda i, j: (i, j)))
        args.append(res)
    return pl.pallas_call(
        functools.partial(_mm_body, norm=norm, has_res=res is not None),
        grid=(m // tm, n // tn),
        in_specs=in_specs,
        out_specs=pl.BlockSpec((tm, tn), lambda i, j: (i, j)),
        out_shape=jax.ShapeDtypeStruct((m, n), F32),
        scratch_shapes=[pltpu.VMEM((tm, k), BF)],
        compiler_params=_params("parallel", "arbitrary"),
        name=name,
    )(*args)


_Q0 = 0
_KV0 = NSA_HEADS * HEAD_DIM
_DQ0 = _KV0 + 6 * NSA_KV_HEADS * HEAD_DIM
_DK0 = _DQ0 + DIFF_HEADS * HEAD_DIM
_DV0 = _DK0 + DIFF_HEADS * HEAD_DIM
_GT0 = _DV0 + DIFF_HEADS * HEAD_DIM
_ZW = _GT0 + NSA_KV_HEADS * LANES


def _att_post_body(z_ref, qg_ref, kg_ref, dqg_ref, dkg_ref,
                   qn_ref, rows_ref, win_ref, gate_ref, dqn_ref, drows_ref):
    hd = HEAD_DIM
    qg = qg_ref[...]
    for h in range(NSA_HEADS):
        x = z_ref[:, _Q0 + h * hd:_Q0 + (h + 1) * hd]
        qn_ref[:, h * hd:(h + 1) * hd] = (_rms(x, qg) * (HEAD_DIM ** -0.5)).astype(qn_ref.dtype)
    gw = NSA_KV_HEADS * hd
    for br in range(3):
        for g in range(NSA_KV_HEADS):
            ko = _KV0 + br * 2 * gw + g * hd
            k = _rms(z_ref[:, ko:ko + hd], kg_ref[br:br + 1, :])
            v = z_ref[:, ko + gw:ko + gw + hd]
            if br < 2:
                rows_ref[:, br * 2 * gw + g * hd:br * 2 * gw + (g + 1) * hd] = k
                rows_ref[:, br * 2 * gw + gw + g * hd:br * 2 * gw + gw + (g + 1) * hd] = v
            else:
                win_ref[:, g * hd:(g + 1) * hd] = k
                win_ref[:, gw + g * hd:gw + (g + 1) * hd] = v
    gate_ref[...] = jax.nn.sigmoid(z_ref[:, _GT0:_ZW])
    dqg = dqg_ref[...]
    dkg = dkg_ref[...]
    for h in range(DIFF_HEADS):
        x = z_ref[:, _DQ0 + h * hd:_DQ0 + (h + 1) * hd]
        dqn_ref[:, h * hd:(h + 1) * hd] = (_rms_halves(x, dqg) * (DIFF_DK ** -0.5)).astype(dqn_ref.dtype)
        x = z_ref[:, _DK0 + h * hd:_DK0 + (h + 1) * hd]
        drows_ref[:, h * hd:(h + 1) * hd] = _rms_halves(x, dkg)
    dvw = DIFF_HEADS * hd
    drows_ref[:, dvw:2 * dvw] = z_ref[:, _DV0:_DV0 + dvw]


def _att_post(z, q_gain, k_gain, dq_gain, dk_gain, *, tm):
    n = z.shape[0]
    assert n % tm == 0 and z.shape[1] == _ZW
    hd = HEAD_DIM
    widths = (NSA_HEADS * hd, 4 * NSA_KV_HEADS * hd, 2 * NSA_KV_HEADS * hd, NSA_KV_HEADS * LANES,
              DIFF_HEADS * hd, 2 * DIFF_HEADS * hd)
    dtypes = (BF, F32, F32, F32, BF, F32)
    small = lambda r: pl.BlockSpec((r, hd), lambda i: (0, 0))
    return pl.pallas_call(
        _att_post_body,
        grid=(n // tm,),
        in_specs=[pl.BlockSpec((tm, _ZW), lambda i: (i, 0)), small(1), small(3), small(1), small(1)],
        out_specs=[pl.BlockSpec((tm, w), lambda i: (i, 0)) for w in widths],
        out_shape=[jax.ShapeDtypeStruct((n, w), d) for w, d in zip(widths, dtypes)],
        compiler_params=_params("parallel"),
        name="att_post",
    )(z, q_gain.reshape(1, hd), k_gain, dq_gain.reshape(1, hd), dk_gain.reshape(1, hd))


def _compress(x, w):
    length, c = x.shape
    npair = length // (2 * L_CMP)
    x3 = x.reshape(npair, 2 * L_CMP, c)
    even = jnp.sum(x3[:, :L_CMP, :] * w, axis=1)
    odd = jnp.sum(x3[:, L_CMP:, :] * w, axis=1)
    pad = jnp.zeros((LANES - 2 * npair, c), F32)
    return jnp.concatenate([even, odd, pad], axis=0)


def _nsa_cmp_body(x_ref, w_ref, o_ref):
    o_ref[...] = _compress(x_ref[...], w_ref[...])


def _nsa_cmp(rows, cmp_w, *, batch, seq):
    gw = NSA_KV_HEADS * HEAD_DIM
    assert seq % (2 * L_CMP) == 0 and seq // L_CMP <= LANES
    w = jnp.concatenate([jnp.tile(cmp_w[0], (1, NSA_KV_HEADS)), jnp.tile(cmp_w[1], (1, NSA_KV_HEADS))], axis=1)
    return pl.pallas_call(
        _nsa_cmp_body,
        grid=(batch,),
        in_specs=[pl.BlockSpec((seq, 2 * gw), lambda b: (b, 0)),
                  pl.BlockSpec((L_CMP, 2 * gw), lambda b: (0, 0))],
        out_specs=pl.BlockSpec((None, LANES, 2 * gw), lambda b: (b, 0, 0)),
        out_shape=jax.ShapeDtypeStruct((batch, LANES, 2 * gw), F32),
        compiler_params=_params("parallel"),
        name="nsa_cmp",
    )(rows, w)


def _cmp_positions(ncmp):
    c = lax.broadcasted_iota(I32, (1, LANES), 1)
    half = ncmp // 2
    blk = jnp.where(c < half, 2 * c, 2 * (c - half) + 1)
    return c, blk * L_CMP + (L_CMP - 1), c < ncmp


def _select_blocks(p_heads, t1, ncmp, nsel_blocks):
    c = lax.broadcasted_iota(I32, (1, LANES), 1)
    half = ncmp // 2
    imp = p_heads[0]
    for p in p_heads[1:]:
        imp = imp + p
    imp = jnp.where(c < half, imp + pltpu.roll(imp, LANES - half, 1), 0.0)
    cur = t1 // L_SEL
    forced = (c == 0) | (c == cur) | (c == cur - 1)
    valid = (c * L_SEL <= t1) & (c < nsel_blocks)
    score = jnp.where(valid, imp + jnp.where(forced, FORCE_BONUS, 0.0), NEG)
    rank = jnp.zeros(score.shape, I32)
    for jp in range(nsel_blocks):
        sj = score[:, jp:jp + 1]
        beats = (sj > score) | ((sj == score) & (c > jp))
        rank = rank + beats.astype(I32)
    return (rank < min(N_SEL, nsel_blocks)) & (c < nsel_blocks)


def _expand_blocks(sel01, length):
    j = lax.broadcasted_iota(I32, (LANES, length), 0)
    l = lax.broadcasted_iota(I32, (LANES, length), 1)
    e = jnp.where(l // L_SEL == j, 1.0, 0.0).astype(BF)
    return _dot(sel01.astype(BF), e)


CAUSAL_STEP = 256


def _causal_variants(need, seq, fn, out_ref):
    step = min(CAUSAL_STEP, seq)
    assert seq % step == 0
    for klen in range(step, seq + 1, step):
        @pl.when((need <= klen) & (need > klen - step))
        def _(klen=klen):
            out_ref[...] = fn(klen)


def _nsa_prompt_body(q_ref, gate_ref, kc_ref, vc_ref, ks_ref, vs_ref, kw_ref, vw_ref, o_ref, os_ref, *, tq, seq):
    g = pl.program_id(1)
    t0 = pl.program_id(2) * tq
    hg = NSA_HPG
    hd = HEAD_DIM
    rows = hg * tq
    q = q_ref[...]
    qs = jnp.concatenate([q[:, h * hd:(h + 1) * hd] for h in range(hg)], axis=0)
    r_io = lax.broadcasted_iota(I32, (rows, 1), 0)
    hrow = r_io // tq
    tpos = t0 + (r_io - hrow * tq)
    slope = _pow2_neg(g * hg + hrow + 1)
    t1 = t0 + lax.broadcasted_iota(I32, (tq, 1), 0)
    ncmp = seq // L_CMP
    nblk = -(-seq // L_SEL)

    _, pc, cvalid = _cmp_positions(ncmp)
    s = _dot_nt(qs, kc_ref[...].astype(BF)) - slope * (tpos - pc).astype(F32)
    p = _msoftmax(s, cvalid & (pc <= tpos))
    o_c = _dot(p.astype(BF), vc_ref[...].astype(BF))
    sel = _select_blocks([p[h * tq:(h + 1) * tq] for h in range(hg)], t1, ncmp, nblk)
    sel01 = jnp.where(sel, 1.0, 0.0)

    head_slope = [_pow2_neg(jnp.full((1, 1), g * hg + h + 1, I32)) for h in range(hg)]

    def attend(k, v, pos, ok):
        s = _dot_nt(qs, k)
        posf = pos.astype(F32)
        masked = jnp.where(ok, 0.0, NEG)
        ps = []
        for h in range(hg):
            logit = s[h * tq:(h + 1) * tq] + (masked + head_slope[h] * posf)
            e = jnp.exp(logit - jnp.max(logit, axis=-1, keepdims=True))
            ps.append((e * (1.0 / jnp.sum(e, axis=-1, keepdims=True))).astype(BF))
        return _dot(jnp.concatenate(ps, axis=0), v)

    def selected(klen):
        pos = lax.broadcasted_iota(I32, (1, klen), 1)
        ok = (pos <= t1) & (_expand_blocks(sel01, klen) > 0.5)
        return attend(ks_ref[0:klen, :].astype(BF), vs_ref[0:klen, :].astype(BF), pos, ok)

    _causal_variants(t0 + tq, seq, selected, os_ref)
    o_s = os_ref[...]

    wl = WINDOW + tq
    start = pl.multiple_of(jnp.maximum(t0 - WINDOW, 0), tq)
    posw = start + lax.broadcasted_iota(I32, (1, wl), 1)
    o_w = attend(kw_ref[pl.ds(start, wl), :].astype(BF), vw_ref[pl.ds(start, wl), :].astype(BF), posw,
                 (posw <= t1) & (posw > t1 - WINDOW))

    gt = gate_ref[...]
    for h in range(hg):
        rs = slice(h * tq, (h + 1) * tq)
        o = gt[:, 3 * h:3 * h + 1] * o_c[rs] + gt[:, 3 * h + 1:3 * h + 2] * o_s[rs] + gt[:, 3 * h + 2:3 * h + 3] * o_w[rs]
        o_ref[:, h * hd:(h + 1) * hd] = o.astype(o_ref.dtype)


def _nsa_prompt(qn, gates, cmp, rows, win, *, batch, seq, tq):
    hd = HEAD_DIM
    ng = NSA_KV_HEADS
    nq = seq // tq
    assert seq % tq == 0 and tq % LANES == 0 and WINDOW % tq == 0 and seq >= WINDOW + tq
    gq = NSA_HPG * hd
    seq_blk = lambda col: pl.BlockSpec((seq, hd), lambda b, g, i, col=col: (b, col + g))
    return pl.pallas_call(
        functools.partial(_nsa_prompt_body, tq=tq, seq=seq),
        grid=(batch, ng, nq),
        in_specs=[
            pl.BlockSpec((tq, gq), lambda b, g, i: (b * nq + i, g)),
            pl.BlockSpec((tq, LANES), lambda b, g, i: (b * nq + i, g)),
            pl.BlockSpec((None, LANES, hd), lambda b, g, i: (b, 0, g)),
            pl.BlockSpec((None, LANES, hd), lambda b, g, i: (b, 0, ng + g)),
            seq_blk(2 * ng), seq_blk(3 * ng),
            seq_blk(0), seq_blk(ng),
        ],
        out_specs=pl.BlockSpec((tq, gq), lambda b, g, i: (b * nq + i, g)),
        out_shape=jax.ShapeDtypeStruct((batch * seq, ng * gq), BF),
        scratch_shapes=[pltpu.VMEM((NSA_HPG * tq, hd), F32)],
        compiler_params=_params("parallel", "parallel", "arbitrary"),
        name="nsa_prompt",
    )(qn, gates, cmp, cmp, rows, rows, win, win)


def _lambda(lv, lam_init):
    a = jnp.sum(lv[0:1, :] * lv[1:2, :], axis=-1, keepdims=True)
    b = jnp.sum(lv[2:3, :] * lv[3:4, :], axis=-1, keepdims=True)
    return jnp.exp(a) - jnp.exp(b) + lam_init


def _diff_prompt_body(lv_ref, og_ref, q_ref, k_ref, v_ref, o_ref, acc_ref, *, tq, seq, lam_init):
    h = pl.program_id(1)
    t0 = pl.program_id(2) * tq
    q = q_ref[...].astype(F32)
    lo = lax.broadcasted_iota(I32, q.shape, 1) < DIFF_DK
    qs = jnp.concatenate([jnp.where(lo, q, 0.0), jnp.where(lo, 0.0, q)], axis=0).astype(BF)
    t1 = t0 + lax.broadcasted_iota(I32, (tq, 1), 0)
    slope = _pow2_neg(jnp.full((1, 1), h + 1, I32))
    lam = _lambda(lv_ref[...], lam_init)

    def attend(klen):
        pos = lax.broadcasted_iota(I32, (1, klen), 1)
        bias = jnp.where(pos <= t1, slope * pos.astype(F32), NEG)
        s = _dot_nt(qs, k_ref[0:klen, :].astype(BF))
        parts = []
        for mp in range(2):
            logit = s[mp * tq:(mp + 1) * tq] + bias
            e = jnp.exp(logit - jnp.max(logit, axis=-1, keepdims=True))
            parts.append((e, 1.0 / jnp.sum(e, axis=-1, keepdims=True)))
        a = parts[0][0] * parts[0][1] - parts[1][0] * (lam * parts[1][1])
        return _dot(a.astype(BF), v_ref[0:klen, :].astype(BF))

    _causal_variants(t0 + tq, seq, attend, acc_ref)
    o_ref[...] = (_rms(acc_ref[...], og_ref[...]) * (1.0 - lam_init)).astype(o_ref.dtype)


def _diff_prompt(dqn, drows, lam_vec, out_gain, *, batch, seq, tq, lam_init):
    hd = HEAD_DIM
    nh = DIFF_HEADS
    nq = seq // tq
    assert seq % tq == 0
    return pl.pallas_call(
        functools.partial(_diff_prompt_body, tq=tq, seq=seq, lam_init=lam_init),
        grid=(batch, nh, nq),
        in_specs=[
            pl.BlockSpec((4, DIFF_DK), lambda b, h, i: (0, 0)),
            pl.BlockSpec((1, hd), lambda b, h, i: (0, 0)),
            pl.BlockSpec((tq, hd), lambda b, h, i: (b * nq + i, h)),
            pl.BlockSpec((seq, hd), lambda b, h, i: (b, h)),
            pl.BlockSpec((seq, hd), lambda b, h, i: (b, nh + h)),
        ],
        out_specs=pl.BlockSpec((tq, hd), lambda b, h, i: (b * nq + i, h)),
        out_shape=jax.ShapeDtypeStruct((batch * seq, nh * hd), BF),
        scratch_shapes=[pltpu.VMEM((tq, hd), F32)],
        compiler_params=_params("parallel", "parallel", "arbitrary"),
        name="diff_prompt",
    )(lam_vec, out_gain.reshape(1, hd), dqn, drows, drows)


def _page_copies(pt_ref, cache_ref, buf_ref, sem_ref, b, slot, start):
    npages = pt_ref.shape[1]
    page, nkind, nhead = cache_ref.shape[1:4]

    def one_page(p, c):
        phys = pt_ref[b, p]
        dst_rows = pl.ds(pl.multiple_of(p * page, page), page)
        for kind in range(nkind):
            for h in range(nhead):
                cp = pltpu.make_async_copy(cache_ref.at[phys, :, kind, h, :],
                                           buf_ref.at[slot, kind, h, dst_rows, :], sem_ref.at[slot])
                if start:
                    cp.start()
                else:
                    cp.wait()
        return c

    lax.fori_loop(0, npages, one_page, 0)


def _state_copies(state_ref, buf_ref, sem_ref, b, slot, start):
    nkind, nhead = state_ref.shape[2:4]
    for kind in range(nkind):
        for h in range(nhead):
            cp = pltpu.make_async_copy(state_ref.at[b, :, kind, h, :], buf_ref.at[slot, kind, h], sem_ref.at[slot])
            if start:
                cp.start()
            else:
                cp.wait()


def _double_buffered(fetch):
    b = pl.program_id(0)
    slot = b % 2

    @pl.when(b == 0)
    def _():
        fetch(b, slot, True)

    @pl.when(b + 1 < pl.num_programs(0))
    def _():
        fetch(b + 1, 1 - slot, True)

    fetch(b, slot, False)
    return slot


def _nsa_sample_body(pt_ref, q_ref, gate_ref, nrow_ref, nwin_ref, cw_ref, cache_ref, wstate_ref, o_ref,
                     kv_buf, w_buf, kv_sem, w_sem, *, past, ts):
    def fetch(b, slot, start):
        _page_copies(pt_ref, cache_ref, kv_buf, kv_sem, b, slot, start)
        _state_copies(wstate_ref, w_buf, w_sem, b, slot, start)

    slot = _double_buffered(fetch)
    past_ref = kv_buf.at[slot]
    wst_ref = w_buf.at[slot]
    hg = NSA_HPG
    hd = HEAD_DIM
    ng = NSA_KV_HEADS
    rows = q_ref.shape[1]
    tsp = rows // hg
    npad = nrow_ref.shape[0]
    gw = ng * hd
    ncmp = (past + ts) // L_CMP
    nblk = -(-(past + ts) // L_SEL)
    new_blk = past // L_SEL
    r_io = lax.broadcasted_iota(I32, (rows, 1), 0)
    hrow = r_io // tsp
    trow = r_io - hrow * tsp
    tpos = past + trow
    t1 = past + lax.broadcasted_iota(I32, (tsp, 1), 0)
    jn = lax.broadcasted_iota(I32, (1, npad), 1)
    new_ok = (jn <= trow) & (jn < ts)
    dist_new = (trow - jn).astype(F32)
    _, pc, cvalid = _cmp_positions(ncmp)
    pos = lax.broadcasted_iota(I32, (1, past), 1)
    nwst = wst_ref.shape[2]
    posw = past - nwst + lax.broadcasted_iota(I32, (1, nwst), 1)
    for g in range(ng):
        qs = q_ref[g]
        slope = _pow2_neg(g * hg + hrow + 1)
        kc = _compress(past_ref[0, g], cw_ref[0]).astype(BF)
        vc = _compress(past_ref[1, g], cw_ref[1]).astype(BF)
        s = _dot_nt(qs, kc) - slope * (tpos - pc).astype(F32)
        p = _msoftmax(s, cvalid & (pc <= tpos))
        o_c = _dot(p.astype(BF), vc)
        sel = _select_blocks([p[h * tsp:(h + 1) * tsp] for h in range(hg)], t1, ncmp, nblk)
        sel01 = jnp.concatenate([jnp.where(sel, 1.0, 0.0)] * hg, axis=0)
        selm = _expand_blocks(sel01, past)
        sel_new = sel01[:, new_blk:new_blk + 1] > 0.5
        kn = nrow_ref[:, 2 * gw + g * hd:2 * gw + (g + 1) * hd].astype(BF)
        vn = nrow_ref[:, 3 * gw + g * hd:3 * gw + (g + 1) * hd].astype(BF)
        s1 = _dot_nt(qs, past_ref[2, g].astype(BF)) - slope * (tpos - pos).astype(F32)
        s2 = _dot_nt(qs, kn) - slope * dist_new
        p1, p2 = _msoftmax2(s1, selm > 0.5, s2, new_ok & sel_new)
        o_s = _dot(p1.astype(BF), past_ref[3, g].astype(BF)) + _dot(p2.astype(BF), vn)
        kn = nwin_ref[:, g * hd:(g + 1) * hd].astype(BF)
        vn = nwin_ref[:, gw + g * hd:gw + (g + 1) * hd].astype(BF)
        dist = tpos - posw
        s1 = _dot_nt(qs, wst_ref[0, g].astype(BF)) - slope * dist.astype(F32)
        s2 = _dot_nt(qs, kn) - slope * dist_new
        p1, p2 = _msoftmax2(s1, (dist >= 0) & (dist < WINDOW), s2, new_ok)
        o_w = _dot(p1.astype(BF), wst_ref[1, g].astype(BF)) + _dot(p2.astype(BF), vn)
        gt = gate_ref[g]
        o_ref[g] = gt[:, 0:1] * o_c + gt[:, 1:2] * o_s + gt[:, 2:3] * o_w


def _nsa_sample(q, gates, cache, page_table, win_state, new_rows, new_win, cmp_w, *, ts):
    nb, ng, rows, hd = q.shape
    past = page_table.shape[1] * cache.shape[1]
    nwst = win_state.shape[1]
    npad = new_rows.shape[1]
    assert past % (2 * L_CMP) == 0 and past % L_SEL == 0 and nwst == WINDOW and (past + ts) // L_CMP == past // L_CMP
    assert cache.shape[2:] == (4, ng, hd) and win_state.shape[2:] == (2, ng, hd)
    return pl.pallas_call(
        functools.partial(_nsa_sample_body, past=past, ts=ts),
        grid_spec=pltpu.PrefetchScalarGridSpec(
            num_scalar_prefetch=1,
            grid=(nb,),
            in_specs=[
                pl.BlockSpec((None, ng, rows, hd), lambda b, pt: (b, 0, 0, 0)),
                pl.BlockSpec((None, ng, rows, LANES), lambda b, pt: (b, 0, 0, 0)),
                pl.BlockSpec((None, npad, new_rows.shape[2]), lambda b, pt: (b, 0, 0)),
                pl.BlockSpec((None, npad, new_win.shape[2]), lambda b, pt: (b, 0, 0)),
                pl.BlockSpec((2, L_CMP, hd), lambda b, pt: (0, 0, 0)),
                pl.BlockSpec(memory_space=pl.ANY),
                pl.BlockSpec(memory_space=pl.ANY),
            ],
            out_specs=pl.BlockSpec((None, ng, rows, hd), lambda b, pt: (b, 0, 0, 0)),
            scratch_shapes=[pltpu.VMEM((2, 4, ng, past, hd), F32), pltpu.VMEM((2, 2, ng, nwst, hd), F32),
                            pltpu.SemaphoreType.DMA((2,)), pltpu.SemaphoreType.DMA((2,))],
        ),
        out_shape=jax.ShapeDtypeStruct((nb, ng, rows, hd), F32),
        compiler_params=_params("arbitrary"),
        name="nsa_sample",
    )(page_table, q, gates, new_rows, new_win, cmp_w, cache, win_state)


def _diff_sample_body(pt_ref, lv_ref, og_ref, q_ref, ndrow_ref, cache_ref, o_ref, kv_buf, kv_sem, *, past, ts, lam_init):
    slot = _double_buffered(functools.partial(_page_copies, pt_ref, cache_ref, kv_buf, kv_sem))
    past_ref = kv_buf.at[slot]
    hd = HEAD_DIM
    nh = DIFF_HEADS
    npad = ndrow_ref.shape[0]
    rows = q_ref.shape[1]
    tsp = rows // 2
    r_io = lax.broadcasted_iota(I32, (rows, 1), 0)
    trow = jnp.where(r_io >= tsp, r_io - tsp, r_io)
    tpos = past + trow
    dist = (tpos - lax.broadcasted_iota(I32, (1, past), 1)).astype(F32)
    jn = lax.broadcasted_iota(I32, (1, npad), 1)
    new_ok = (jn <= trow) & (jn < ts)
    dist_new = (trow - jn).astype(F32)
    lam = _lambda(lv_ref[...], lam_init)
    og = og_ref[...]
    for h in range(nh):
        slope = 2.0 ** -(h + 1)
        qs = q_ref[h]
        kn = ndrow_ref[:, h * hd:(h + 1) * hd].astype(BF)
        vn = ndrow_ref[:, (nh + h) * hd:(nh + h + 1) * hd].astype(BF)
        s1 = _dot_nt(qs, past_ref[0, h].astype(BF)) - slope * dist
        s2 = _dot_nt(qs, kn) - slope * dist_new
        p1, p2 = _msoftmax2(s1, dist >= 0.0, s2, new_ok)
        a1 = p1 - lam * jnp.concatenate([p1[tsp:], p1[:tsp]], axis=0)
        a2 = p2 - lam * jnp.concatenate([p2[tsp:], p2[:tsp]], axis=0)
        o = _dot(a1.astype(BF), past_ref[1, h].astype(BF)) + _dot(a2.astype(BF), vn)
        o_ref[h] = _rms(o, og) * (1.0 - lam_init)


def _diff_sample(q, cache, page_table, new_drows, lam_vec, out_gain, *, ts, lam_init):
    nb, nh, rows, hd = q.shape
    past = page_table.shape[1] * cache.shape[1]
    npad = new_drows.shape[1]
    assert cache.shape[2:] == (2, nh, hd)
    return pl.pallas_call(
        functools.partial(_diff_sample_body, past=past, ts=ts, lam_init=lam_init),
        grid_spec=pltpu.PrefetchScalarGridSpec(
            num_scalar_prefetch=1,
            grid=(nb,),
            in_specs=[
                pl.BlockSpec((4, DIFF_DK), lambda b, pt: (0, 0)),
                pl.BlockSpec((1, hd), lambda b, pt: (0, 0)),
                pl.BlockSpec((None, nh, rows, hd), lambda b, pt: (b, 0, 0, 0)),
                pl.BlockSpec((None, npad, new_drows.shape[2]), lambda b, pt: (b, 0, 0)),
                pl.BlockSpec(memory_space=pl.ANY),
            ],
            out_specs=pl.BlockSpec((None, nh, rows, hd), lambda b, pt: (b, 0, 0, 0)),
            scratch_shapes=[pltpu.VMEM((2, 2, nh, past, hd), F32), pltpu.SemaphoreType.DMA((2,))],
        ),
        out_shape=jax.ShapeDtypeStruct((nb, nh, rows, hd), F32),
        compiler_params=_params("arbitrary"),
        name="diff_sample",
    )(page_table, lam_vec, out_gain.reshape(1, hd), q, new_drows, cache)


def _rows(a, lo, n):
    return lax.slice_in_dim(a, lo, lo + n, axis=a.ndim - 2)


def _pool_conv(p_ext, u_ext, b_gate, cnt_pos, pool_w_ref, pool_scale, conv_w, ts):
    gwid = p_ext.shape[-1] // len(POOL_WINDOWS)
    lead = p_ext.shape[:-2]
    outs = []
    for gi, w in enumerate(POOL_WINDOWS):
        x = p_ext[..., gi * gwid:(gi + 1) * gwid]
        acc = x
        span = 1
        while span < w:
            n = acc.shape[-2]
            acc = _rows(acc, span, n - span) + _rows(acc, 0, n - span)
            span *= 2
        win_sum = _rows(acc, HALO - (w - 1), ts)
        cnt = jnp.minimum(w, cnt_pos + 1).astype(F32)
        m = win_sum / cnt - _rows(x, HALO, ts)
        m2 = m.reshape((-1, gwid)).astype(BF)
        outs.append(_dot(m2, pool_w_ref[gi].astype(BF)).reshape(lead + (ts, gwid)))
    y_pool = jnp.concatenate(outs, axis=-1) * pool_scale
    conv = None
    for j in range(CONV_K):
        term = _rows(u_ext, HALO - (CONV_K - 1) + j, ts) * conv_w[j:j + 1, :]
        conv = term if conv is None else conv + term
    return jnp.concatenate([y_pool, b_gate * conv], axis=-1).astype(BF)


def _pc_prompt_body(zc_ref, zh_ref, pw_ref, ps_ref, cw_ref, o_ref, *, ts, cw):
    i = pl.program_id(1)
    keep = i > 0
    p_ext = jnp.concatenate([jnp.where(keep, zh_ref[:, 0:cw], 0.0), zc_ref[:, 0:cw]], axis=0)
    u_h = jnp.where(keep, zh_ref[:, 2 * cw:3 * cw] * zh_ref[:, 3 * cw:4 * cw], 0.0)
    u_ext = jnp.concatenate([u_h, zc_ref[:, 2 * cw:3 * cw] * zc_ref[:, 3 * cw:4 * cw]], axis=0)
    cnt_pos = i * ts + lax.broadcasted_iota(I32, (ts, 1), 0)
    o_ref[...] = _pool_conv(p_ext, u_ext, zc_ref[:, cw:2 * cw], cnt_pos, pw_ref, ps_ref[...], cw_ref[...], ts)


def _pc_prompt(z, pool_w, pool_scale, conv_w, *, batch, seq, ts):
    cw = pool_scale.shape[0]
    nt = seq // ts
    assert seq % ts == 0 and ts % HALO == 0 and z.shape[1] == 4 * cw
    hb = ts // HALO
    return pl.pallas_call(
        functools.partial(_pc_prompt_body, ts=ts, cw=cw),
        grid=(batch, nt),
        in_specs=[
            pl.BlockSpec((ts, 4 * cw), lambda b, i: (b * nt + i, 0)),
            pl.BlockSpec((HALO, 4 * cw), lambda b, i: (jnp.maximum((b * nt + i) * hb - 1, 0), 0)),
            pl.BlockSpec(pool_w.shape, lambda b, i: (0, 0, 0)),
            pl.BlockSpec((1, cw), lambda b, i: (0, 0)),
            pl.BlockSpec((CONV_K, cw), lambda b, i: (0, 0)),
        ],
        out_specs=pl.BlockSpec((ts, 2 * cw), lambda b, i: (b * nt + i, 0)),
        out_shape=jax.ShapeDtypeStruct((batch * seq, 2 * cw), BF),
        compiler_params=_params("parallel", "arbitrary"),
        name="pool_conv_prompt",
    )(z, z, pool_w, pool_scale.reshape(1, cw), conv_w)


def _pc_sample_body(pe_ref, ue_ref, bg_ref, pw_ref, ps_ref, cw_ref, o_ref, *, ts, past):
    cnt_pos = past + lax.broadcasted_iota(I32, (ts, 1), 0)
    o_ref[...] = _pool_conv(pe_ref[...], ue_ref[...], bg_ref[...], cnt_pos, pw_ref, ps_ref[...], cw_ref[...], ts)


def _pc_sample(p_ext, u_ext, b_gate, pool_w, pool_scale, conv_w, *, past, tb):
    nb, ext, cw = p_ext.shape
    ts = ext - HALO
    assert nb % tb == 0
    blk = lambda r: pl.BlockSpec((tb, r, cw), lambda b: (b, 0, 0))
    return pl.pallas_call(
        functools.partial(_pc_sample_body, ts=ts, past=past),
        grid=(nb // tb,),
        in_specs=[blk(ext), blk(ext), blk(ts),
                  pl.BlockSpec(pool_w.shape, lambda b: (0, 0, 0)),
                  pl.BlockSpec((1, cw), lambda b: (0, 0)),
                  pl.BlockSpec((CONV_K, cw), lambda b: (0, 0))],
        out_specs=pl.BlockSpec((tb, ts, 2 * cw), lambda b: (b, 0, 0)),
        out_shape=jax.ShapeDtypeStruct((nb, ts, 2 * cw), BF),
        compiler_params=_params("parallel"),
        name="pool_conv_sample",
    )(p_ext, u_ext, b_gate, pool_w, pool_scale.reshape(1, cw), conv_w)


def _peer_score_body(x_ref, g_ref, wq_ref, keys_ref, s_ref, xn_ref):
    @pl.when(pl.program_id(1) == 0)
    def _():
        xn_ref[...] = _rms(x_ref[...], g_ref[...]).astype(xn_ref.dtype)

    z = _dot(xn_ref[...], wq_ref[...]).astype(BF)
    nk = keys_ref.shape[1]
    dk = keys_ref.shape[2]
    for r in range(keys_ref.shape[0]):
        s_ref[r * nk:(r + 1) * nk, :] = _dot_nt(keys_ref[r], z[:, r * dk:(r + 1) * dk])


def _peer_scores(x, gain, wq, keys, li, *, tm):
    n, d = x.shape
    nsub, nk, dk = keys.shape[1:]
    per = 4
    assert n % tm == 0 and tm % LANES == 0 and nsub % per == 0 and wq.shape[2] == nsub * dk
    return pl.pallas_call(
        _peer_score_body,
        grid=(n // tm, nsub // per),
        in_specs=[
            pl.BlockSpec((tm, d), lambda i, j: (i, 0)),
            pl.BlockSpec((1, d), lambda i, j: (0, 0)),
            pl.BlockSpec((None, d, per * dk), lambda i, j: (li, 0, j)),
            pl.BlockSpec((None, per, nk, dk), lambda i, j: (li, j, 0, 0)),
        ],
        out_specs=[pl.BlockSpec((per * nk, tm), lambda i, j: (j, i)),
                   pl.BlockSpec((tm, d), lambda i, j: (i, 0))],
        out_shape=[jax.ShapeDtypeStruct((nsub * nk, n), F32), jax.ShapeDtypeStruct((n, d), BF)],
        compiler_params=_params("parallel", "arbitrary"),
        name="peer_scores",
    )(x, gain.reshape(1, d), wq.astype(BF), keys.astype(BF))


def _topk_rows(x, k, code):
    slot = lax.broadcasted_iota(I32, (k, x.shape[1]), 0)
    vals = jnp.zeros((k, x.shape[1]), F32)
    idxs = jnp.zeros((k, x.shape[1]), F32)
    for kk in range(k):
        m = jnp.max(x, axis=0, keepdims=True)
        idx = jnp.min(jnp.where(x == m, code, float(2 ** 24)), axis=0, keepdims=True)
        vals = jnp.where(slot == kk, m, vals)
        idxs = jnp.where(slot == kk, idx, idxs)
        x = jnp.where(code == idx, -jnp.inf, x)
    return vals, idxs.astype(I32)


def _pair_candidates(v0, v1):
    k = PEER_TOPK
    assert k == 16 and v0.shape[0] == k
    half = k // 2
    blocks = [v0[0:1] + v1] + [v0[a:a + 1] + v1[0:half] for a in range(1, half)] + [v0[half:] + v1[0:1]]
    r = lax.broadcasted_iota(I32, (k + (half - 1) * half + half, v0.shape[1]), 0)
    mid = r - k
    code = jnp.where(r < k, r,
                     jnp.where(mid < (half - 1) * half, (mid // half + 1) * k + mid % half,
                               (half + mid - (half - 1) * half) * k))
    return jnp.concatenate(blocks, axis=0), code.astype(F32)


def _peer_topk_body(s_ref, i1_ref, i2_ref, g_ref, sv_ref, si_ref, i1t_ref, i2t_ref, gt_ref):
    nsub = sv_ref.shape[0]
    k = PEER_TOPK
    nk = PEER_NKEYS

    def sub_key(gi, c):
        x = s_ref[pl.ds(pl.multiple_of(gi * nk, nk), nk), :]
        v, i = _topk_rows(x, k, lax.broadcasted_iota(I32, x.shape, 0).astype(F32))
        sv_ref[gi] = v
        si_ref[gi] = i
        return c

    lax.fori_loop(0, nsub, sub_key, 0)

    def head(h, c):
        comb, code = _pair_candidates(sv_ref[2 * h], sv_ref[2 * h + 1])
        cv, ci = _topk_rows(comb, k, code)
        a = ci // k
        b = ci - a * k
        s0 = si_ref[2 * h]
        s1 = si_ref[2 * h + 1]
        i1 = jnp.zeros(ci.shape, I32)
        i2 = jnp.zeros(ci.shape, I32)
        for q in range(k):
            i1 = jnp.where(a == q, s0[q:q + 1], i1)
            i2 = jnp.where(b == q, s1[q:q + 1], i2)
        e = jnp.exp(cv - cv[0:1])
        rows = pl.ds(pl.multiple_of(h * k, k), k)
        i1t_ref[rows, :] = i1
        i2t_ref[rows, :] = i2
        gt_ref[rows, :] = e / jnp.sum(e, axis=0, keepdims=True)
        return c

    lax.fori_loop(0, nsub // 2, head, 0)
    i1_ref[...] = i1t_ref[...].T
    i2_ref[...] = i2t_ref[...].T
    g_ref[...] = gt_ref[...].T


def _peer_topk(s_t, *, nh, tl):
    rows, n = s_t.shape
    npair = nh * PEER_TOPK
    assert rows == nh * 2 * PEER_NKEYS and n % tl == 0 and tl % LANES == 0
    out = pl.BlockSpec((tl, npair), lambda i: (i, 0))
    return pl.pallas_call(
        _peer_topk_body,
        grid=(n // tl,),
        in_specs=[pl.BlockSpec((rows, tl), lambda i: (0, i))],
        out_specs=[out, out, out],
        out_shape=[jax.ShapeDtypeStruct((n, npair), I32), jax.ShapeDtypeStruct((n, npair), I32),
                   jax.ShapeDtypeStruct((n, npair), F32)],
        scratch_shapes=[pltpu.VMEM((nh * 2, PEER_TOPK, tl), F32), pltpu.VMEM((nh * 2, PEER_TOPK, tl), I32),
                        pltpu.VMEM((npair, tl), I32), pltpu.VMEM((npair, tl), I32), pltpu.VMEM((npair, tl), F32)],
        compiler_params=_params("parallel"),
        name="peer_topk",
    )(s_t)


W_ROWS = 8


def _peer_w_body(a_ref, b_ref, g_ref, w_ref):
    tt = a_ref.shape[0]
    npair = a_ref.shape[2]
    io = lax.broadcasted_iota(I32, (tt, PEER_NKEYS, npair), 1)
    one_a = jnp.where(a_ref[...] == io, 1.0, 0.0).astype(BF)
    g = g_ref[...]
    g_hi = g.astype(BF).astype(F32)
    g_lo = g - g_hi
    hit = b_ref[...] == io
    dims = (((2,), (2,)), ((0,), (0,)))
    lhs = jnp.concatenate([one_a, one_a], axis=2)
    rhs = jnp.concatenate([jnp.where(hit, g_hi, 0.0).astype(BF), jnp.where(hit, g_lo, 0.0).astype(BF)], axis=2)
    w = lax.dot_general(lhs, rhs, dims, preferred_element_type=F32)
    nchunk = w_ref.shape[0]
    w4 = w.reshape(tt, nchunk, PEER_NKEYS // nchunk, PEER_NKEYS)
    for j in range(nchunk):
        w_ref[j] = w4[:, j]


def _peer_w(i1, i2, gate, *, tt):
    n, npair = i1.shape
    assert n % tt == 0
    blk = pl.BlockSpec((tt, 1, npair), lambda i: (i, 0, 0))
    return pl.pallas_call(
        _peer_w_body,
        grid=(n // tt,),
        in_specs=[blk, blk, blk],
        out_specs=pl.BlockSpec((PEER_NKEYS // W_ROWS, tt, W_ROWS, PEER_NKEYS), lambda i: (0, i, 0, 0)),
        out_shape=jax.ShapeDtypeStruct((PEER_NKEYS // W_ROWS, n, W_ROWS, PEER_NKEYS), F32),
        compiler_params=_params("parallel"),
        name="peer_w",
    )(i1.reshape(n, 1, npair), i2.reshape(n, 1, npair), gate.reshape(n, 1, npair))


def _peer_dense_body(xn_ref, xr_ref, u_ref, v_ref, w_ref, o_ref):
    @pl.when(pl.program_id(1) == 0)
    def _():
        o_ref[...] = xr_ref[...]

    tn = xn_ref.shape[0]
    nk = w_ref.shape[1]
    nw = w_ref.shape[0] // tn
    s = _dot_nt(xn_ref[...], u_ref[...])
    act = 0.5 * s * (1.0 + lax.erf(s * (2.0 ** -0.5)))
    wact = jnp.concatenate(
        [act[:, r * nk:(r + 1) * nk] * w_ref[pl.ds(r, tn, stride=nw), :] for r in range(nw)], axis=1)
    o_ref[...] += _dot(wact.astype(BF), v_ref[...])


def _peer_dense(xn, x_res, u_bf, v_bf, w, li, *, tn, ec):
    n, d = xn.shape
    ne = u_bf.shape[1]
    nchunk, _, nw, nk = w.shape
    assert n % tn == 0 and ne == nchunk * ec and ec == nw * nk
    w = w.reshape(nchunk, n * nw, nk)
    return pl.pallas_call(
        _peer_dense_body,
        grid=(n // tn, ne // ec),
        in_specs=[
            pl.BlockSpec((tn, d), lambda i, j: (i, 0)),
            pl.BlockSpec((tn, d), lambda i, j: (i, 0)),
            pl.BlockSpec((None, ec, d), lambda i, j: (li, j, 0)),
            pl.BlockSpec((None, ec, d), lambda i, j: (li, j, 0)),
            pl.BlockSpec((None, tn * nw, nk), lambda i, j: (j, i, 0)),
        ],
        out_specs=pl.BlockSpec((tn, d), lambda i, j: (i, 0)),
        out_shape=jax.ShapeDtypeStruct((n, d), F32),
        compiler_params=_params("parallel", "arbitrary"),
        name="peer_dense",
    )(xn, x_res, u_bf, v_bf, w)


def _peer_ffn(x, gain, wq, keys, u_bf, v_bf, li, *, tm):
    nl, nh = keys.shape[:2]
    s_t, xn = _peer_scores(x, gain, wq, keys.reshape(nl, nh * 2, PEER_NKEYS, keys.shape[-1]), li, tm=512)
    i1, i2, gate = _peer_topk(s_t, nh=nh, tl=512)
    w = _peer_w(i1, i2, gate, tt=64)
    return _peer_dense(xn, x, u_bf, v_bf, w, li, tn=tm // 2, ec=W_ROWS * PEER_NKEYS)


def _reorder_att_w_in(w_in):
    d = w_in.shape[0]
    o2 = _DQ0
    ngate = 3 * NSA_HPG
    gates = [jnp.pad(w_in[:, o2 + g * ngate:o2 + (g + 1) * ngate], ((0, 0), (0, LANES - ngate)))
             for g in range(NSA_KV_HEADS)]
    return jnp.concatenate([w_in[:, :o2], w_in[:, o2 + 3 * NSA_HEADS:]] + gates, axis=1)


def _attention_layer(x, li, n_prompt, batch, seq, nb, ts, cache_nsa_l, cache_diff_l, win_state_l, page_table,
                     norm_g, w_in, w_out, q_gain, k_gain, cmp_w, dq_gain, dk_gain, lam_vec, out_gain, *, tm):
    hd = HEAD_DIM
    ng = NSA_KV_HEADS
    hg = NSA_HPG
    nh = DIFF_HEADS
    lam_init = 0.8 - 0.6 * math.exp(-0.3 * li)

    z = _mm(x, _reorder_att_w_in(w_in), gain=norm_g, tm=tm, tn=256, name="att_in")
    qn, rows, win, gates, dqn, drows = _att_post(z, q_gain, k_gain, dq_gain, dk_gain, tm=272)

    cmp = _nsa_cmp(rows, cmp_w, batch=batch, seq=seq)
    o_nsa_p = _nsa_prompt(qn, gates, cmp, rows, win, batch=batch, seq=seq, tq=128)
    o_diff_p = _diff_prompt(dqn, drows, lam_vec, out_gain, batch=batch, seq=seq, tq=256, lam_init=lam_init)

    tsp = 8
    assert ts <= tsp
    pad_t = lambda a, ax: jnp.pad(a, [(0, tsp - ts) if k == ax else (0, 0) for k in range(a.ndim)])
    pad8 = lambda a: pad_t(a.reshape(nb, ts, a.shape[-1]), 1)
    rows_s, win_s, drows_s = rows[n_prompt:], win[n_prompt:], drows[n_prompt:]
    q_s = pad_t(qn[n_prompt:].reshape(nb, ts, ng, hg, hd).transpose(0, 2, 3, 1, 4), 3).reshape(nb, ng, hg * tsp, hd)
    g_s = gates[n_prompt:].reshape(nb, ts, ng, LANES)[..., :3 * hg].reshape(nb, ts, ng, hg, 3)
    g_s = pad_t(g_s.transpose(0, 2, 3, 1, 4), 3).reshape(nb, ng, hg * tsp, 3)
    g_s = jnp.pad(g_s, ((0, 0), (0, 0), (0, 0), (0, LANES - 3)))
    o_nsa_s = _nsa_sample(q_s, g_s, cache_nsa_l, page_table, win_state_l, pad8(rows_s), pad8(win_s), cmp_w, ts=ts)
    o_nsa_s = o_nsa_s.reshape(nb, ng, hg, tsp, hd)[:, :, :, :ts]
    o_nsa_s = o_nsa_s.transpose(0, 3, 1, 2, 4).reshape(nb * ts, ng * hg * hd)

    dq_s = pad_t(dqn[n_prompt:].reshape(nb, ts, nh, hd).transpose(0, 2, 1, 3), 2)
    lo = jnp.arange(hd) < DIFF_DK
    dq_s = jnp.concatenate([jnp.where(lo, dq_s, 0), jnp.where(lo, 0, dq_s)], axis=2).astype(BF)
    o_diff_s = _diff_sample(dq_s, cache_diff_l, page_table, pad8(drows_s), lam_vec, out_gain, ts=ts, lam_init=lam_init)
    o_diff_s = o_diff_s[:, :, :ts].transpose(0, 2, 1, 3).reshape(nb * ts, nh * hd)

    mix = jnp.concatenate([jnp.concatenate([o_nsa_p, o_diff_p], axis=1),
                           jnp.concatenate([o_nsa_s, o_diff_s], axis=1).astype(BF)], axis=0)
    x = _mm(mix, w_out, res=x, tm=tm, tn=1024, name="att_out")

    rows_p = rows[:n_prompt].reshape(batch, seq, 4, ng, hd)
    win_p = win[:n_prompt].reshape(batch, seq, 2, ng, hd)
    keep_p = min(WINDOW, seq)
    win_new_s = win_s.reshape(nb, ts, 2, ng, hd)
    keep_s = win_state_l.shape[1]
    state = (rows_p, rows_s.reshape(nb, ts, 4, ng, hd),
             win_p[:, seq - keep_p:], jnp.concatenate([win_state_l, win_new_s], axis=1)[:, -keep_s:],
             drows[:n_prompt].reshape(batch, seq, 2, nh, hd), drows_s.reshape(nb, ts, 2, nh, hd))
    return x, state


def _pool_conv_layer(x, n_prompt, batch, seq, nb, ts, past, pool_hist, conv_hist,
                     norm_g, w_in, w_out, pool_w, pool_scale, conv_w, *, tm):
    cw = pool_scale.shape[0]
    z = _mm(x, w_in, gain=norm_g, tm=tm, tn=1024, name="pc_in")
    mix_p = _pc_prompt(z, pool_w, pool_scale, conv_w, batch=batch, seq=seq, ts=256)

    zs = z[n_prompt:].reshape(nb, ts, 4 * cw)
    p_s = zs[..., :cw]
    u_s = zs[..., 2 * cw:3 * cw] * zs[..., 3 * cw:]
    tpad = 8
    front = lambda hist: jnp.pad(hist, ((0, 0), (HALO - hist.shape[1], 0), (0, 0)))
    back = lambda a: jnp.pad(a, ((0, 0), (0, tpad - ts), (0, 0)))
    p_ext = jnp.concatenate([front(pool_hist), back(p_s)], axis=1)
    u_ext = jnp.concatenate([front(conv_hist), back(u_s)], axis=1)
    mix_s = _pc_sample(p_ext, u_ext, back(zs[..., cw:2 * cw]), pool_w, pool_scale, conv_w, past=past, tb=8)
    mix = jnp.concatenate([mix_p, mix_s[:, :ts].reshape(nb * ts, 2 * cw)], axis=0)
    x = _mm(mix, w_out, res=x, tm=tm, tn=1024, name="pc_out")

    tail = jnp.stack([z[(b + 1) * seq - POOL_HIST:(b + 1) * seq] for b in range(batch)])
    ct = tail[:, POOL_HIST - (CONV_K - 1):]
    u_tail = ct[..., 2 * cw:3 * cw] * ct[..., 3 * cw:]
    state = (tail[..., :cw],
             jnp.concatenate([pool_hist, p_s], axis=1)[:, -POOL_HIST:],
             u_tail,
             jnp.concatenate([conv_hist, u_s], axis=1)[:, -(CONV_K - 1):])
    return x, state


def kernel(x_prompt, x_sample, cache_nsa, cache_diff, state_nsa_win, state_pool, state_conv, page_table,
           att_norm_g, att_w_in, att_w_out, nsa_q_gain, nsa_k_gain, nsa_cmp_w, diff_q_gain, diff_k_gain,
           diff_lambda, diff_out_gain, pc_norm_g, pc_w_in, pc_w_out, pool_w, pool_scale, conv_w,
           ffn_norm_g, peer_wq, peer_keys, peer_u, peer_v):
    batch, seq, d = x_prompt.shape
    nb, ts, _ = x_sample.shape
    n_prompt = batch * seq
    n = n_prompt + nb * ts
    depth = ffn_norm_g.shape[0]
    past = page_table.shape[1] * cache_nsa.shape[2]
    tm = n // 8
    assert n % 8 == 0 and tm % 16 == 0
    x = jnp.concatenate([x_prompt.reshape(n_prompt, d), x_sample.reshape(nb * ts, d)], axis=0)
    u_bf, v_bf = peer_u.astype(BF), peer_v.astype(BF)
    att_states, pc_states = [], []
    for li in range(depth):
        i = li // 2
        if li % 2 == 0:
            x, st = _attention_layer(
                x, li, n_prompt, batch, seq, nb, ts, cache_nsa[i], cache_diff[i], state_nsa_win[i], page_table,
                att_norm_g[i], att_w_in[i], att_w_out[i], nsa_q_gain[i], nsa_k_gain[i], nsa_cmp_w[i],
                diff_q_gain[i], diff_k_gain[i], diff_lambda[i], diff_out_gain[i], tm=tm)
            att_states.append(st)
        else:
            x, st = _pool_conv_layer(
                x, n_prompt, batch, seq, nb, ts, past, state_pool[i], state_conv[i],
                pc_norm_g[i], pc_w_in[i], pc_w_out[i], pool_w[i], pool_scale[i], conv_w[i], tm=tm)
            pc_states.append(st)
        x = _peer_ffn(x, ffn_norm_g[li], peer_wq, peer_keys, u_bf, v_bf, li, tm=tm)
    stack = lambda states, k: jnp.stack([s[k] for s in states])
    return (x[:n_prompt].reshape(batch, seq, d), x[n_prompt:].reshape(nb, ts, d),
            stack(att_states, 0), stack(att_states, 1), stack(att_states, 2), stack(att_states, 3),
            stack(att_states, 4), stack(att_states, 5),
            stack(pc_states, 0), stack(pc_states, 1), stack(pc_states, 2), stack(pc_states, 3))
```
